```python
import functools
import jax, jax.numpy as jnp
from jax import lax
import numpy as np

D_MODEL = 1024
BATCH = 2
SEQ = 8192
DEPTH = 2

GRID_W = 64
CTX_LEN = 256
N_BRANCH = 4
BRANCH_W = 512
EPS = 1e-6

NA_HEADS = 8
NA_HEAD_DIM = BRANCH_W // NA_HEADS
NA_ROWS = 8
NA_COLS = 16

POOL_WINDOWS = (2, 4, 8, 16)
POOL_GROUP = BRANCH_W // len(POOL_WINDOWS)

MLA_HEADS = 8
MLA_Q_LORA = 384
MLA_KV_LORA = 256
MLA_NOPE = 64
MLA_ROPE = 32
MLA_V = BRANCH_W // MLA_HEADS
MLA_BLOCK = 128
ROPE_BASE = 10000.0

CONV_W = BRANCH_W
CONV_K = 3

D_FF = 2816
N_EXPERTS = 8
TOP_K = 2
D_FF_EXPERT = 3584
N_DENSE = (DEPTH + 1) // 2
N_MOE = DEPTH // 2

IN_SIZES = (BRANCH_W, BRANCH_W, BRANCH_W,
            BRANCH_W,
            MLA_Q_LORA, MLA_KV_LORA, MLA_ROPE,
            CONV_W, CONV_W, CONV_W,
            N_BRANCH * D_MODEL)
IN_COLS = sum(IN_SIZES)
IN_SPLITS = tuple(int(s) for s in np.cumsum(IN_SIZES)[:-1])

kernel_name = 'hybrid_parallel_branch_dit_block'


def rms_norm(x, g):
    xf = x.astype(jnp.float32)
    xf = xf * lax.rsqrt(jnp.mean(jnp.square(xf), axis=-1, keepdims=True) + EPS)
    return (xf * g.astype(jnp.float32)).astype(x.dtype)


def modulate(h, shift, scale):
    return h * (1 + scale) + shift


def softmax_f32(s, dtype):
    return jax.nn.softmax(s.astype(jnp.float32), axis=-1).astype(dtype)


def axial_rope_tables(n_tokens):
    pos = jnp.arange(n_tokens)
    row = (pos // GRID_W).astype(jnp.float32)
    col = (pos % GRID_W).astype(jnp.float32)
    n_pairs = MLA_ROPE // 4
    inv_freq = ROPE_BASE ** (-jnp.arange(n_pairs, dtype=jnp.float32) / n_pairs)
    ang = jnp.concatenate([row[:, None] * inv_freq, col[:, None] * inv_freq], axis=-1)
    return jnp.cos(ang), jnp.sin(ang)


def apply_rope(t, cos, sin):
    tp = t.astype(jnp.float32).reshape(t.shape[:-1] + (t.shape[-1] // 2, 2))
    a, b = tp[..., 0], tp[..., 1]
    out = jnp.stack([a * cos - b * sin, a * sin + b * cos], axis=-1)
    return out.reshape(t.shape).astype(t.dtype)


def na_heads(t):
    return t.reshape(t.shape[0], t.shape[1], NA_HEADS, NA_HEAD_DIM)


def neighbourhood_attention(q, k, v, k_ctx, v_ctx, rpb):
    B, S, H, Dh = q.shape
    rows = S // GRID_W
    kr = min(NA_ROWS, rows)
    n_loc = kr * NA_COLS
    scale = Dh ** -0.5
    r = jnp.arange(rows)
    col = jnp.arange(GRID_W)
    key_rows = jnp.clip(r - kr // 2, 0, rows - kr)[:, None] + jnp.arange(kr)
    key_cols = jnp.clip(col - NA_COLS // 2, 0, GRID_W - NA_COLS)[:, None] + jnp.arange(NA_COLS)
    d_row = key_rows - r[:, None] + (NA_ROWS - 1)
    d_col = key_cols - col[:, None] + (NA_COLS - 1)
    bias_col = rpb[:, :, d_col]
    q_rows = q.reshape(B, rows, GRID_W, H, Dh).transpose(1, 0, 2, 3, 4)

    def row_block(args):
        q_r, rows_r, drow_r = args
        idx = (rows_r[None, :, None] * GRID_W + key_cols[:, None, :]).reshape(GRID_W, n_loc)
        k_g = k[:, idx]
        v_g = v[:, idx]
        bias = bias_col[:, drow_r].transpose(0, 2, 1, 3).reshape(H, GRID_W, n_loc)
        s_loc = jnp.einsum('bqhd,bqnhd->bhqn', q_r, k_g) * scale + bias
        s_ctx = jnp.einsum('bqhd,bchd->bhqc', q_r, k_ctx) * scale
        p = softmax_f32(jnp.concatenate([s_loc, s_ctx], axis=-1), v.dtype)
        return (jnp.einsum('bhqn,bqnhd->bqhd', p[..., :n_loc], v_g)
                + jnp.einsum('bhqc,bchd->bqhd', p[..., n_loc:], v_ctx))

    out = lax.map(row_block, (q_rows, key_rows, d_row))
    return out.transpose(1, 0, 2, 3, 4).reshape(B, S, H * Dh)


def context_attention(q, k, v):
    s = jnp.einsum('bqhd,bkhd->bhqk', q, k) * (q.shape[-1] ** -0.5)
    p = softmax_f32(s, v.dtype)
    return jnp.einsum('bhqk,bkhd->bqhd', p, v).reshape(q.shape[0], q.shape[1], -1)


def multiscale_pool(u):
    B, N, _ = u.shape
    uf = u.astype(jnp.float32)
    csum = jnp.pad(jnp.cumsum(uf, axis=1), ((0, 0), (1, 0), (0, 0)))
    t = jnp.arange(N)
    outs = []
    for g, w in enumerate(POOL_WINDOWS):
        lo = jnp.clip(t - w // 2, 0, N)
        hi = jnp.clip(t - w // 2 + w, 0, N)
        sl = slice(g * POOL_GROUP, (g + 1) * POOL_GROUP)
        cnt = (hi - lo).astype(jnp.float32)[None, :, None]
        outs.append((csum[:, hi, sl] - csum[:, lo, sl]) / cnt - uf[:, :, sl])
    return jnp.concatenate(outs, axis=-1).astype(u.dtype)


def pool_branch(u, pool_w, pool_scale):
    B, N, _ = u.shape
    pooled = multiscale_pool(u).reshape(B, N, len(POOL_WINDOWS), POOL_GROUP)
    mixed = jnp.einsum('bngc,gcd->bngd', pooled, pool_w).reshape(B, N, BRANCH_W)
    return mixed * pool_scale


def mla_queries(c_q, g_q, w_uq):
    B, N, _ = c_q.shape
    q = (rms_norm(c_q, g_q) @ w_uq).reshape(B, N, MLA_HEADS, MLA_NOPE + MLA_ROPE)
    return q[..., :MLA_NOPE], q[..., MLA_NOPE:]


def mla_keys_values(c_kv, g_kv, w_ukv):
    B, N, _ = c_kv.shape
    kv = (rms_norm(c_kv, g_kv) @ w_ukv).reshape(B, N, MLA_HEADS, MLA_NOPE + MLA_V)
    return kv[..., :MLA_NOPE], kv[..., MLA_NOPE:]


def mla_scores(qn, qr, kn, kr):
    s = jnp.einsum('bqhd,bkhd->bhqk', qn, kn) + jnp.einsum('bqhr,bkr->bhqk', qr, kr)
    return s * ((MLA_NOPE + MLA_ROPE) ** -0.5)


def mla_latent_attention(qn, qr, kn, kr, v, kn_c, kr_c, v_c):
    B, S, H, _ = qn.shape
    nb = S // MLA_BLOCK

    def block(args):
        qn_b, qr_b = args
        s = jnp.concatenate([mla_scores(qn_b, qr_b, kn, kr), mla_scores(qn_b, qr_b, kn_c, kr_c)], axis=-1)
        p = softmax_f32(s, v.dtype)
        return (jnp.einsum('bhqk,bkhd->bqhd', p[..., :S], v)
                + jnp.einsum('bhqk,bkhd->bqhd', p[..., S:], v_c))

    to_blocks = lambda t: t.reshape((B, nb, MLA_BLOCK) + t.shape[2:]).swapaxes(0, 1)
    out = lax.map(block, (to_blocks(qn), to_blocks(qr)))
    return out.swapaxes(0, 1).reshape(B, S, H * MLA_V)


def mla_context_attention(qn, qr, kn, kr, v):
    p = softmax_f32(mla_scores(qn, qr, kn, kr), v.dtype)
    return jnp.einsum('bhqk,bkhd->bqhd', p, v).reshape(qn.shape[0], qn.shape[1], -1)


def short_conv(u, w):
    return lax.conv_general_dilated(
        u, w[:, None, :].astype(u.dtype), window_strides=(1,),
        padding=((CONV_K // 2, CONV_K // 2),),
        dimension_numbers=('NWC', 'WIO', 'NWC'), feature_group_count=u.shape[-1])


def merge_branches(outs, gate_pre, w_branch):
    o = jnp.stack(outs, axis=-2)
    proj = jnp.einsum('bnko,kod->bnkd', o, w_branch)
    g = jax.nn.sigmoid(gate_pre.reshape(gate_pre.shape[:-1] + (N_BRANCH, D_MODEL)))
    return jnp.sum(g * proj, axis=-2)


def token_mixer(h, hc, cos, sin, w_in, rpb, pool_w, pool_scale, g_q, w_uq, g_kv, w_ukv,
                conv_w, w_branch, w_out, with_ctx_out):
    (q_a, k_a, v_a, u_p, c_q, c_kv, k_r, b_g, c_g, x_c, gate) = jnp.split(h @ w_in, IN_SPLITS, axis=-1)
    (q_ac, k_ac, v_ac, u_pc, c_qc, c_kvc, k_rc, b_gc, c_gc, x_cc, gate_c) = jnp.split(hc @ w_in, IN_SPLITS, axis=-1)
    k_na_c, v_na_c = na_heads(k_ac), na_heads(v_ac)
    kn_c, vm_c = mla_keys_values(c_kvc, g_kv, w_ukv)

    o_a = neighbourhood_attention(na_heads(q_a), na_heads(k_a), na_heads(v_a), k_na_c, v_na_c, rpb)
    o_b = pool_branch(u_p, pool_w, pool_scale)
    qn, qr = mla_queries(c_q, g_q, w_uq)
    kn, vm = mla_keys_values(c_kv, g_kv, w_ukv)
    o_c = mla_latent_attention(qn, apply_rope(qr, cos[:, None], sin[:, None]), kn,
                               apply_rope(k_r, cos, sin), vm, kn_c, k_rc, vm_c)
    o_d = b_g * short_conv(c_g * x_c, conv_w)
    y = merge_branches((o_a, o_b, o_c, o_d), gate, w_branch) @ w_out
    if not with_ctx_out:
        return y, None

    o_ac = context_attention(na_heads(q_ac), k_na_c, v_na_c)
    o_bc = pool_branch(u_pc, pool_w, pool_scale)
    qn_c, qr_c = mla_queries(c_qc, g_q, w_uq)
    o_cc = mla_context_attention(qn_c, qr_c, kn_c, k_rc, vm_c)
    o_dc = b_gc * short_conv(c_gc * x_cc, conv_w)
    yc = merge_branches((o_ac, o_bc, o_cc, o_dc), gate_c, w_branch) @ w_out
    return y, yc


def swiglu(t, w1, w3, w2):
    return (jax.nn.silu(t @ w1) * (t @ w3)) @ w2


def moe_swiglu(t, router, w1, w3, w2):
    logits = (t @ router).astype(jnp.float32)
    top_v, top_i = lax.top_k(logits, TOP_K)
    wts = jax.nn.softmax(top_v, axis=-1)
    combine = jnp.sum(jax.nn.one_hot(top_i, N_EXPERTS, dtype=jnp.float32) * wts[..., None], axis=-2).astype(t.dtype)
    y = jnp.zeros_like(t)
    for e in range(N_EXPERTS):
        y = y + combine[..., e:e + 1] * swiglu(t, w1[e], w3[e], w2[e])
    return y


def setup_inputs(seed: int = 0) -> dict:
    key = jax.random.key(seed)
    ks = jax.random.split(key, 32)
    nrm = lambda k, shape, fan_in: jax.random.normal(k, shape, jnp.float32) * (fan_in ** -0.5)
    gain = lambda k, shape: 1.0 + 0.05 * jax.random.normal(k, shape, jnp.float32)
    return {
        'x': jax.random.normal(ks[0], (BATCH, SEQ, D_MODEL), jnp.float32),
        'c': jax.random.normal(ks[1], (BATCH, D_MODEL), jnp.float32),
        'ctx': jax.random.normal(ks[2], (BATCH, CTX_LEN, D_MODEL), jnp.float32),
        'c_ctx': jax.random.normal(ks[3], (D_MODEL,), jnp.float32),
        'w_ada': 0.5 * nrm(ks[4], (DEPTH, D_MODEL, 6 * D_MODEL), D_MODEL),
        'b_ada': 0.02 * jax.random.normal(ks[5], (DEPTH, 6 * D_MODEL), jnp.float32),
        'g_norm': gain(ks[6], (DEPTH, 4, D_MODEL)),
        'w_in': nrm(ks[7], (DEPTH, D_MODEL, IN_COLS), D_MODEL),
        'na_rpb': 0.1 * jax.random.normal(ks[8], (DEPTH, NA_HEADS, 2 * NA_ROWS - 1, 2 * NA_COLS - 1), jnp.float32),
        'pool_w': nrm(ks[9], (DEPTH, len(POOL_WINDOWS), POOL_GROUP, POOL_GROUP), POOL_GROUP),
        'pool_scale': gain(ks[10], (DEPTH, BRANCH_W)),
        'mla_g_q': gain(ks[11], (DEPTH, MLA_Q_LORA)),
        'mla_w_uq': nrm(ks[12], (DEPTH, MLA_Q_LORA, MLA_HEADS * (MLA_NOPE + MLA_ROPE)), MLA_Q_LORA),
        'mla_g_kv': gain(ks[13], (DEPTH, MLA_KV_LORA)),
        'mla_w_ukv': nrm(ks[14], (DEPTH, MLA_KV_LORA, MLA_HEADS * (MLA_NOPE + MLA_V)), MLA_KV_LORA),
        'conv_w': nrm(ks[15], (DEPTH, CONV_K, CONV_W), CONV_K),
        'w_branch': nrm(ks[16], (DEPTH, N_BRANCH, BRANCH_W, D_MODEL), BRANCH_W),
        'w_out': nrm(ks[17], (DEPTH, D_MODEL, D_MODEL), D_MODEL),
        'ffn_w1': nrm(ks[18], (N_DENSE, D_MODEL, D_FF), D_MODEL),
        'ffn_w3': nrm(ks[19], (N_DENSE, D_MODEL, D_FF), D_MODEL),
        'ffn_w2': nrm(ks[20], (N_DENSE, D_FF, D_MODEL), D_FF),
        'moe_router': nrm(ks[21], (N_MOE, D_MODEL, N_EXPERTS), D_MODEL),
        'moe_w1': nrm(ks[22], (N_MOE, N_EXPERTS, D_MODEL, D_FF_EXPERT), D_MODEL),
        'moe_w3': nrm(ks[23], (N_MOE, N_EXPERTS, D_MODEL, D_FF_EXPERT), D_MODEL),
        'moe_w2': nrm(ks[24], (N_MOE, N_EXPERTS, D_FF_EXPERT, D_MODEL), D_FF_EXPERT),
    }


def reference(x, c, ctx, c_ctx, w_ada, b_ada, g_norm, w_in, na_rpb, pool_w, pool_scale,
              mla_g_q, mla_w_uq, mla_g_kv, mla_w_ukv, conv_w, w_branch, w_out,
              ffn_w1, ffn_w3, ffn_w2, moe_router, moe_w1, moe_w3, moe_w2):
    B, S, _ = x.shape
    cos, sin = axial_rope_tables(S)
    cx = ctx
    for l in range(DEPTH):
        ctx_needed = l < DEPTH - 1
        mod = (jax.nn.silu(c) @ w_ada[l] + b_ada[l]).reshape(B, 6, 1, D_MODEL)
        mod_c = (jax.nn.silu(c_ctx) @ w_ada[l] + b_ada[l]).reshape(6, D_MODEL)

        h = modulate(rms_norm(x, g_norm[l, 0]), mod[:, 0], mod[:, 1])
        hc = modulate(rms_norm(cx, g_norm[l, 0]), mod_c[0], mod_c[1])
        y, yc = token_mixer(h, hc, cos, sin, w_in[l], na_rpb[l], pool_w[l], pool_scale[l],
                            mla_g_q[l], mla_w_uq[l], mla_g_kv[l], mla_w_ukv[l], conv_w[l],
                            w_branch[l], w_out[l], ctx_needed)
        x = x + mod[:, 2] * rms_norm(y, g_norm[l, 1])

        if l % 2 == 0:
            ffn = functools.partial(swiglu, w1=ffn_w1[l // 2], w3=ffn_w3[l // 2], w2=ffn_w2[l // 2])
        else:
            ffn = functools.partial(moe_swiglu, router=moe_router[l // 2], w1=moe_w1[l // 2],
                                    w3=moe_w3[l // 2], w2=moe_w2[l // 2])
        x = x + mod[:, 5] * rms_norm(ffn(modulate(rms_norm(x, g_norm[l, 2]), mod[:, 3], mod[:, 4])), g_norm[l, 3])

        if ctx_needed:
            cx = cx + mod_c[2] * rms_norm(yc, g_norm[l, 1])
            cx = cx + mod_c[5] * rms_norm(ffn(modulate(rms_norm(cx, g_norm[l, 2]), mod_c[3], mod_c[4])), g_norm[l, 3])
    return x
```

```python
import functools

import numpy as np
import jax
import jax.numpy as jnp
from jax import lax
from jax.experimental import pallas as pl
from jax.experimental.pallas import tpu as pltpu

F32 = jnp.float32
BF16 = jnp.bfloat16

GRID_W = 64
N_BRANCH = 4
BRANCH_W = 512
EPS = 1e-6
NA_HEADS = 8
NA_HEAD_DIM = 64
NA_ROWS = 8
NA_COLS = 16
POOL_WINDOWS = (2, 4, 8, 16)
POOL_GROUP = 128
MLA_HEADS = 8
MLA_Q_LORA = 384
MLA_KV_LORA = 256
MLA_NOPE = 64
MLA_ROPE = 32
MLA_V = 64
ROPE_BASE = 10000.0
CONV_K = 3
N_EXPERTS = 8
TOP_K = 2

IN_SIZES = (512, 512, 512, 512, MLA_Q_LORA, MLA_KV_LORA, MLA_ROPE, 512, 512, 512, 4096)
IN_SPLITS = tuple(int(s) for s in np.cumsum(IN_SIZES)[:-1])

OFF_GATE = 0
OFF_QA = 4096
OFF_KA = 4608
OFF_VA = 5120
OFF_UP = 5632
OFF_BG = 6144
OFF_CG = 6656
OFF_XC = 7168
OFF_CQ = 7680
OFF_CKV = 8192
OFF_KR = 8448
N_IN = 8704

NA_SCALE = NA_HEAD_DIM ** -0.5
MLA_SCALE = (MLA_NOPE + MLA_ROPE) ** -0.5
NEG = -1e30

VMEM_LIMIT = 52 * 1024 * 1024
TM = 512
TL = 256
HALO = 16


def _cparams(sem):
    return pltpu.CompilerParams(dimension_semantics=sem, vmem_limit_bytes=VMEM_LIMIT)


def _rms(xf, g):
    ms = jnp.mean(xf * xf, axis=-1, keepdims=True)
    return xf * lax.rsqrt(ms + EPS) * g


def _dot(a, b):
    return jnp.dot(a, b, preferred_element_type=F32)


def _dot_nt(a, b):
    return lax.dot_general(a, b, (((1,), (1,)), ((), ())), preferred_element_type=F32)


def _adaln_kernel(c_ref, w_ref, b_ref, o_ref):
    c = c_ref[...]
    sc = c * jax.nn.sigmoid(c)
    o_ref[...] = jnp.dot(sc, w_ref[...], preferred_element_type=F32,
                         precision=lax.Precision.HIGHEST) + b_ref[...]


def _adaln(c8, w, b):
    d = c8.shape[1]
    n = w.shape[1]
    tn = 1536
    return pl.pallas_call(
        _adaln_kernel,
        grid=(n // tn,),
        in_specs=[pl.BlockSpec((8, d), lambda j: (0, 0)),
                  pl.BlockSpec((d, tn), lambda j: (0, j)),
                  pl.BlockSpec((1, tn), lambda j: (0, j))],
        out_specs=pl.BlockSpec((8, tn), lambda j: (0, j)),
        out_shape=jax.ShapeDtypeStruct((8, n), F32),
        compiler_params=_cparams(("arbitrary",)),
        name="adaln",
    )(c8, w, b.reshape(1, n))


def _inproj_kernel(x_ref, g_ref, sh_ref, sc_ref, w_ref, o_ref, h_ref):
    @pl.when(pl.program_id(1) == 0)
    def _():
        h = _rms(x_ref[...], g_ref[...]) * (1.0 + sc_ref[...]) + sh_ref[...]
        h_ref[...] = h.astype(BF16)

    o_ref[...] = _dot(h_ref[...], w_ref[...]).astype(BF16)


def _inproj(xall, g, shift, scale, w_p, tiles_per_batch):
    r, d = xall.shape
    n = w_p.shape[1]
    tn = 512
    mod_spec = pl.BlockSpec((None, 1, d), lambda i, j: (i // tiles_per_batch, 0, 0))
    return pl.pallas_call(
        _inproj_kernel,
        grid=(r // TM, n // tn),
        in_specs=[pl.BlockSpec((TM, d), lambda i, j: (i, 0)),
                  pl.BlockSpec((1, d), lambda i, j: (0, 0)),
                  mod_spec, mod_spec,
                  pl.BlockSpec((d, tn), lambda i, j: (0, j))],
        out_specs=pl.BlockSpec((TM, tn), lambda i, j: (i, j)),
        out_shape=jax.ShapeDtypeStruct((r, n), BF16),
        scratch_shapes=[pltpu.VMEM((TM, d), BF16)],
        compiler_params=_cparams(("parallel", "arbitrary")),
        name="inproj",
    )(xall, g.reshape(1, d), shift, scale, w_p)


def _mla_prep_kernel(cq_ref, ckv_ref, kr_ref, gq_ref, gkv_ref, wq_ref, wkn_ref, wv_ref,
                     cos_ref, sin_ref, q_out, k_out, v_out):
    cos = cos_ref[...]
    sin = sin_ref[...]

    def rope(t):
        return t * cos + pltpu.roll(t, 96, 1) * sin

    cq = cq_ref[...].astype(F32)
    ms = jnp.sum(cq * cq, axis=-1, keepdims=True) * (1.0 / MLA_Q_LORA)
    cqn = (cq * lax.rsqrt(ms + EPS) * gq_ref[...]).astype(BF16)
    q = _dot(cqn, wq_ref[...])
    for h in range(MLA_HEADS):
        sl = slice(128 * h, 128 * (h + 1))
        q_out[:, sl] = rope(q[:, sl] * MLA_SCALE).astype(BF16)

    ckvn = _rms(ckv_ref[...].astype(F32), gkv_ref[...]).astype(BF16)
    kn = _dot(ckvn, wkn_ref[...])
    krt = rope(kr_ref[...].astype(F32))
    for h in range(MLA_HEADS):
        sl = slice(128 * h, 128 * (h + 1))
        k_out[:, sl] = (kn[:, sl] + krt).astype(BF16)
    v_out[...] = _dot(ckvn, wv_ref[...]).astype(BF16)


def _mla_prep(proj, gq, gkv, wq, wkn, wv, cos_t, sin_t):
    r = proj.shape[0]
    full = lambda shape: pl.BlockSpec(shape, lambda i: (0, 0))
    return pl.pallas_call(
        _mla_prep_kernel,
        grid=(r // TM,),
        in_specs=[pl.BlockSpec((TM, 512), lambda i: (i, OFF_CQ // 512)),
                  pl.BlockSpec((TM, 256), lambda i: (i, OFF_CKV // 256)),
                  pl.BlockSpec((TM, 128), lambda i: (i, OFF_KR // 128)),
                  full((1, 512)), full((1, 256)),
                  full((512, 1024)), full((256, 1024)), full((256, 512)),
                  pl.BlockSpec((TM, 128), lambda i: (i, 0)),
                  pl.BlockSpec((TM, 128), lambda i: (i, 0))],
        out_specs=[pl.BlockSpec((TM, 1024), lambda i: (i, 0)),
                   pl.BlockSpec((TM, 1024), lambda i: (i, 0)),
                   pl.BlockSpec((TM, 512), lambda i: (i, 0))],
        out_shape=[jax.ShapeDtypeStruct((r, 1024), BF16),
                   jax.ShapeDtypeStruct((r, 1024), BF16),
                   jax.ShapeDtypeStruct((r, 512), BF16)],
        compiler_params=_cparams(("parallel",)),
        name="mla_prep",
    )(proj, proj, proj, gq, gkv, wq, wkn, wv, cos_t, sin_t)


def _pick_heads(o0, o1):
    lane = lax.broadcasted_iota(jnp.int32, o0.shape, 1)
    return jnp.where(lane < 64, o0, o1)


def _mla_attn_kernel(q_ref, k_ref, v_ref, kc_ref, vc_ref, o_ref, *, tk):
    nk = k_ref.shape[0] // tk
    q = q_ref[...]
    qs = (q[:, 0:128], q[:, 128:256])
    vc = vc_ref[...]

    init = []
    for h in range(2):
        s = _dot_nt(qs[h], kc_ref[:, 128 * h:128 * (h + 1)])
        m = jnp.max(s, axis=-1, keepdims=True)
        p = jnp.exp(s - m)
        init += [m, jnp.sum(p, axis=-1, keepdims=True), _dot(p.astype(BF16), vc)]

    def body(kb, carry):
        off = pl.multiple_of(kb * tk, tk)
        v = v_ref[pl.ds(off, tk), :]
        out = []
        for h in range(2):
            m, l, acc = carry[3 * h:3 * h + 3]
            k = k_ref[pl.ds(off, tk), 128 * h:128 * (h + 1)]
            s = _dot_nt(qs[h], k)
            m_new = jnp.maximum(m, jnp.max(s, axis=-1, keepdims=True))
            alpha = jnp.exp(m - m_new)
            p = jnp.exp(s - m_new)
            l = alpha * l + jnp.sum(p, axis=-1, keepdims=True)
            acc = alpha * acc + _dot(p.astype(BF16), v)
            out += [m_new, l, acc]
        return tuple(out)

    m0, l0, a0, m1, l1, a1 = lax.fori_loop(0, nk, body, tuple(init))
    o_ref[...] = _pick_heads(a0 / l0, a1 / l1).astype(BF16)


def _mla_attn(q, k, v, nb, s_len, c_len):
    tq = min(512, s_len)
    tk = min(512, s_len)
    nq = s_len // tq
    ctx_blk = nb * s_len // c_len
    return pl.pallas_call(
        functools.partial(_mla_attn_kernel, tk=tk),
        grid=(nb, 4, nq),
        in_specs=[pl.BlockSpec((tq, 256), lambda b, hp, i: (b * nq + i, hp)),
                  pl.BlockSpec((s_len, 256), lambda b, hp, i: (b, hp)),
                  pl.BlockSpec((s_len, 128), lambda b, hp, i: (b, hp)),
                  pl.BlockSpec((c_len, 256), lambda b, hp, i: (ctx_blk + b, hp)),
                  pl.BlockSpec((c_len, 128), lambda b, hp, i: (ctx_blk + b, hp))],
        out_specs=pl.BlockSpec((tq, 128), lambda b, hp, i: (b * nq + i, hp)),
        out_shape=jax.ShapeDtypeStruct((nb * s_len, 512), BF16),
        compiler_params=_cparams(("parallel", "parallel", "arbitrary")),
        name="mla_attn",
    )(q, k, v, k, v)


def _ctx_attn_kernel(q_ref, k_ref, v_ref, o_ref, *, dqk):
    v = v_ref[...]
    outs = []
    for h in range(2):
        sl = slice(dqk * h, dqk * (h + 1))
        s = _dot_nt(q_ref[:, sl], k_ref[:, sl])
        m = jnp.max(s, axis=-1, keepdims=True)
        p = jnp.exp(s - m)
        l = jnp.sum(p, axis=-1, keepdims=True)
        outs.append(_dot(p.astype(BF16), v) / l)
    o_ref[...] = _pick_heads(outs[0], outs[1]).astype(BF16)


def _ctx_attn(q, k, v, nb, c_len, row_blk0, dqk, qcol, kcol, vcol, name):
    w = 2 * dqk
    return pl.pallas_call(
        functools.partial(_ctx_attn_kernel, dqk=dqk),
        grid=(nb, 4),
        in_specs=[pl.BlockSpec((c_len, w), lambda b, hp: (row_blk0 + b, qcol // w + hp)),
                  pl.BlockSpec((c_len, w), lambda b, hp: (row_blk0 + b, kcol // w + hp)),
                  pl.BlockSpec((c_len, 128), lambda b, hp: (row_blk0 + b, vcol // 128 + hp))],
        out_specs=pl.BlockSpec((c_len, 128), lambda b, hp: (b, hp)),
        out_shape=jax.ShapeDtypeStruct((nb * c_len, 512), BF16),
        compiler_params=_cparams(("parallel", "parallel")),
        name=name,
    )(q, k, v)


def _na_attn_kernel(q_ref, k_ref, v_ref, kc_ref, vc_ref, bias_ref, o_ref, *, rows):
    kr = min(NA_ROWS, rows)
    nloc = kr * GRID_W
    vc = vc_ref[...]

    def body(r, carry):
        start = jnp.clip(r - kr // 2, 0, rows - kr)
        pat = start - r + (NA_ROWS - 1)
        qoff = pl.multiple_of(r * GRID_W, GRID_W)
        koff = pl.multiple_of(start * GRID_W, GRID_W)
        q = q_ref[pl.ds(qoff, GRID_W), :]
        kw = k_ref[pl.ds(koff, nloc), :]
        vw = v_ref[pl.ds(koff, nloc), :]
        outs = []
        for h in range(2):
            sl = slice(64 * h, 64 * (h + 1))
            qh = q[:, sl]
            s = _dot_nt(qh, kw[:, sl]) + bias_ref[pat, h]
            sc = _dot_nt(qh, kc_ref[:, sl])
            m = jnp.maximum(jnp.max(s, axis=-1, keepdims=True), jnp.max(sc, axis=-1, keepdims=True))
            p = jnp.exp(s - m)
            pc = jnp.exp(sc - m)
            l = jnp.sum(p, axis=-1, keepdims=True) + jnp.sum(pc, axis=-1, keepdims=True)
            outs.append((_dot(p.astype(BF16), vw) + _dot(pc.astype(BF16), vc)) / l)
        o_ref[pl.ds(qoff, GRID_W), :] = _pick_heads(outs[0], outs[1]).astype(BF16)
        return carry

    lax.fori_loop(0, rows, body, 0)


def _na_attn(proj, bias, nb, s_len, c_len):
    rows = s_len // GRID_W
    ctx_blk = nb * s_len // c_len
    nloc = min(NA_ROWS, rows) * GRID_W
    return pl.pallas_call(
        functools.partial(_na_attn_kernel, rows=rows),
        grid=(nb, 4),
        in_specs=[pl.BlockSpec((s_len, 128), lambda b, hp: (b, OFF_QA // 128 + hp)),
                  pl.BlockSpec((s_len, 128), lambda b, hp: (b, OFF_KA // 128 + hp)),
                  pl.BlockSpec((s_len, 128), lambda b, hp: (b, OFF_VA // 128 + hp)),
                  pl.BlockSpec((c_len, 128), lambda b, hp: (ctx_blk + b, OFF_KA // 128 + hp)),
                  pl.BlockSpec((c_len, 128), lambda b, hp: (ctx_blk + b, OFF_VA // 128 + hp)),
                  pl.BlockSpec((NA_ROWS, 2, GRID_W, nloc), lambda b, hp: (0, hp, 0, 0))],
        out_specs=pl.BlockSpec((s_len, 128), lambda b, hp: (b, hp)),
        out_shape=jax.ShapeDtypeStruct((nb * s_len, 512), BF16),
        compiler_params=_cparams(("parallel", "parallel")),
        name="na_attn",
    )(proj, proj, proj, proj, proj, bias)


def _na_bias_table(rpb, rows):
    kr = min(NA_ROWS, rows)
    col = jnp.arange(GRID_W)
    cstart = jnp.clip(col - NA_COLS // 2, 0, GRID_W - NA_COLS)
    kc = jnp.arange(GRID_W)
    valid = (kc[None, :] >= cstart[:, None]) & (kc[None, :] < cstart[:, None] + NA_COLS)
    dcol = jnp.clip(kc[None, :] - col[:, None] + (NA_COLS - 1), 0, 2 * NA_COLS - 2)
    tab = jnp.where(valid[None, None], rpb[:, :, dcol], NEG)
    ridx = jnp.arange(NA_ROWS)[:, None] + jnp.arange(kr)[None, :]
    t = tab[:, ridx]
    t = t.transpose(1, 0, 3, 2, 4)
    return t.reshape(NA_ROWS, NA_HEADS, GRID_W, kr * GRID_W).astype(F32)


def _local_kernel(up_p, up_c, up_n, cg_p, cg_c, cg_n, xc_p, xc_c, xc_n, bg_ref,
                  pw_ref, ps_ref, cw_ref, ob_ref, od_ref, *, lat_tiles, tiles_lat_seq, s_len, c_len):
    i = pl.program_id(0)
    is_lat = i < lat_tiles
    j = jnp.where(is_lat, i % tiles_lat_seq, 0)
    n_seq = jnp.where(is_lat, s_len, c_len)
    first = j == 0
    last = (j + 1) * TL == n_seq

    def ext(p_ref, c_ref, n_ref):
        p = jnp.where(first, 0.0, p_ref[...].astype(F32))
        n = jnp.where(last, 0.0, n_ref[...].astype(F32))
        return jnp.concatenate([p, c_ref[...].astype(F32), n], axis=0)

    z = ext(cg_p, cg_c, cg_n) * ext(xc_p, xc_c, xc_n)
    cw = cw_ref[...]
    y = (cw[0:1] * z[HALO - 1:HALO - 1 + TL] + cw[1:2] * z[HALO:HALO + TL]
         + cw[2:3] * z[HALO + 1:HALO + 1 + TL])
    od_ref[...] = (bg_ref[...].astype(F32) * y).astype(BF16)

    u = ext(up_p, up_c, up_n)
    t = j * TL + lax.broadcasted_iota(jnp.int32, (TL, 1), 0)
    ps = ps_ref[...]
    for g, w in enumerate(POOL_WINDOWS):
        sl = slice(POOL_GROUP * g, POOL_GROUP * (g + 1))
        ug = u[:, sl]
        acc = ug[HALO - w // 2:HALO - w // 2 + TL]
        for d in range(-w // 2 + 1, w // 2):
            acc = acc + ug[HALO + d:HALO + d + TL]
        cnt = (jnp.minimum(t - w // 2 + w, n_seq) - jnp.maximum(t - w // 2, 0)).astype(F32)
        pooled = acc / cnt - ug[HALO:HALO + TL]
        mixed = _dot(pooled.astype(BF16), pw_ref[g])
        ob_ref[:, sl] = (mixed * ps[:, sl]).astype(BF16)


def _local(proj, pool_w, pool_scale, conv_w, nb, s_len, c_len):
    r = proj.shape[0]
    lat_tiles = nb * s_len // TL
    hpt = TL // HALO
    nhalo = r // HALO

    def cur(off):
        return pl.BlockSpec((TL, 512), lambda i: (i, off // 512))

    def prev(off):
        return pl.BlockSpec((HALO, 512), lambda i: (jnp.maximum(i * hpt - 1, 0), off // 512))

    def nxt(off):
        return pl.BlockSpec((HALO, 512), lambda i: (jnp.minimum((i + 1) * hpt, nhalo - 1), off // 512))

    specs = []
    for off in (OFF_UP, OFF_CG, OFF_XC):
        specs += [prev(off), cur(off), nxt(off)]
    specs += [cur(OFF_BG),
              pl.BlockSpec((4, POOL_GROUP, POOL_GROUP), lambda i: (0, 0, 0)),
              pl.BlockSpec((1, 512), lambda i: (0, 0)),
              pl.BlockSpec((CONV_K, 512), lambda i: (0, 0))]
    return pl.pallas_call(
        functools.partial(_local_kernel, lat_tiles=lat_tiles, tiles_lat_seq=s_len // TL,
                          s_len=s_len, c_len=c_len),
        grid=(r // TL,),
        in_specs=specs,
        out_specs=[pl.BlockSpec((TL, 512), lambda i: (i, 0)), pl.BlockSpec((TL, 512), lambda i: (i, 0))],
        out_shape=[jax.ShapeDtypeStruct((r, 512), BF16), jax.ShapeDtypeStruct((r, 512), BF16)],
        compiler_params=_cparams(("parallel",)),
        name="local_mix",
    )(*([proj] * 10), pool_w, pool_scale, conv_w)


def _merge_kernel(oa_ref, ob_ref, oc_ref, od_ref, gate_ref, wb_ref, wo_ref, x_ref, gm_ref, g_ref, o_ref):
    merged = None
    for k, o in enumerate((oa_ref, ob_ref, oc_ref, od_ref)):
        proj = _dot(o[...], wb_ref[k])
        gk = jax.nn.sigmoid(gate_ref[:, 1024 * k:1024 * (k + 1)].astype(F32))
        merged = gk * proj if merged is None else merged + gk * proj
    y = _dot(merged.astype(BF16), wo_ref[...])
    o_ref[...] = x_ref[...] + gm_ref[...] * _rms(y, g_ref[...])


def _merge(oa, ob, oc, od, proj, wb, wo, xall, gate_mod, g, n_rows, tiles_per_batch):
    d = xall.shape[1]
    tm = 256
    tpb = tiles_per_batch * (TM // tm)
    row = lambda w: pl.BlockSpec((tm, w), lambda i: (i, 0))
    return pl.pallas_call(
        _merge_kernel,
        grid=(n_rows // tm,),
        in_specs=[row(512), row(512), row(512), row(512), row(4096),
                  pl.BlockSpec((4, 512, d), lambda i: (0, 0, 0)),
                  pl.BlockSpec((d, d), lambda i: (0, 0)),
                  row(d),
                  pl.BlockSpec((None, 1, d), lambda i: (i // tpb, 0, 0)),
                  pl.BlockSpec((1, d), lambda i: (0, 0))],
        out_specs=row(d),
        out_shape=jax.ShapeDtypeStruct((n_rows, d), F32),
        compiler_params=_cparams(("parallel",)),
        name="merge",
    )(oa, ob, oc, od, proj, wb, wo, xall, gate_mod, g.reshape(1, d))


def _ffn_kernel(x_ref, g2_ref, sh_ref, sc_ref, gm_ref, g3_ref, w1_ref, w3_ref, w2_ref, o_ref, h_ref, acc_ref):
    f = pl.program_id(1)

    @pl.when(f == 0)
    def _():
        h = _rms(x_ref[...], g2_ref[...]) * (1.0 + sc_ref[...]) + sh_ref[...]
        h_ref[...] = h.astype(BF16)
        acc_ref[...] = jnp.zeros_like(acc_ref)

    h = h_ref[...]
    a = _dot(h, w1_ref[...])
    b = _dot(h, w3_ref[...])
    acc_ref[...] += _dot((a * jax.nn.sigmoid(a) * b).astype(BF16), w2_ref[...])

    @pl.when(f == pl.num_programs(1) - 1)
    def _():
        o_ref[...] = x_ref[...] + gm_ref[...] * _rms(acc_ref[...], g3_ref[...])


def _ffn(xall, g2, shift, scale, gate_mod, g3, w1, w3, w2, tiles_per_batch):
    r, d = xall.shape
    dff = w1.shape[1]
    tf = dff // 2
    mod = pl.BlockSpec((None, 1, d), lambda i, f: (i // tiles_per_batch, 0, 0))
    vec = pl.BlockSpec((1, d), lambda i, f: (0, 0))
    return pl.pallas_call(
        _ffn_kernel,
        grid=(r // TM, dff // tf),
        in_specs=[pl.BlockSpec((TM, d), lambda i, f: (i, 0)), vec, mod, mod, mod, vec,
                  pl.BlockSpec((d, tf), lambda i, f: (0, f)),
                  pl.BlockSpec((d, tf), lambda i, f: (0, f)),
                  pl.BlockSpec((tf, d), lambda i, f: (f, 0))],
        out_specs=pl.BlockSpec((TM, d), lambda i, f: (i, 0)),
        out_shape=jax.ShapeDtypeStruct((r, d), F32),
        scratch_shapes=[pltpu.VMEM((TM, d), BF16), pltpu.VMEM((TM, d), F32)],
        compiler_params=_cparams(("parallel", "arbitrary")),
        name="ffn_dense",
    )(xall, g2.reshape(1, d), shift, scale, gate_mod, g3.reshape(1, d), w1, w3, w2)


def _router_kernel(x_ref, g2_ref, sh_ref, sc_ref, rt_ref, t_ref, idx_ref, w_ref):
    t = _rms(x_ref[...], g2_ref[...]) * (1.0 + sc_ref[...]) + sh_ref[...]
    t_ref[...] = t
    logits = lax.dot_general(rt_ref[...], t, (((1,), (1,)), ((), ())), preferred_element_type=F32,
                             precision=lax.Precision.HIGHEST)
    e = lax.broadcasted_iota(jnp.int32, logits.shape, 0).astype(F32)
    m1 = jnp.max(logits, axis=0, keepdims=True)
    i1 = jnp.min(jnp.where(logits == m1, e, float(N_EXPERTS)), axis=0, keepdims=True)
    rest = jnp.where(e == i1, -jnp.inf, logits)
    m2 = jnp.max(rest, axis=0, keepdims=True)
    i2 = jnp.min(jnp.where(rest == m2, e, float(N_EXPERTS)), axis=0, keepdims=True)
    ex = jnp.exp(m2 - m1)
    w1 = 1.0 / (1.0 + ex)
    idx_ref[0:1, :] = i1.astype(jnp.int32)
    idx_ref[1:2, :] = i2.astype(jnp.int32)
    w_ref[0:1, :] = w1
    w_ref[1:2, :] = ex * w1


def _router(x, g2, shift, scale, router_t, tiles_per_batch):
    t_rows, d = x.shape
    mod = pl.BlockSpec((None, 1, d), lambda i: (i // tiles_per_batch, 0, 0))
    return pl.pallas_call(
        _router_kernel,
        grid=(t_rows // TM,),
        in_specs=[pl.BlockSpec((TM, d), lambda i: (i, 0)),
                  pl.BlockSpec((1, d), lambda i: (0, 0)), mod, mod,
                  pl.BlockSpec((N_EXPERTS, d), lambda i: (0, 0))],
        out_specs=[pl.BlockSpec((TM, d), lambda i: (i, 0)),
                   pl.BlockSpec((TOP_K, TM), lambda i: (0, i)),
                   pl.BlockSpec((TOP_K, TM), lambda i: (0, i))],
        out_shape=[jax.ShapeDtypeStruct((t_rows, d), F32),
                   jax.ShapeDtypeStruct((TOP_K, t_rows), jnp.int32),
                   jax.ShapeDtypeStruct((TOP_K, t_rows), F32)],
        compiler_params=_cparams(("parallel",)),
        name="router",
    )(x, g2.reshape(1, d), shift, scale, router_t)


def _row_gather(ids_ref, src_ref, buf_ref, sem, n):
    def issue(r, c):
        pltpu.make_async_copy(src_ref.at[pl.ds(ids_ref[0, r], 1)], buf_ref.at[pl.ds(r, 1)], sem).start()
        return c

    lax.fori_loop(0, n, issue, 0)
    pltpu.make_async_copy(src_ref.at[pl.ds(0, n)], buf_ref, sem).wait()


def _gather_kernel(ids_ref, t_ref, o_ref, buf_ref, sem):
    _row_gather(ids_ref, t_ref, buf_ref, sem, TM)
    o_ref[...] = buf_ref[...].astype(BF16)


def _gather_rows(t, slot_tok):
    n_tiles = slot_tok.shape[0]
    d = t.shape[1]
    return pl.pallas_call(
        _gather_kernel,
        grid=(n_tiles,),
        in_specs=[pl.BlockSpec((None, 1, TM), lambda i: (i, 0, 0), memory_space=pltpu.SMEM),
                  pl.BlockSpec(memory_space=pl.ANY)],
        out_specs=pl.BlockSpec((TM, d), lambda i: (i, 0)),
        out_shape=jax.ShapeDtypeStruct((n_tiles * TM, d), BF16),
        scratch_shapes=[pltpu.VMEM((TM, d), F32), pltpu.SemaphoreType.DMA(())],
        compiler_params=_cparams(("arbitrary",)),
        name="moe_gather",
    )(slot_tok, t)


def _moe_ffn_kernel(eid_ref, nused_ref, x_ref, w1_ref, w3_ref, w2_ref, o_ref, acc_ref):
    i = pl.program_id(0)
    f = pl.program_id(1)

    @pl.when(i < nused_ref[0])
    def _():
        @pl.when(f == 0)
        def _():
            acc_ref[...] = jnp.zeros_like(acc_ref)

        x = x_ref[...]
        a = _dot(x, w1_ref[...])
        b = _dot(x, w3_ref[...])
        acc_ref[...] += _dot((a * jax.nn.sigmoid(a) * b).astype(BF16), w2_ref[...])

    @pl.when(f == pl.num_programs(1) - 1)
    def _():
        o_ref[...] = acc_ref[...]


def _moe_ffn(xs, tile_eid, n_used, w1, w3, w2):
    p, d = xs.shape
    dff = w1.shape[2]
    tf = dff // 4
    grid_spec = pltpu.PrefetchScalarGridSpec(
        num_scalar_prefetch=2,
        grid=(p // TM, dff // tf),
        in_specs=[pl.BlockSpec((TM, d), lambda i, f, eid, nu: (i, 0)),
                  pl.BlockSpec((None, d, tf), lambda i, f, eid, nu: (eid[i], 0, f)),
                  pl.BlockSpec((None, d, tf), lambda i, f, eid, nu: (eid[i], 0, f)),
                  pl.BlockSpec((None, tf, d), lambda i, f, eid, nu: (eid[i], f, 0))],
        out_specs=pl.BlockSpec((TM, d), lambda i, f, eid, nu: (i, 0)),
        scratch_shapes=[pltpu.VMEM((TM, d), F32)])
    return pl.pallas_call(
        _moe_ffn_kernel,
        grid_spec=grid_spec,
        out_shape=jax.ShapeDtypeStruct((p, d), F32),
        compiler_params=_cparams(("arbitrary", "arbitrary")),
        name="moe_ffn",
    )(tile_eid, n_used, xs, w1, w3, w2)


def _combine_kernel(p0_ref, p1_ref, ys_ref, w_ref, x_ref, gm_ref, g3_ref, o_ref, b0_ref, b1_ref, sem0, sem1):
    _row_gather(p0_ref, ys_ref, b0_ref, sem0, TM)
    _row_gather(p1_ref, ys_ref, b1_ref, sem1, TM)
    w = w_ref[...]
    y = w[:, 0:1] * b0_ref[...] + w[:, 1:2] * b1_ref[...]
    o_ref[...] = x_ref[...] + gm_ref[...] * _rms(y, g3_ref[...])


def _combine(pos0, pos1, ys, wcol, x, gate_mod, g3, tiles_per_batch):
    t_rows, d = x.shape
    ids = pl.BlockSpec((None, 1, TM), lambda i: (i, 0, 0), memory_space=pltpu.SMEM)
    return pl.pallas_call(
        _combine_kernel,
        grid=(t_rows // TM,),
        in_specs=[ids, ids,
                  pl.BlockSpec(memory_space=pl.ANY),
                  pl.BlockSpec((TM, TOP_K), lambda i: (i, 0)),
                  pl.BlockSpec((TM, d), lambda i: (i, 0)),
                  pl.BlockSpec((None, 1, d), lambda i: (i // tiles_per_batch, 0, 0)),
                  pl.BlockSpec((1, d), lambda i: (0, 0))],
        out_specs=pl.BlockSpec((TM, d), lambda i: (i, 0)),
        out_shape=jax.ShapeDtypeStruct((t_rows, d), F32),
        scratch_shapes=[pltpu.VMEM((TM, d), F32), pltpu.VMEM((TM, d), F32),
                        pltpu.SemaphoreType.DMA(()), pltpu.SemaphoreType.DMA(())],
        compiler_params=_cparams(("arbitrary",)),
        name="moe_combine",
    )(pos0, pos1, ys, wcol, x, gate_mod, g3.reshape(1, d))


def _routing_tables(top_i, t_rows):
    n_assign = TOP_K * t_rows
    n_tiles = n_assign // TM + N_EXPERTS
    e_flat = top_i.reshape(n_assign)
    onehot = (e_flat[:, None] == jnp.arange(N_EXPERTS)[None, :]).astype(jnp.int32)
    csum = jnp.cumsum(onehot, axis=0)
    counts = csum[-1]
    rank = jnp.sum((csum - onehot) * onehot, axis=1)
    tiles_e = (counts + TM - 1) // TM
    tile_end = jnp.cumsum(tiles_e)
    tile_start = tile_end - tiles_e
    slot = (tile_start * TM)[e_flat] + rank
    tok = jnp.tile(jnp.arange(t_rows, dtype=jnp.int32), TOP_K)
    slot_tok = jnp.zeros((n_tiles * TM,), jnp.int32).at[slot].set(tok)
    n_used = tile_end[-1]
    tile_ids = jnp.arange(n_tiles)
    tile_eid = jnp.sum(tile_ids[:, None] >= tile_end[None, :], axis=1)
    last_eid = jnp.sum(n_used - 1 >= tile_end)
    tile_eid = jnp.where(tile_ids < n_used, tile_eid, last_eid).astype(jnp.int32)
    pos = slot.reshape(TOP_K, t_rows).astype(jnp.int32)
    return (slot_tok.reshape(n_tiles, 1, TM), tile_eid, n_used.reshape(1).astype(jnp.int32),
            pos[0].reshape(t_rows // TM, 1, TM), pos[1].reshape(t_rows // TM, 1, TM))


def _prep_w_in(w):
    d = w.shape[0]
    q_a, k_a, v_a, u_p, c_q, c_kv, k_r, b_g, c_g, x_c, gate = jnp.split(w, IN_SPLITS, axis=1)
    z = lambda n: jnp.zeros((d, n), w.dtype)
    kr_e, kr_o = k_r[:, 0::2], k_r[:, 1::2]
    cols = [gate, q_a * NA_SCALE, k_a, v_a, u_p, b_g, c_g, x_c, c_q, z(128), c_kv,
            z(64), kr_e, kr_o, -kr_o, kr_e, z(128)]
    return jnp.concatenate(cols, axis=1).astype(BF16)


def _prep_w_uq(w_uq):
    wq = w_uq.reshape(MLA_Q_LORA, MLA_HEADS, MLA_NOPE + MLA_ROPE)
    nope, r = wq[..., :MLA_NOPE], wq[..., MLA_NOPE:]
    re, ro = r[..., 0::2], r[..., 1::2]
    ext = jnp.concatenate([nope, re, ro, -ro, re], axis=-1).reshape(MLA_Q_LORA, MLA_HEADS * 128)
    return jnp.pad(ext, ((0, 512 - MLA_Q_LORA), (0, 0))).astype(BF16)


def _prep_w_ukv(w_ukv):
    wkv = w_ukv.reshape(MLA_KV_LORA, MLA_HEADS, MLA_NOPE + MLA_V)
    kn = jnp.concatenate([wkv[..., :MLA_NOPE], jnp.zeros((MLA_KV_LORA, MLA_HEADS, 64), w_ukv.dtype)], axis=-1)
    return (kn.reshape(MLA_KV_LORA, MLA_HEADS * 128).astype(BF16),
            wkv[..., MLA_NOPE:].reshape(MLA_KV_LORA, MLA_HEADS * MLA_V).astype(BF16))


def _rope_tables(nb, s_len, n_ctx_rows):
    pos = jnp.arange(s_len)
    row = (pos // GRID_W).astype(F32)
    col = (pos % GRID_W).astype(F32)
    n_pairs = MLA_ROPE // 4
    inv_freq = ROPE_BASE ** (-jnp.arange(n_pairs, dtype=F32) / n_pairs)
    ang = jnp.concatenate([row[:, None] * inv_freq, col[:, None] * inv_freq], axis=-1)
    cos, sin = jnp.cos(ang), jnp.sin(ang)
    one = jnp.ones((s_len, 64), F32)
    zero32 = jnp.zeros((s_len, 32), F32)
    cos_l = jnp.concatenate([one, cos, cos, zero32], axis=1)
    sin_l = jnp.concatenate([jnp.zeros((s_len, 64), F32), sin, sin, zero32], axis=1)
    cos_c = jnp.concatenate([jnp.ones((n_ctx_rows, 96), F32), jnp.zeros((n_ctx_rows, 32), F32)], axis=1)
    sin_c = jnp.zeros((n_ctx_rows, 128), F32)
    return (jnp.concatenate([jnp.tile(cos_l, (nb, 1)), cos_c], axis=0),
            jnp.concatenate([jnp.tile(sin_l, (nb, 1)), sin_c], axis=0))


def kernel(x, c, ctx, c_ctx, w_ada, b_ada, g_norm, w_in, na_rpb, pool_w, pool_scale, mla_g_q, mla_w_uq,
           mla_g_kv, mla_w_ukv, conv_w, w_branch, w_out, ffn_w1, ffn_w3, ffn_w2, moe_router, moe_w1,
           moe_w3, moe_w2):
    nb, s_len, d = x.shape
    c_len = ctx.shape[1]
    depth = w_in.shape[0]
    t_rows = nb * s_len
    n_ctx = nb * c_len
    assert s_len % TM == 0 and n_ctx == TM and s_len % GRID_W == 0 and t_rows % c_len == 0
    tpb = s_len // TM
    rows = s_len // GRID_W

    xall = jnp.concatenate([x.reshape(t_rows, d), ctx.reshape(n_ctx, d)], axis=0)
    c8 = jnp.zeros((8, d), F32).at[:nb].set(c).at[nb].set(c_ctx)
    cos_t, sin_t = _rope_tables(nb, s_len, n_ctx)

    for l in range(depth):
        last = l == depth - 1
        mod = _adaln(c8, w_ada[l], b_ada[l]).reshape(8, 6, 1, d)
        mods = [mod[:, k] for k in range(6)]

        proj = _inproj(xall, g_norm[l, 0], mods[0], mods[1], _prep_w_in(w_in[l]), tpb)

        wkn, wv = _prep_w_ukv(mla_w_ukv[l])
        gq = jnp.pad(mla_g_q[l], (0, 512 - MLA_Q_LORA)).reshape(1, 512)
        q_m, k_m, v_m = _mla_prep(proj, gq, mla_g_kv[l].reshape(1, MLA_KV_LORA), _prep_w_uq(mla_w_uq[l]),
                                  wkn, wv, cos_t, sin_t)
        o_c = _mla_attn(q_m, k_m, v_m, nb, s_len, c_len)
        o_a = _na_attn(proj, _na_bias_table(na_rpb[l], rows), nb, s_len, c_len)
        o_b, o_d = _local(proj, pool_w[l].astype(BF16), pool_scale[l].reshape(1, 512), conv_w[l],
                          nb, s_len, c_len)

        wb = w_branch[l].astype(BF16)
        wo = w_out[l].astype(BF16)
        if not last:
            ctx_blk = t_rows // c_len
            o_cc = _ctx_attn(q_m, k_m, v_m, nb, c_len, ctx_blk, 128, 0, 0, 0, "mla_ctx_attn")
            o_ac = _ctx_attn(proj, proj, proj, nb, c_len, ctx_blk, 64, OFF_QA, OFF_KA, OFF_VA, "na_ctx_attn")
            o_a = jnp.concatenate([o_a, o_ac], axis=0)
            o_c = jnp.concatenate([o_c, o_cc], axis=0)
            n_rows = t_rows + n_ctx
        else:
            n_rows = t_rows
        xall = _merge(o_a, o_b, o_c, o_d, proj, wb, wo, xall, mods[2], g_norm[l, 1], n_rows, tpb)

        if l % 2 == 0:
            j = l // 2
            xall = _ffn(xall, g_norm[l, 2], mods[3], mods[4], mods[5], g_norm[l, 3],
                        ffn_w1[j].astype(BF16), ffn_w3[j].astype(BF16), ffn_w2[j].astype(BF16), tpb)
        else:
            j = l // 2
            assert last, "context rows are not routed through the experts"
            t_f32, top_i, top_w = _router(xall, g_norm[l, 2], mods[3], mods[4], moe_router[j].T, tpb)
            slot_tok, tile_eid, n_used, pos0, pos1 = _routing_tables(top_i, t_rows)
            xs = _gather_rows(t_f32, slot_tok)
            ys = _moe_ffn(xs, tile_eid, n_used, moe_w1[j].astype(BF16), moe_w3[j].astype(BF16),
                          moe_w2[j].astype(BF16))
            xall = _combine(pos0, pos1, ys, top_w.T, xall, mods[5], g_norm[l, 3], tpb)

    return xall[:t_rows].reshape(nb, s_len, d)
```

```python
import functools

import numpy as np
import jax
import jax.numpy as jnp
from jax import lax
from jax.experimental import pallas as pl
from jax.experimental.pallas import tpu as pltpu

F32 = jnp.float32
BF16 = jnp.bfloat16

GRID_W = 64
N_BRANCH = 4
BRANCH_W = 512
EPS = 1e-6
NA_HEADS = 8
NA_HEAD_DIM = 64
NA_ROWS = 8
NA_COLS = 16
POOL_WINDOWS = (2, 4, 8, 16)
POOL_GROUP = 128
MLA_HEADS = 8
MLA_Q_LORA = 384
MLA_KV_LORA = 256
MLA_NOPE = 64
MLA_ROPE = 32
MLA_V = 64
ROPE_BASE = 10000.0
CONV_K = 3
N_EXPERTS = 8
TOP_K = 2

IN_SIZES = (512, 512, 512, 512, MLA_Q_LORA, MLA_KV_LORA, MLA_ROPE, 512, 512, 512, 4096)
IN_SPLITS = tuple(int(s) for s in np.cumsum(IN_SIZES)[:-1])

OFF_GATE = 0
OFF_QA = 4096
OFF_KA = 4608
OFF_VA = 5120
OFF_UP = 5632
OFF_BG = 6144
OFF_CG = 6656
OFF_XC = 7168
OFF_CQ = 7680
OFF_CKV = 8192
OFF_KR = 8448
N_IN = 8704

NA_SCALE = NA_HEAD_DIM ** -0.5
MLA_SCALE = (MLA_NOPE + MLA_ROPE) ** -0.5
NEG = -1e30

VMEM_LIMIT = 52 * 1024 * 1024
TM = 512
TL = 256
HALO = 16
KC = 256


def _cparams(sem):
    return pltpu.CompilerParams(dimension_semantics=sem, vmem_limit_bytes=VMEM_LIMIT)


def _rms(xf, g):
    ms = jnp.mean(xf * xf, axis=-1, keepdims=True)
    return xf * lax.rsqrt(ms + EPS) * g


def _dot(a, b):
    return jnp.dot(a, b, preferred_element_type=F32)


def _dot_nt(a, b):
    return lax.dot_general(a, b, (((1,), (1,)), ((), ())), preferred_element_type=F32)


def _adaln_kernel(c_ref, w_ref, b_ref, o_ref):
    c = c_ref[...]
    sc = c * jax.nn.sigmoid(c)
    o_ref[...] = jnp.dot(sc, w_ref[...], preferred_element_type=F32,
                         precision=lax.Precision.HIGHEST) + b_ref[...]


def _adaln(c8, w, b):
    d = c8.shape[1]
    n = w.shape[1]
    tn = 1536
    return pl.pallas_call(
        _adaln_kernel,
        grid=(n // tn,),
        in_specs=[pl.BlockSpec((8, d), lambda j: (0, 0)),
                  pl.BlockSpec((d, tn), lambda j: (0, j)),
                  pl.BlockSpec((1, tn), lambda j: (0, j))],
        out_specs=pl.BlockSpec((8, tn), lambda j: (0, j)),
        out_shape=jax.ShapeDtypeStruct((8, n), F32),
        compiler_params=_cparams(("arbitrary",)),
        name="adaln",
    )(c8, w, b.reshape(1, n))


def _inproj_kernel(x_ref, g_ref, sh_ref, sc_ref, w_ref, o_ref, h_ref):
    @pl.when(pl.program_id(1) == 0)
    def _():
        h = _rms(x_ref[...], g_ref[...]) * (1.0 + sc_ref[...]) + sh_ref[...]
        h_ref[...] = h.astype(BF16)

    o_ref[...] = _dot(h_ref[...], w_ref[...]).astype(BF16)


def _inproj(xall, g, shift, scale, w_p, tiles_per_batch):
    r, d = xall.shape
    n = w_p.shape[1]
    tn = n // 4
    mod_spec = pl.BlockSpec((None, 1, d), lambda i, j: (i // tiles_per_batch, 0, 0))
    return pl.pallas_call(
        _inproj_kernel,
        grid=(r // TM, n // tn),
        in_specs=[pl.BlockSpec((TM, d), lambda i, j: (i, 0)),
                  pl.BlockSpec((1, d), lambda i, j: (0, 0)),
                  mod_spec, mod_spec,
                  pl.BlockSpec((d, tn), lambda i, j: (0, j))],
        out_specs=pl.BlockSpec((TM, tn), lambda i, j: (i, j)),
        out_shape=jax.ShapeDtypeStruct((r, n), BF16),
        scratch_shapes=[pltpu.VMEM((TM, d), BF16)],
        compiler_params=_cparams(("parallel", "arbitrary")),
        name="inproj",
    )(xall, g.reshape(1, d), shift, scale, w_p)


def _mla_prep_kernel(cq_ref, ckv_ref, kr_ref, gq_ref, gkv_ref, wqt_ref, wkn_ref, wvt_ref,
                     cos_ref, sin_ref, cost_ref, sint_ref, qt_out, k_out, vt_out):
    cq = cq_ref[...].astype(F32)
    ms = jnp.sum(cq * cq, axis=-1, keepdims=True) * (1.0 / MLA_Q_LORA)
    cqn = (cq * lax.rsqrt(ms + EPS) * gq_ref[...]).astype(BF16)
    qt = _dot_nt(wqt_ref[...], cqn) * MLA_SCALE
    cost = cost_ref[...]
    sint = sint_ref[...]
    for h in range(MLA_HEADS):
        b = 128 * h
        qt_out[b:b + 64, :] = qt[b:b + 64].astype(BF16)
        qt_out[b + 64:b + 96, :] = (qt[b + 64:b + 96] * cost + qt[b + 96:b + 128] * sint).astype(BF16)
        qt_out[b + 96:b + 128, :] = jnp.zeros((32, TM), BF16)

    ckvn = _rms(ckv_ref[...].astype(F32), gkv_ref[...]).astype(BF16)
    kn = _dot(ckvn, wkn_ref[...])
    kseg = kr_ref[...].astype(F32)
    krt = kseg * cos_ref[...] + pltpu.roll(kseg, 96, 1) * sin_ref[...]
    for h in range(MLA_HEADS):
        sl = slice(128 * h, 128 * (h + 1))
        k_out[:, sl] = (kn[:, sl] + krt).astype(BF16)

    vt = _dot_nt(wvt_ref[...], ckvn)
    row = lax.broadcasted_iota(jnp.int32, vt.shape, 0)
    ones = ((row & 127) >= 64) != (((row >> 7) & 1) == 1)
    vt = jnp.where(ones, 1.0, vt).astype(BF16)
    for j in range(TM // KC):
        vt_out[j] = vt[:, KC * j:KC * (j + 1)]


def _mla_prep(proj, gq, gkv, wqt, wkn, wvt, cos_t, sin_t, cos_tt, sin_tt):
    r = proj.shape[0]
    full = lambda shape: pl.BlockSpec(shape, lambda i: (0, 0))
    return pl.pallas_call(
        _mla_prep_kernel,
        grid=(r // TM,),
        in_specs=[pl.BlockSpec((TM, 512), lambda i: (i, OFF_CQ // 512)),
                  pl.BlockSpec((TM, 256), lambda i: (i, OFF_CKV // 256)),
                  pl.BlockSpec((TM, 128), lambda i: (i, OFF_KR // 128)),
                  full((1, 512)), full((1, 256)),
                  full((1024, 512)), full((256, 1024)), full((1024, 256)),
                  pl.BlockSpec((TM, 128), lambda i: (i, 0)),
                  pl.BlockSpec((TM, 128), lambda i: (i, 0)),
                  pl.BlockSpec((32, TM), lambda i: (0, i)),
                  pl.BlockSpec((32, TM), lambda i: (0, i))],
        out_specs=[pl.BlockSpec((1024, TM), lambda i: (0, i)),
                   pl.BlockSpec((TM, 1024), lambda i: (i, 0)),
                   pl.BlockSpec((TM // KC, 1024, KC), lambda i: (i, 0, 0))],
        out_shape=[jax.ShapeDtypeStruct((1024, r), BF16),
                   jax.ShapeDtypeStruct((r, 1024), BF16),
                   jax.ShapeDtypeStruct((r // KC, 1024, KC), BF16)],
        compiler_params=_cparams(("parallel",)),
        name="mla_prep",
    )(proj, proj, proj, gq, gkv, wqt, wkn, wvt, cos_t, sin_t, cos_tt, sin_tt)


def _pick_heads(o0, o1):
    lane = lax.broadcasted_iota(jnp.int32, o0.shape, 1)
    return jnp.where(lane < 64, o0, o1)


def _mla_update(s, vt, m_old, acc_ref):
    m_new = jnp.maximum(m_old, jnp.max(s, axis=0, keepdims=True))
    alpha = jnp.exp(m_old - m_new)
    p = jnp.exp(s - m_new).astype(BF16)
    acc_ref[...] = alpha * acc_ref[...] + _dot(vt, p)
    return m_new


def _mla_finish(a0, a1):
    return _pick_heads(a0 / a0[:, 64:65], a1 / a1[:, 0:1]).astype(BF16)


def _mla_attn_kernel(qt_ref, k_ref, vt_ref, kc_ref, vtc_ref, o_ref, *acc_refs, cpi):
    nt = qt_ref.shape[1] // 256
    units = [(t, h) for h in range(2) for t in range(nt)]
    qts = [qt_ref[128 * h:128 * (h + 1), 256 * t:256 * (t + 1)] for t, h in units]
    n_chunks = k_ref.shape[0] // KC
    for acc_ref in acc_refs:
        acc_ref[...] = jnp.zeros_like(acc_ref)

    def scores(kget):
        return tuple(_dot(kget(h), qts[u]) for u, (t, h) in enumerate(units))

    def latent_keys(j):
        off = pl.multiple_of(j * KC, KC)
        return lambda h: k_ref[pl.ds(off, KC), 128 * h:128 * (h + 1)]

    def update(s_tiles, vget, ms):
        return tuple(_mla_update(s_tiles[u], vget(h), ms[u], acc_refs[u]) for u, (t, h) in enumerate(units))

    ms = tuple(jnp.full((1, 256), -jnp.inf, F32) for _ in units)
    s_ctx = scores(lambda h: kc_ref[:, 128 * h:128 * (h + 1)])
    s_first = scores(latent_keys(0))
    ms = update(s_ctx, lambda h: vtc_ref[0, 128 * h:128 * (h + 1), :], ms)

    def values(j):
        return lambda h: vt_ref[j, 128 * h:128 * (h + 1), :]

    def body(i, carry):
        ms, s_cur = carry
        for c in range(cpi):
            j = i * cpi + c
            s_next = scores(latent_keys(j + 1))
            ms = update(s_cur, values(j), ms)
            s_cur = s_next
        return ms, s_cur

    n_iter = n_chunks // cpi - 1
    ms, s_cur = lax.fori_loop(0, n_iter, body, (ms, s_first))
    for j in range(n_iter * cpi, n_chunks):
        s_next = scores(latent_keys(j + 1)) if j + 1 < n_chunks else None
        ms = update(s_cur, values(j), ms)
        s_cur = s_next
    for t in range(nt):
        o_ref[256 * t:256 * (t + 1), :] = _mla_finish(acc_refs[units.index((t, 0))][...].T,
                                                      acc_refs[units.index((t, 1))][...].T)


def _mla_attn(qt, k, vt, nb, s_len, c_len):
    tq = min(512, s_len)
    nq = s_len // tq
    t_rows = nb * s_len
    cpi = 2 if (s_len // KC) % 2 == 0 else 1
    return pl.pallas_call(
        functools.partial(_mla_attn_kernel, cpi=cpi),
        grid=(nb, 4, nq),
        in_specs=[pl.BlockSpec((256, tq), lambda b, hp, i: (hp, b * nq + i)),
                  pl.BlockSpec((s_len, 256), lambda b, hp, i: (b, hp)),
                  pl.BlockSpec((s_len // KC, 256, KC), lambda b, hp, i: (b, hp, 0)),
                  pl.BlockSpec((c_len, 256), lambda b, hp, i: (t_rows // c_len + b, hp)),
                  pl.BlockSpec((1, 256, KC), lambda b, hp, i: (t_rows // KC + b, hp, 0))],
        out_specs=pl.BlockSpec((tq, 128), lambda b, hp, i: (b * nq + i, hp)),
        out_shape=jax.ShapeDtypeStruct((t_rows, 512), BF16),
        scratch_shapes=[pltpu.VMEM((128, 256), F32)] * (2 * (tq // 256)),
        compiler_params=_cparams(("parallel", "parallel", "arbitrary")),
        name="mla_attn",
    )(qt, k, vt, k, vt)


def _mla_ctx_attn_kernel(qt_ref, kc_ref, vtc_ref, o_ref):
    outs = []
    for h in range(2):
        s = _dot(kc_ref[:, 128 * h:128 * (h + 1)], qt_ref[128 * h:128 * (h + 1), :])
        p = jnp.exp(s - jnp.max(s, axis=0, keepdims=True)).astype(BF16)
        outs.append(_dot(vtc_ref[0, 128 * h:128 * (h + 1), :], p).T)
    o_ref[...] = _mla_finish(outs[0], outs[1])


def _mla_ctx_attn(qt, k, vt, nb, t_rows, c_len):
    blk0 = t_rows // c_len
    return pl.pallas_call(
        _mla_ctx_attn_kernel,
        grid=(nb, 4),
        in_specs=[pl.BlockSpec((256, c_len), lambda b, hp: (hp, blk0 + b)),
                  pl.BlockSpec((c_len, 256), lambda b, hp: (blk0 + b, hp)),
                  pl.BlockSpec((1, 256, KC), lambda b, hp: (blk0 + b, hp, 0))],
        out_specs=pl.BlockSpec((c_len, 128), lambda b, hp: (b, hp)),
        out_shape=jax.ShapeDtypeStruct((nb * c_len, 512), BF16),
        compiler_params=_cparams(("parallel", "parallel")),
        name="mla_ctx_attn",
    )(qt, k, vt)


def _ctx_attn_kernel(q_ref, k_ref, v_ref, o_ref, *, dqk):
    v = v_ref[...]
    outs = []
    for h in range(2):
        sl = slice(dqk * h, dqk * (h + 1))
        s = _dot_nt(q_ref[:, sl], k_ref[:, sl])
        m = jnp.max(s, axis=-1, keepdims=True)
        p = jnp.exp(s - m)
        l = jnp.sum(p, axis=-1, keepdims=True)
        outs.append(_dot(p.astype(BF16), v) / l)
    o_ref[...] = _pick_heads(outs[0], outs[1]).astype(BF16)


def _ctx_attn(q, k, v, nb, c_len, row_blk0, dqk, qcol, kcol, vcol, name):
    w = 2 * dqk
    return pl.pallas_call(
        functools.partial(_ctx_attn_kernel, dqk=dqk),
        grid=(nb, 4),
        in_specs=[pl.BlockSpec((c_len, w), lambda b, hp: (row_blk0 + b, qcol // w + hp)),
                  pl.BlockSpec((c_len, w), lambda b, hp: (row_blk0 + b, kcol // w + hp)),
                  pl.BlockSpec((c_len, 128), lambda b, hp: (row_blk0 + b, vcol // 128 + hp))],
        out_specs=pl.BlockSpec((c_len, 128), lambda b, hp: (b, hp)),
        out_shape=jax.ShapeDtypeStruct((nb * c_len, 512), BF16),
        compiler_params=_cparams(("parallel", "parallel")),
        name=name,
    )(q, k, v)


def _na_attn_kernel(q_ref, k_ref, v_ref, kc_ref, vc_ref, bias_ref, o_ref, *, rows):
    kr = min(NA_ROWS, rows)
    nloc = kr * GRID_W
    vc = vc_ref[...]

    def body(r, carry):
        start = jnp.clip(r - kr // 2, 0, rows - kr)
        pat = start - r + (NA_ROWS - 1)
        qoff = pl.multiple_of(r * GRID_W, GRID_W)
        koff = pl.multiple_of(start * GRID_W, GRID_W)
        q = q_ref[pl.ds(qoff, GRID_W), :]
        kw = k_ref[pl.ds(koff, nloc), :]
        vw = v_ref[pl.ds(koff, nloc), :]
        outs = []
        for h in range(2):
            sl = slice(64 * h, 64 * (h + 1))
            qh = q[:, sl]
            s = _dot_nt(qh, kw[:, sl]) + bias_ref[pat, h]
            sc = _dot_nt(qh, kc_ref[:, sl])
            m = jnp.maximum(jnp.max(s, axis=-1, keepdims=True), jnp.max(sc, axis=-1, keepdims=True))
            p = jnp.exp(s - m)
            pc = jnp.exp(sc - m)
            l = jnp.sum(p, axis=-1, keepdims=True) + jnp.sum(pc, axis=-1, keepdims=True)
            outs.append((_dot(p.astype(BF16), vw) + _dot(pc.astype(BF16), vc)) / l)
        o_ref[pl.ds(qoff, GRID_W), :] = _pick_heads(outs[0], outs[1]).astype(BF16)
        return carry

    lax.fori_loop(0, rows, body, 0)


def _na_attn(proj, bias, nb, s_len, c_len):
    rows = s_len // GRID_W
    ctx_blk = nb * s_len // c_len
    nloc = min(NA_ROWS, rows) * GRID_W
    return pl.pallas_call(
        functools.partial(_na_attn_kernel, rows=rows),
        grid=(nb, 4),
        in_specs=[pl.BlockSpec((s_len, 128), lambda b, hp: (b, OFF_QA // 128 + hp)),
                  pl.BlockSpec((s_len, 128), lambda b, hp: (b, OFF_KA // 128 + hp)),
                  pl.BlockSpec((s_len, 128), lambda b, hp: (b, OFF_VA // 128 + hp)),
                  pl.BlockSpec((c_len, 128), lambda b, hp: (ctx_blk + b, OFF_KA // 128 + hp)),
                  pl.BlockSpec((c_len, 128), lambda b, hp: (ctx_blk + b, OFF_VA // 128 + hp)),
                  pl.BlockSpec((NA_ROWS, 2, GRID_W, nloc), lambda b, hp: (0, hp, 0, 0))],
        out_specs=pl.BlockSpec((s_len, 128), lambda b, hp: (b, hp)),
        out_shape=jax.ShapeDtypeStruct((nb * s_len, 512), BF16),
        compiler_params=_cparams(("parallel", "parallel")),
        name="na_attn",
    )(proj, proj, proj, proj, proj, bias)


def _na_bias_table(rpb, rows):
    kr = min(NA_ROWS, rows)
    col = jnp.arange(GRID_W)
    cstart = jnp.clip(col - NA_COLS // 2, 0, GRID_W - NA_COLS)
    kc = jnp.arange(GRID_W)
    valid = (kc[None, :] >= cstart[:, None]) & (kc[None, :] < cstart[:, None] + NA_COLS)
    dcol = jnp.clip(kc[None, :] - col[:, None] + (NA_COLS - 1), 0, 2 * NA_COLS - 2)
    tab = jnp.where(valid[None, None], rpb[:, :, dcol], NEG)
    ridx = jnp.arange(NA_ROWS)[:, None] + jnp.arange(kr)[None, :]
    t = tab[:, ridx]
    t = t.transpose(1, 0, 3, 2, 4)
    return t.reshape(NA_ROWS, NA_HEADS, GRID_W, kr * GRID_W).astype(F32)


def _local_kernel(up_p, up_c, up_n, cg_p, cg_c, cg_n, xc_p, xc_c, xc_n, bg_ref,
                  pw_ref, ps_ref, cw_ref, ob_ref, od_ref, *, lat_tiles, tiles_lat_seq, s_len, c_len):
    i = pl.program_id(0)
    is_lat = i < lat_tiles
    j = jnp.where(is_lat, i % tiles_lat_seq, 0)
    n_seq = jnp.where(is_lat, s_len, c_len)
    first = j == 0
    last = (j + 1) * TL == n_seq

    def ext(p_ref, c_ref, n_ref):
        p = jnp.where(first, 0.0, p_ref[...].astype(F32))
        n = jnp.where(last, 0.0, n_ref[...].astype(F32))
        return jnp.concatenate([p, c_ref[...].astype(F32), n], axis=0)

    z = ext(cg_p, cg_c, cg_n) * ext(xc_p, xc_c, xc_n)
    cw = cw_ref[...]
    y = (cw[0:1] * z[HALO - 1:HALO - 1 + TL] + cw[1:2] * z[HALO:HALO + TL]
         + cw[2:3] * z[HALO + 1:HALO + 1 + TL])
    od_ref[...] = (bg_ref[...].astype(F32) * y).astype(BF16)

    u = ext(up_p, up_c, up_n)
    t = j * TL + lax.broadcasted_iota(jnp.int32, (TL, 1), 0)
    ps = ps_ref[...]
    for g, w in enumerate(POOL_WINDOWS):
        sl = slice(POOL_GROUP * g, POOL_GROUP * (g + 1))
        ug = u[:, sl]
        acc = ug[HALO - w // 2:HALO - w // 2 + TL]
        for d in range(-w // 2 + 1, w // 2):
            acc = acc + ug[HALO + d:HALO + d + TL]
        cnt = (jnp.minimum(t - w // 2 + w, n_seq) - jnp.maximum(t - w // 2, 0)).astype(F32)
        pooled = acc / cnt - ug[HALO:HALO + TL]
        mixed = _dot(pooled.astype(BF16), pw_ref[g])
        ob_ref[:, sl] = (mixed * ps[:, sl]).astype(BF16)


def _local(proj, pool_w, pool_scale, conv_w, nb, s_len, c_len):
    r = proj.shape[0]
    lat_tiles = nb * s_len // TL
    hpt = TL // HALO
    nhalo = r // HALO

    def cur(off):
        return pl.BlockSpec((TL, 512), lambda i: (i, off // 512))

    def prev(off):
        return pl.BlockSpec((HALO, 512), lambda i: (jnp.maximum(i * hpt - 1, 0), off // 512))

    def nxt(off):
        return pl.BlockSpec((HALO, 512), lambda i: (jnp.minimum((i + 1) * hpt, nhalo - 1), off // 512))

    specs = []
    for off in (OFF_UP, OFF_CG, OFF_XC):
        specs += [prev(off), cur(off), nxt(off)]
    specs += [cur(OFF_BG),
              pl.BlockSpec((4, POOL_GROUP, POOL_GROUP), lambda i: (0, 0, 0)),
              pl.BlockSpec((1, 512), lambda i: (0, 0)),
              pl.BlockSpec((CONV_K, 512), lambda i: (0, 0))]
    return pl.pallas_call(
        functools.partial(_local_kernel, lat_tiles=lat_tiles, tiles_lat_seq=s_len // TL,
                          s_len=s_len, c_len=c_len),
        grid=(r // TL,),
        in_specs=specs,
        out_specs=[pl.BlockSpec((TL, 512), lambda i: (i, 0)), pl.BlockSpec((TL, 512), lambda i: (i, 0))],
        out_shape=[jax.ShapeDtypeStruct((r, 512), BF16), jax.ShapeDtypeStruct((r, 512), BF16)],
        compiler_params=_cparams(("parallel",)),
        name="local_mix",
    )(*([proj] * 10), pool_w, pool_scale, conv_w)


def _merge_kernel(oa_ref, ob_ref, oc_ref, od_ref, gate_ref, wb_ref, wo_ref, x_ref, gm_ref, g_ref, o_ref):
    merged = None
    for k, o in enumerate((oa_ref, ob_ref, oc_ref, od_ref)):
        proj = _dot(o[...], wb_ref[k])
        gk = jax.nn.sigmoid(gate_ref[:, 1024 * k:1024 * (k + 1)].astype(F32))
        merged = gk * proj if merged is None else merged + gk * proj
    y = _dot(merged.astype(BF16), wo_ref[...])
    o_ref[...] = x_ref[...] + gm_ref[...] * _rms(y, g_ref[...])


def _merge(oa, ob, oc, od, proj, wb, wo, xall, gate_mod, g, n_rows, tiles_per_batch):
    d = xall.shape[1]
    tm = 256
    tpb = tiles_per_batch * (TM // tm)
    row = lambda w: pl.BlockSpec((tm, w), lambda i: (i, 0))
    return pl.pallas_call(
        _merge_kernel,
        grid=(n_rows // tm,),
        in_specs=[row(512), row(512), row(512), row(512), row(4096),
                  pl.BlockSpec((4, 512, d), lambda i: (0, 0, 0)),
                  pl.BlockSpec((d, d), lambda i: (0, 0)),
                  row(d),
                  pl.BlockSpec((None, 1, d), lambda i: (i // tpb, 0, 0)),
                  pl.BlockSpec((1, d), lambda i: (0, 0))],
        out_specs=row(d),
        out_shape=jax.ShapeDtypeStruct((n_rows, d), F32),
        compiler_params=_cparams(("parallel",)),
        name="merge",
    )(oa, ob, oc, od, proj, wb, wo, xall, gate_mod, g.reshape(1, d))


def _ffn_kernel(x_ref, g2_ref, sh_ref, sc_ref, gm_ref, g3_ref, w1_ref, w3_ref, w2_ref, o_ref, h_ref, acc_ref):
    f = pl.program_id(1)

    @pl.when(f == 0)
    def _():
        h = _rms(x_ref[...], g2_ref[...]) * (1.0 + sc_ref[...]) + sh_ref[...]
        h_ref[...] = h.astype(BF16)
        acc_ref[...] = jnp.zeros_like(acc_ref)

    h = h_ref[...]
    a = _dot(h, w1_ref[...])
    b = _dot(h, w3_ref[...])
    acc_ref[...] += _dot((a * jax.nn.sigmoid(a) * b).astype(BF16), w2_ref[...])

    @pl.when(f == pl.num_programs(1) - 1)
    def _():
        o_ref[...] = x_ref[...] + gm_ref[...] * _rms(acc_ref[...], g3_ref[...])


def _ffn(xall, g2, shift, scale, gate_mod, g3, w1, w3, w2, tiles_per_batch):
    r, d = xall.shape
    dff = w1.shape[1]
    tf = dff // 2
    mod = pl.BlockSpec((None, 1, d), lambda i, f: (i // tiles_per_batch, 0, 0))
    vec = pl.BlockSpec((1, d), lambda i, f: (0, 0))
    return pl.pallas_call(
        _ffn_kernel,
        grid=(r // TM, dff // tf),
        in_specs=[pl.BlockSpec((TM, d), lambda i, f: (i, 0)), vec, mod, mod, mod, vec,
                  pl.BlockSpec((d, tf), lambda i, f: (0, f)),
                  pl.BlockSpec((d, tf), lambda i, f: (0, f)),
                  pl.BlockSpec((tf, d), lambda i, f: (f, 0))],
        out_specs=pl.BlockSpec((TM, d), lambda i, f: (i, 0)),
        out_shape=jax.ShapeDtypeStruct((r, d), F32),
        scratch_shapes=[pltpu.VMEM((TM, d), BF16), pltpu.VMEM((TM, d), F32)],
        compiler_params=_cparams(("parallel", "arbitrary")),
        name="ffn_dense",
    )(xall, g2.reshape(1, d), shift, scale, gate_mod, g3.reshape(1, d), w1, w3, w2)


def _router_kernel(x_ref, g2_ref, sh_ref, sc_ref, rt_ref, t_ref, idx_ref, w_ref):
    t = _rms(x_ref[...], g2_ref[...]) * (1.0 + sc_ref[...]) + sh_ref[...]
    t_ref[...] = t
    logits = lax.dot_general(rt_ref[...], t, (((1,), (1,)), ((), ())), preferred_element_type=F32,
                             precision=lax.Precision.HIGHEST)
    e = lax.broadcasted_iota(jnp.int32, logits.shape, 0).astype(F32)
    m1 = jnp.max(logits, axis=0, keepdims=True)
    i1 = jnp.min(jnp.where(logits == m1, e, float(N_EXPERTS)), axis=0, keepdims=True)
    rest = jnp.where(e == i1, -jnp.inf, logits)
    m2 = jnp.max(rest, axis=0, keepdims=True)
    i2 = jnp.min(jnp.where(rest == m2, e, float(N_EXPERTS)), axis=0, keepdims=True)
    ex = jnp.exp(m2 - m1)
    w1 = 1.0 / (1.0 + ex)
    idx_ref[0:1, :] = i1.astype(jnp.int32)
    idx_ref[1:2, :] = i2.astype(jnp.int32)
    w_ref[0:1, :] = w1
    w_ref[1:2, :] = ex * w1


def _router(x, g2, shift, scale, router_t, tiles_per_batch):
    t_rows, d = x.shape
    mod = pl.BlockSpec((None, 1, d), lambda i: (i // tiles_per_batch, 0, 0))
    return pl.pallas_call(
        _router_kernel,
        grid=(t_rows // TM,),
        in_specs=[pl.BlockSpec((TM, d), lambda i: (i, 0)),
                  pl.BlockSpec((1, d), lambda i: (0, 0)), mod, mod,
                  pl.BlockSpec((N_EXPERTS, d), lambda i: (0, 0))],
        out_specs=[pl.BlockSpec((TM, d), lambda i: (i, 0)),
                   pl.BlockSpec((TOP_K, TM), lambda i: (0, i)),
                   pl.BlockSpec((TOP_K, TM), lambda i: (0, i))],
        out_shape=[jax.ShapeDtypeStruct((t_rows, d), F32),
                   jax.ShapeDtypeStruct((TOP_K, t_rows), jnp.int32),
                   jax.ShapeDtypeStruct((TOP_K, t_rows), F32)],
        compiler_params=_cparams(("parallel",)),
        name="router",
    )(x, g2.reshape(1, d), shift, scale, router_t)


def _row_gather(ids_ref, src_ref, buf_ref, sem, n):
    def issue(r, c):
        pltpu.make_async_copy(src_ref.at[pl.ds(ids_ref[0, r], 1)], buf_ref.at[pl.ds(r, 1)], sem).start()
        return c

    lax.fori_loop(0, n, issue, 0)
    pltpu.make_async_copy(src_ref.at[pl.ds(0, n)], buf_ref, sem).wait()


def _gather_kernel(ids_ref, t_ref, o_ref, buf_ref, sem):
    _row_gather(ids_ref, t_ref, buf_ref, sem, TM)
    o_ref[...] = buf_ref[...].astype(BF16)


def _gather_rows(t, slot_tok):
    n_tiles = slot_tok.shape[0]
    d = t.shape[1]
    return pl.pallas_call(
        _gather_kernel,
        grid=(n_tiles,),
        in_specs=[pl.BlockSpec((None, 1, TM), lambda i: (i, 0, 0), memory_space=pltpu.SMEM),
                  pl.BlockSpec(memory_space=pl.ANY)],
        out_specs=pl.BlockSpec((TM, d), lambda i: (i, 0)),
        out_shape=jax.ShapeDtypeStruct((n_tiles * TM, d), BF16),
        scratch_shapes=[pltpu.VMEM((TM, d), F32), pltpu.SemaphoreType.DMA(())],
        compiler_params=_cparams(("arbitrary",)),
        name="moe_gather",
    )(slot_tok, t)


def _moe_ffn_kernel(eid_ref, nused_ref, x_ref, w1_ref, w3_ref, w2_ref, o_ref, acc_ref):
    i = pl.program_id(0)
    f = pl.program_id(1)

    @pl.when(i < nused_ref[0])
    def _():
        @pl.when(f == 0)
        def _():
            acc_ref[...] = jnp.zeros_like(acc_ref)

        x = x_ref[...]
        a = _dot(x, w1_ref[...])
        b = _dot(x, w3_ref[...])
        acc_ref[...] += _dot((a * jax.nn.sigmoid(a) * b).astype(BF16), w2_ref[...])

    @pl.when(f == pl.num_programs(1) - 1)
    def _():
        o_ref[...] = acc_ref[...]


def _moe_ffn(xs, tile_eid, n_used, w1, w3, w2):
    p, d = xs.shape
    dff = w1.shape[2]
    tf = dff // 4
    grid_spec = pltpu.PrefetchScalarGridSpec(
        num_scalar_prefetch=2,
        grid=(p // TM, dff // tf),
        in_specs=[pl.BlockSpec((TM, d), lambda i, f, eid, nu: (i, 0)),
                  pl.BlockSpec((None, d, tf), lambda i, f, eid, nu: (eid[i], 0, f)),
                  pl.BlockSpec((None, d, tf), lambda i, f, eid, nu: (eid[i], 0, f)),
                  pl.BlockSpec((None, tf, d), lambda i, f, eid, nu: (eid[i], f, 0))],
        out_specs=pl.BlockSpec((TM, d), lambda i, f, eid, nu: (i, 0)),
        scratch_shapes=[pltpu.VMEM((TM, d), F32)])
    return pl.pallas_call(
        _moe_ffn_kernel,
        grid_spec=grid_spec,
        out_shape=jax.ShapeDtypeStruct((p, d), F32),
        compiler_params=_cparams(("arbitrary", "arbitrary")),
        name="moe_ffn",
    )(tile_eid, n_used, xs, w1, w3, w2)


def _combine_kernel(p0_ref, p1_ref, ys_ref, w_ref, x_ref, gm_ref, g3_ref, o_ref, b0_ref, b1_ref, sem0, sem1):
    _row_gather(p0_ref, ys_ref, b0_ref, sem0, TM)
    _row_gather(p1_ref, ys_ref, b1_ref, sem1, TM)
    w = w_ref[...]
    y = w[:, 0:1] * b0_ref[...] + w[:, 1:2] * b1_ref[...]
    o_ref[...] = x_ref[...] + gm_ref[...] * _rms(y, g3_ref[...])


def _combine(pos0, pos1, ys, wcol, x, gate_mod, g3, tiles_per_batch):
    t_rows, d = x.shape
    ids = pl.BlockSpec((None, 1, TM), lambda i: (i, 0, 0), memory_space=pltpu.SMEM)
    return pl.pallas_call(
        _combine_kernel,
        grid=(t_rows // TM,),
        in_specs=[ids, ids,
                  pl.BlockSpec(memory_space=pl.ANY),
                  pl.BlockSpec((TM, TOP_K), lambda i: (i, 0)),
                  pl.BlockSpec((TM, d), lambda i: (i, 0)),
                  pl.BlockSpec((None, 1, d), lambda i: (i // tiles_per_batch, 0, 0)),
                  pl.BlockSpec((1, d), lambda i: (0, 0))],
        out_specs=pl.BlockSpec((TM, d), lambda i: (i, 0)),
        out_shape=jax.ShapeDtypeStruct((t_rows, d), F32),
        scratch_shapes=[pltpu.VMEM((TM, d), F32), pltpu.VMEM((TM, d), F32),
                        pltpu.SemaphoreType.DMA(()), pltpu.SemaphoreType.DMA(())],
        compiler_params=_cparams(("arbitrary",)),
        name="moe_combine",
    )(pos0, pos1, ys, wcol, x, gate_mod, g3.reshape(1, d))


def _routing_tables(top_i, t_rows):
    n_assign = TOP_K * t_rows
    n_tiles = n_assign // TM + N_EXPERTS
    e_flat = top_i.reshape(n_assign)
    onehot = (e_flat[:, None] == jnp.arange(N_EXPERTS)[None, :]).astype(jnp.int32)
    csum = jnp.cumsum(onehot, axis=0)
    counts = csum[-1]
    rank = jnp.sum((csum - onehot) * onehot, axis=1)
    tiles_e = (counts + TM - 1) // TM
    tile_end = jnp.cumsum(tiles_e)
    tile_start = tile_end - tiles_e
    slot = (tile_start * TM)[e_flat] + rank
    tok = jnp.tile(jnp.arange(t_rows, dtype=jnp.int32), TOP_K)
    slot_tok = jnp.zeros((n_tiles * TM,), jnp.int32).at[slot].set(tok)
    n_used = tile_end[-1]
    tile_ids = jnp.arange(n_tiles)
    tile_eid = jnp.sum(tile_ids[:, None] >= tile_end[None, :], axis=1)
    last_eid = jnp.sum(n_used - 1 >= tile_end)
    tile_eid = jnp.where(tile_ids < n_used, tile_eid, last_eid).astype(jnp.int32)
    pos = slot.reshape(TOP_K, t_rows).astype(jnp.int32)
    return (slot_tok.reshape(n_tiles, 1, TM), tile_eid, n_used.reshape(1).astype(jnp.int32),
            pos[0].reshape(t_rows // TM, 1, TM), pos[1].reshape(t_rows // TM, 1, TM))


def _prep_w_in(w):
    d = w.shape[0]
    q_a, k_a, v_a, u_p, c_q, c_kv, k_r, b_g, c_g, x_c, gate = jnp.split(w, IN_SPLITS, axis=1)
    z = lambda n: jnp.zeros((d, n), w.dtype)
    kr_e, kr_o = k_r[:, 0::2], k_r[:, 1::2]
    cols = [gate, q_a * NA_SCALE, k_a, v_a, u_p, b_g, c_g, x_c, c_q, z(128), c_kv,
            z(64), kr_e, kr_o, -kr_o, kr_e, z(128)]
    return jnp.concatenate(cols, axis=1).astype(BF16)


def _prep_w_uq(w_uq):
    wq = w_uq.reshape(MLA_Q_LORA, MLA_HEADS, MLA_NOPE + MLA_ROPE)
    nope, r = wq[..., :MLA_NOPE], wq[..., MLA_NOPE:]
    re, ro = r[..., 0::2], r[..., 1::2]
    ext = jnp.concatenate([nope, re, ro, -ro, re], axis=-1).reshape(MLA_Q_LORA, MLA_HEADS * 128)
    return jnp.pad(ext, ((0, 512 - MLA_Q_LORA), (0, 0))).T.astype(BF16)


def _prep_w_ukv(w_ukv):
    wkv = w_ukv.reshape(MLA_KV_LORA, MLA_HEADS, MLA_NOPE + MLA_V)
    zeros = jnp.zeros((MLA_KV_LORA, MLA_HEADS, 64), w_ukv.dtype)
    kn = jnp.concatenate([wkv[..., :MLA_NOPE], zeros], axis=-1).reshape(MLA_KV_LORA, MLA_HEADS * 128)
    v = wkv[..., MLA_NOPE:]
    odd = (jnp.arange(MLA_HEADS) % 2 == 1)[None, :, None]
    vt = jnp.where(odd, jnp.concatenate([zeros, v], axis=-1), jnp.concatenate([v, zeros], axis=-1))
    return kn.astype(BF16), vt.reshape(MLA_KV_LORA, MLA_HEADS * 128).T.astype(BF16)


def _rope_tables(nb, s_len, n_ctx_rows):
    pos = jnp.arange(s_len)
    row = (pos // GRID_W).astype(F32)
    col = (pos % GRID_W).astype(F32)
    n_pairs = MLA_ROPE // 4
    inv_freq = ROPE_BASE ** (-jnp.arange(n_pairs, dtype=F32) / n_pairs)
    ang = jnp.concatenate([row[:, None] * inv_freq, col[:, None] * inv_freq], axis=-1)
    cos = jnp.concatenate([jnp.tile(jnp.cos(ang), (nb, 1)), jnp.ones((n_ctx_rows, 16), F32)], axis=0)
    sin = jnp.concatenate([jnp.tile(jnp.sin(ang), (nb, 1)), jnp.zeros((n_ctx_rows, 16), F32)], axis=0)
    r = cos.shape[0]
    cos2 = jnp.concatenate([cos, cos], axis=1)
    sin2 = jnp.concatenate([sin, sin], axis=1)
    pad = lambda t, lead: jnp.concatenate([lead, t, jnp.zeros((r, 32), F32)], axis=1)
    return (pad(cos2, jnp.ones((r, 64), F32)), pad(sin2, jnp.zeros((r, 64), F32)), cos2.T, sin2.T)


def kernel(x, c, ctx, c_ctx, w_ada, b_ada, g_norm, w_in, na_rpb, pool_w, pool_scale, mla_g_q, mla_w_uq,
           mla_g_kv, mla_w_ukv, conv_w, w_branch, w_out, ffn_w1, ffn_w3, ffn_w2, moe_router, moe_w1,
           moe_w3, moe_w2):
    nb, s_len, d = x.shape
    c_len = ctx.shape[1]
    depth = w_in.shape[0]
    t_rows = nb * s_len
    n_ctx = nb * c_len
    assert s_len % TM == 0 and n_ctx == TM and s_len % GRID_W == 0 and t_rows % c_len == 0
    assert c_len == KC and c_len == TL
    tpb = s_len // TM
    rows = s_len // GRID_W

    xall = jnp.concatenate([x.reshape(t_rows, d), ctx.reshape(n_ctx, d)], axis=0)
    c8 = jnp.zeros((8, d), F32).at[:nb].set(c).at[nb].set(c_ctx)
    cos_t, sin_t, cos_tt, sin_tt = _rope_tables(nb, s_len, n_ctx)

    for l in range(depth):
        last = l == depth - 1
        mod = _adaln(c8, w_ada[l], b_ada[l]).reshape(8, 6, 1, d)
        mods = [mod[:, k] for k in range(6)]

        proj = _inproj(xall, g_norm[l, 0], mods[0], mods[1], _prep_w_in(w_in[l]), tpb)

        wkn, wvt = _prep_w_ukv(mla_w_ukv[l])
        gq = jnp.pad(mla_g_q[l], (0, 512 - MLA_Q_LORA)).reshape(1, 512)
        qt_m, k_m, vt_m = _mla_prep(proj, gq, mla_g_kv[l].reshape(1, MLA_KV_LORA), _prep_w_uq(mla_w_uq[l]),
                                    wkn, wvt, cos_t, sin_t, cos_tt, sin_tt)
        o_c = _mla_attn(qt_m, k_m, vt_m, nb, s_len, c_len)
        o_a = _na_attn(proj, _na_bias_table(na_rpb[l], rows), nb, s_len, c_len)
        o_b, o_d = _local(proj, pool_w[l].astype(BF16), pool_scale[l].reshape(1, 512), conv_w[l],
                          nb, s_len, c_len)

        wb = w_branch[l].astype(BF16)
        wo = w_out[l].astype(BF16)
        if not last:
            ctx_blk = t_rows // c_len
            o_cc = _mla_ctx_attn(qt_m, k_m, vt_m, nb, t_rows, c_len)
            o_ac = _ctx_attn(proj, proj, proj, nb, c_len, ctx_blk, 64, OFF_QA, OFF_KA, OFF_VA, "na_ctx_attn")
            o_a = jnp.concatenate([o_a, o_ac], axis=0)
            o_c = jnp.concatenate([o_c, o_cc], axis=0)
            n_rows = t_rows + n_ctx
        else:
            n_rows = t_rows
        xall = _merge(o_a, o_b, o_c, o_d, proj, wb, wo, xall, mods[2], g_norm[l, 1], n_rows, tpb)

        if l % 2 == 0:
            j = l // 2
            xall = _ffn(xall, g_norm[l, 2], mods[3], mods[4], mods[5], g_norm[l, 3],
                        ffn_w1[j].astype(BF16), ffn_w3[j].astype(BF16), ffn_w2[j].astype(BF16), tpb)
        else:
            j = l // 2
            assert last, "context rows are not routed through the experts"
            t_f32, top_i, top_w = _router(xall, g_norm[l, 2], mods[3], mods[4], moe_router[j].T, tpb)
            slot_tok, tile_eid, n_used, pos0, pos1 = _routing_tables(top_i, t_rows)
            xs = _gather_rows(t_f32, slot_tok)
            ys = _moe_ffn(xs, tile_eid, n_used, moe_w1[j].astype(BF16), moe_w3[j].astype(BF16),
                          moe_w2[j].astype(BF16))
            xall = _combine(pos0, pos1, ys, top_w.T, xall, mods[5], g_norm[l, 3], tpb)

    return xall[:t_rows].reshape(nb, s_len, d)
```

```python
import functools

import numpy as np
import jax
import jax.numpy as jnp
from jax import lax
from jax.experimental import pallas as pl
from jax.experimental.pallas import tpu as pltpu

F32 = jnp.float32
BF16 = jnp.bfloat16

GRID_W = 64
N_BRANCH = 4
BRANCH_W = 512
EPS = 1e-6
NA_HEADS = 8
NA_HEAD_DIM = 64
NA_ROWS = 8
NA_COLS = 16
NA_G = 4
NA_WIN = NA_G + NA_ROWS
POOL_WINDOWS = (2, 4, 8, 16)
POOL_GROUP = 128
MLA_HEADS = 8
MLA_Q_LORA = 384
MLA_KV_LORA = 256
MLA_NOPE = 64
MLA_ROPE = 32
MLA_V = 64
ROPE_BASE = 10000.0
CONV_K = 3
N_EXPERTS = 8
TOP_K = 2

IN_SIZES = (512, 512, 512, 512, MLA_Q_LORA, MLA_KV_LORA, MLA_ROPE, 512, 512, 512, 4096)
IN_SPLITS = tuple(int(s) for s in np.cumsum(IN_SIZES)[:-1])

OFF_GATE = 0
OFF_QA = 4096
OFF_KA = 4608
OFF_VA = 5120
OFF_UP = 5632
OFF_BG = 6144
OFF_CG = 6656
OFF_XC = 7168
OFF_CQ = 7680
OFF_CKV = 8192
OFF_KR = 8448
N_IN = 8704

NA_SCALE = NA_HEAD_DIM ** -0.5
MLA_SCALE = (MLA_NOPE + MLA_ROPE) ** -0.5
NEG = -1e30

VMEM_LIMIT = 52 * 1024 * 1024
TM = 512
TL = 256
HALO = 16
KC = 256
MOE_F_STEPS = 4


def _cparams(sem):
    return pltpu.CompilerParams(dimension_semantics=sem, vmem_limit_bytes=VMEM_LIMIT)


def _rms(xf, g):
    ms = jnp.mean(xf * xf, axis=-1, keepdims=True)
    return xf * lax.rsqrt(ms + EPS) * g


def _dot(a, b):
    return jnp.dot(a, b, preferred_element_type=F32)


def _dot_nt(a, b):
    return lax.dot_general(a, b, (((1,), (1,)), ((), ())), preferred_element_type=F32)


def _adaln_kernel(c_ref, w_ref, b_ref, o_ref):
    c = c_ref[...]
    sc = c * jax.nn.sigmoid(c)
    o_ref[...] = jnp.dot(sc, w_ref[...], preferred_element_type=F32,
                         precision=lax.Precision.HIGHEST) + b_ref[...]


def _adaln(c8, w, b):
    d = c8.shape[1]
    n = w.shape[1]
    tn = 1536
    return pl.pallas_call(
        _adaln_kernel,
        grid=(n // tn,),
        in_specs=[pl.BlockSpec((8, d), lambda j: (0, 0)),
                  pl.BlockSpec((d, tn), lambda j: (0, j)),
                  pl.BlockSpec((1, tn), lambda j: (0, j))],
        out_specs=pl.BlockSpec((8, tn), lambda j: (0, j)),
        out_shape=jax.ShapeDtypeStruct((8, n), F32),
        compiler_params=_cparams(("arbitrary",)),
        name="adaln",
    )(c8, w, b.reshape(1, n))


def _inproj_kernel(x_ref, g_ref, sh_ref, sc_ref, w_ref, o_ref, h_ref):
    @pl.when(pl.program_id(1) == 0)
    def _():
        h = _rms(x_ref[...], g_ref[...]) * (1.0 + sc_ref[...]) + sh_ref[...]
        h_ref[...] = h.astype(BF16)

    o_ref[...] = _dot(h_ref[...], w_ref[...]).astype(BF16)


def _inproj(xall, g, shift, scale, w_p, tiles_per_batch):
    r, d = xall.shape
    n = w_p.shape[1]
    tn = n // 4
    mod_spec = pl.BlockSpec((None, 1, d), lambda i, j: (i // tiles_per_batch, 0, 0))
    return pl.pallas_call(
        _inproj_kernel,
        grid=(r // TM, n // tn),
        in_specs=[pl.BlockSpec((TM, d), lambda i, j: (i, 0)),
                  pl.BlockSpec((1, d), lambda i, j: (0, 0)),
                  mod_spec, mod_spec,
                  pl.BlockSpec((d, tn), lambda i, j: (0, j))],
        out_specs=pl.BlockSpec((TM, tn), lambda i, j: (i, j)),
        out_shape=jax.ShapeDtypeStruct((r, n), BF16),
        scratch_shapes=[pltpu.VMEM((TM, d), BF16)],
        compiler_params=_cparams(("parallel", "arbitrary")),
        name="inproj",
    )(xall, g.reshape(1, d), shift, scale, w_p)


def _mla_prep_kernel(cq_ref, ckv_ref, kr_ref, gq_ref, gkv_ref, wqt_ref, wkn_ref, wvt_ref,
                     cos_ref, sin_ref, cost_ref, sint_ref, qt_out, k_out, vt_out):
    cq = cq_ref[...].astype(F32)
    ms = jnp.sum(cq * cq, axis=-1, keepdims=True) * (1.0 / MLA_Q_LORA)
    cqn = (cq * lax.rsqrt(ms + EPS) * gq_ref[...]).astype(BF16)
    qt = _dot_nt(wqt_ref[...], cqn) * MLA_SCALE
    cost = cost_ref[...]
    sint = sint_ref[...]
    for h in range(MLA_HEADS):
        b = 128 * h
        qt_out[b:b + 64, :] = qt[b:b + 64].astype(BF16)
        qt_out[b + 64:b + 96, :] = (qt[b + 64:b + 96] * cost + qt[b + 96:b + 128] * sint).astype(BF16)
        qt_out[b + 96:b + 128, :] = jnp.zeros((32, TM), BF16)

    ckvn = _rms(ckv_ref[...].astype(F32), gkv_ref[...]).astype(BF16)
    kn = _dot(ckvn, wkn_ref[...])
    kseg = kr_ref[...].astype(F32)
    krt = kseg * cos_ref[...] + pltpu.roll(kseg, 96, 1) * sin_ref[...]
    for h in range(MLA_HEADS):
        sl = slice(128 * h, 128 * (h + 1))
        k_out[:, sl] = (kn[:, sl] + krt).astype(BF16)

    vt = _dot_nt(wvt_ref[...], ckvn)
    row = lax.broadcasted_iota(jnp.int32, vt.shape, 0)
    ones = ((row & 127) >= 64) != (((row >> 7) & 1) == 1)
    vt = jnp.where(ones, 1.0, vt).astype(BF16)
    for j in range(TM // KC):
        vt_out[j] = vt[:, KC * j:KC * (j + 1)]


def _mla_prep(proj, gq, gkv, wqt, wkn, wvt, cos_t, sin_t, cos_tt, sin_tt):
    r = proj.shape[0]
    full = lambda shape: pl.BlockSpec(shape, lambda i: (0, 0))
    return pl.pallas_call(
        _mla_prep_kernel,
        grid=(r // TM,),
        in_specs=[pl.BlockSpec((TM, 512), lambda i: (i, OFF_CQ // 512)),
                  pl.BlockSpec((TM, 256), lambda i: (i, OFF_CKV // 256)),
                  pl.BlockSpec((TM, 128), lambda i: (i, OFF_KR // 128)),
                  full((1, 512)), full((1, 256)),
                  full((1024, 512)), full((256, 1024)), full((1024, 256)),
                  pl.BlockSpec((TM, 128), lambda i: (i, 0)),
                  pl.BlockSpec((TM, 128), lambda i: (i, 0)),
                  pl.BlockSpec((32, TM), lambda i: (0, i)),
                  pl.BlockSpec((32, TM), lambda i: (0, i))],
        out_specs=[pl.BlockSpec((1024, TM), lambda i: (0, i)),
                   pl.BlockSpec((TM, 1024), lambda i: (i, 0)),
                   pl.BlockSpec((TM // KC, 1024, KC), lambda i: (i, 0, 0))],
        out_shape=[jax.ShapeDtypeStruct((1024, r), BF16),
                   jax.ShapeDtypeStruct((r, 1024), BF16),
                   jax.ShapeDtypeStruct((r // KC, 1024, KC), BF16)],
        compiler_params=_cparams(("parallel",)),
        name="mla_prep",
    )(proj, proj, proj, gq, gkv, wqt, wkn, wvt, cos_t, sin_t, cos_tt, sin_tt)


def _pick_heads(o0, o1):
    lane = lax.broadcasted_iota(jnp.int32, o0.shape, 1)
    return jnp.where(lane < 64, o0, o1)


def _mla_update(s, vt, m_old, acc_ref):
    m_new = jnp.maximum(m_old, jnp.max(s, axis=0, keepdims=True))
    alpha = jnp.exp(m_old - m_new)
    p = jnp.exp(s - m_new).astype(BF16)
    acc_ref[...] = alpha * acc_ref[...] + _dot(vt, p)
    return m_new


def _mla_finish(a0, a1):
    return _pick_heads(a0 / a0[:, 64:65], a1 / a1[:, 0:1]).astype(BF16)


def _mla_attn_kernel(qt_ref, k_ref, vt_ref, kc_ref, vtc_ref, o_ref, *acc_refs, cpi):
    nt = qt_ref.shape[1] // 256
    units = [(t, h) for h in range(2) for t in range(nt)]
    qts = [qt_ref[128 * h:128 * (h + 1), 256 * t:256 * (t + 1)] for t, h in units]
    n_chunks = k_ref.shape[0] // KC
    for acc_ref in acc_refs:
        acc_ref[...] = jnp.zeros_like(acc_ref)

    def scores(kget):
        return tuple(_dot(kget(h), qts[u]) for u, (t, h) in enumerate(units))

    def latent_keys(j):
        off = pl.multiple_of(j * KC, KC)
        return lambda h: k_ref[pl.ds(off, KC), 128 * h:128 * (h + 1)]

    def update(s_tiles, vget, ms):
        return tuple(_mla_update(s_tiles[u], vget(h), ms[u], acc_refs[u]) for u, (t, h) in enumerate(units))

    ms = tuple(jnp.full((1, 256), -jnp.inf, F32) for _ in units)
    s_ctx = scores(lambda h: kc_ref[:, 128 * h:128 * (h + 1)])
    s_first = scores(latent_keys(0))
    ms = update(s_ctx, lambda h: vtc_ref[0, 128 * h:128 * (h + 1), :], ms)

    def values(j):
        return lambda h: vt_ref[j, 128 * h:128 * (h + 1), :]

    def body(i, carry):
        ms, s_cur = carry
        for c in range(cpi):
            j = i * cpi + c
            s_next = scores(latent_keys(j + 1))
            ms = update(s_cur, values(j), ms)
            s_cur = s_next
        return ms, s_cur

    n_iter = n_chunks // cpi - 1
    ms, s_cur = lax.fori_loop(0, n_iter, body, (ms, s_first))
    for j in range(n_iter * cpi, n_chunks):
        s_next = scores(latent_keys(j + 1)) if j + 1 < n_chunks else None
        ms = update(s_cur, values(j), ms)
        s_cur = s_next
    for t in range(nt):
        o_ref[256 * t:256 * (t + 1), :] = _mla_finish(acc_refs[units.index((t, 0))][...].T,
                                                      acc_refs[units.index((t, 1))][...].T)


def _mla_attn(qt, k, vt, nb, s_len, c_len):
    tq = min(512, s_len)
    nq = s_len // tq
    t_rows = nb * s_len
    cpi = 2 if (s_len // KC) % 2 == 0 else 1
    return pl.pallas_call(
        functools.partial(_mla_attn_kernel, cpi=cpi),
        grid=(nb, 4, nq),
        in_specs=[pl.BlockSpec((256, tq), lambda b, hp, i: (hp, b * nq + i)),
                  pl.BlockSpec((s_len, 256), lambda b, hp, i: (b, hp)),
                  pl.BlockSpec((s_len // KC, 256, KC), lambda b, hp, i: (b, hp, 0)),
                  pl.BlockSpec((c_len, 256), lambda b, hp, i: (t_rows // c_len + b, hp)),
                  pl.BlockSpec((1, 256, KC), lambda b, hp, i: (t_rows // KC + b, hp, 0))],
        out_specs=pl.BlockSpec((tq, 128), lambda b, hp, i: (b * nq + i, hp)),
        out_shape=jax.ShapeDtypeStruct((t_rows, 512), BF16),
        scratch_shapes=[pltpu.VMEM((128, 256), F32)] * (2 * (tq // 256)),
        compiler_params=_cparams(("parallel", "parallel", "arbitrary")),
        name="mla_attn",
    )(qt, k, vt, k, vt)


def _mla_ctx_attn_kernel(qt_ref, kc_ref, vtc_ref, o_ref):
    outs = []
    for h in range(2):
        s = _dot(kc_ref[:, 128 * h:128 * (h + 1)], qt_ref[128 * h:128 * (h + 1), :])
        p = jnp.exp(s - jnp.max(s, axis=0, keepdims=True)).astype(BF16)
        outs.append(_dot(vtc_ref[0, 128 * h:128 * (h + 1), :], p).T)
    o_ref[...] = _mla_finish(outs[0], outs[1])


def _mla_ctx_attn(qt, k, vt, nb, t_rows, c_len):
    blk0 = t_rows // c_len
    return pl.pallas_call(
        _mla_ctx_attn_kernel,
        grid=(nb, 4),
        in_specs=[pl.BlockSpec((256, c_len), lambda b, hp: (hp, blk0 + b)),
                  pl.BlockSpec((c_len, 256), lambda b, hp: (blk0 + b, hp)),
                  pl.BlockSpec((1, 256, KC), lambda b, hp: (blk0 + b, hp, 0))],
        out_specs=pl.BlockSpec((c_len, 128), lambda b, hp: (b, hp)),
        out_shape=jax.ShapeDtypeStruct((nb * c_len, 512), BF16),
        compiler_params=_cparams(("parallel", "parallel")),
        name="mla_ctx_attn",
    )(qt, k, vt)


def _ctx_attn_kernel(q_ref, k_ref, v_ref, o_ref, *, dqk):
    v = v_ref[...]
    outs = []
    for h in range(2):
        sl = slice(dqk * h, dqk * (h + 1))
        s = _dot_nt(q_ref[:, sl], k_ref[:, sl])
        m = jnp.max(s, axis=-1, keepdims=True)
        p = jnp.exp(s - m)
        l = jnp.sum(p, axis=-1, keepdims=True)
        outs.append(_dot(p.astype(BF16), v) / l)
    o_ref[...] = _pick_heads(outs[0], outs[1]).astype(BF16)


def _ctx_attn(q, k, v, nb, c_len, row_blk0, dqk, qcol, kcol, vcol, name):
    w = 2 * dqk
    return pl.pallas_call(
        functools.partial(_ctx_attn_kernel, dqk=dqk),
        grid=(nb, 4),
        in_specs=[pl.BlockSpec((c_len, w), lambda b, hp: (row_blk0 + b, qcol // w + hp)),
                  pl.BlockSpec((c_len, w), lambda b, hp: (row_blk0 + b, kcol // w + hp)),
                  pl.BlockSpec((c_len, 128), lambda b, hp: (row_blk0 + b, vcol // 128 + hp))],
        out_specs=pl.BlockSpec((c_len, 128), lambda b, hp: (b, hp)),
        out_shape=jax.ShapeDtypeStruct((nb * c_len, 512), BF16),
        compiler_params=_cparams(("parallel", "parallel")),
        name=name,
    )(q, k, v)


def _na_attn_kernel(q_ref, k_ref, v_ref, kc_ref, vc_ref, bias_ref, o_ref, *, rows):
    nq = NA_G * GRID_W
    nk = NA_WIN * GRID_W
    vc = vc_ref[...]

    def body(g, carry):
        r0 = g * NA_G
        ws = jnp.clip(r0 - NA_ROWS // 2, 0, rows - NA_WIN)
        pat = lax.shift_right_logical(r0 - ws, 2)
        qoff = pl.multiple_of(r0 * GRID_W, nq)
        koff = pl.multiple_of(ws * GRID_W, GRID_W)
        q = q_ref[pl.ds(qoff, nq), :]
        kw = k_ref[pl.ds(koff, nk), :]
        vw = v_ref[pl.ds(koff, nk), :]
        scores = []
        for h in range(2):
            sl = slice(64 * h, 64 * (h + 1))
            scores.append((_dot_nt(q[:, sl], kw[:, sl]) + bias_ref[pat, h], _dot_nt(q[:, sl], kc_ref[:, sl])))
        outs = []
        for s, sc in scores:
            m = jnp.maximum(jnp.max(s, axis=-1, keepdims=True), jnp.max(sc, axis=-1, keepdims=True))
            p = jnp.exp(s - m)
            pc = jnp.exp(sc - m)
            l = jnp.sum(p, axis=-1, keepdims=True) + jnp.sum(pc, axis=-1, keepdims=True)
            outs.append((_dot(p.astype(BF16), vw) + _dot(pc.astype(BF16), vc)) / l)
        o_ref[pl.ds(qoff, nq), :] = _pick_heads(outs[0], outs[1]).astype(BF16)
        return carry

    lax.fori_loop(0, rows // NA_G, body, 0)


def _na_attn(proj, bias, nb, s_len, c_len):
    rows = s_len // GRID_W
    assert NA_G == 4 and rows % NA_G == 0 and rows >= NA_WIN
    ctx_blk = nb * s_len // c_len
    return pl.pallas_call(
        functools.partial(_na_attn_kernel, rows=rows),
        grid=(nb, 4),
        in_specs=[pl.BlockSpec((s_len, 128), lambda b, hp: (b, OFF_QA // 128 + hp)),
                  pl.BlockSpec((s_len, 128), lambda b, hp: (b, OFF_KA // 128 + hp)),
                  pl.BlockSpec((s_len, 128), lambda b, hp: (b, OFF_VA // 128 + hp)),
                  pl.BlockSpec((c_len, 128), lambda b, hp: (ctx_blk + b, OFF_KA // 128 + hp)),
                  pl.BlockSpec((c_len, 128), lambda b, hp: (ctx_blk + b, OFF_VA // 128 + hp)),
                  pl.BlockSpec((3, 2, NA_G * GRID_W, NA_WIN * GRID_W), lambda b, hp: (0, hp, 0, 0))],
        out_specs=pl.BlockSpec((s_len, 128), lambda b, hp: (b, hp)),
        out_shape=jax.ShapeDtypeStruct((nb * s_len, 512), BF16),
        compiler_params=_cparams(("parallel", "parallel")),
        name="na_attn",
    )(proj, proj, proj, proj, proj, bias)


def _na_bias_table(rpb):
    col = jnp.arange(GRID_W)
    cstart = jnp.clip(col - NA_COLS // 2, 0, GRID_W - NA_COLS)
    kc = jnp.arange(GRID_W)
    valid_c = (kc[None, :] >= cstart[:, None]) & (kc[None, :] < cstart[:, None] + NA_COLS)
    dcol = jnp.clip(kc[None, :] - col[:, None] + (NA_COLS - 1), 0, 2 * NA_COLS - 2)
    tab = jnp.where(valid_c[None, None], rpb[:, :, dcol], NEG)
    j = jnp.arange(NA_G)
    off = (NA_G * jnp.arange(3))[:, None, None]
    kr_lo = jnp.stack([jnp.zeros_like(j), j, jnp.full_like(j, NA_G)])[:, :, None]
    kr = jnp.arange(NA_WIN)[None, None, :]
    valid_r = (kr >= kr_lo) & (kr < kr_lo + NA_ROWS)
    drow = jnp.clip(kr - off - j[None, :, None] + (NA_ROWS - 1), 0, 2 * NA_ROWS - 2)
    t = jnp.where(valid_r[None, :, :, :, None, None], tab[:, drow], NEG)
    t = t.transpose(1, 0, 2, 4, 3, 5)
    return t.reshape(3, NA_HEADS, NA_G * GRID_W, NA_WIN * GRID_W).astype(F32)


def _local_kernel(up_p, up_c, up_n, cg_p, cg_c, cg_n, xc_p, xc_c, xc_n, bg_ref,
                  pw_ref, ps_ref, cw_ref, ob_ref, od_ref, *, lat_tiles, tiles_lat_seq, s_len, c_len):
    i = pl.program_id(0)
    is_lat = i < lat_tiles
    j = jnp.where(is_lat, i % tiles_lat_seq, 0)
    n_seq = jnp.where(is_lat, s_len, c_len)
    first = j == 0
    last = (j + 1) * TL == n_seq

    def ext(p_ref, c_ref, n_ref):
        p = jnp.where(first, 0.0, p_ref[...].astype(F32))
        n = jnp.where(last, 0.0, n_ref[...].astype(F32))
        return jnp.concatenate([p, c_ref[...].astype(F32), n], axis=0)

    z = ext(cg_p, cg_c, cg_n) * ext(xc_p, xc_c, xc_n)
    cw = cw_ref[...]
    y = (cw[0:1] * z[HALO - 1:HALO - 1 + TL] + cw[1:2] * z[HALO:HALO + TL]
         + cw[2:3] * z[HALO + 1:HALO + 1 + TL])
    od_ref[...] = (bg_ref[...].astype(F32) * y).astype(BF16)

    u = ext(up_p, up_c, up_n)
    t = j * TL + lax.broadcasted_iota(jnp.int32, (TL, 1), 0)
    ps = ps_ref[...]
    for g, w in enumerate(POOL_WINDOWS):
        sl = slice(POOL_GROUP * g, POOL_GROUP * (g + 1))
        ug = u[:, sl]
        acc = ug[HALO - w // 2:HALO - w // 2 + TL]
        for d in range(-w // 2 + 1, w // 2):
            acc = acc + ug[HALO + d:HALO + d + TL]
        cnt = (jnp.minimum(t - w // 2 + w, n_seq) - jnp.maximum(t - w // 2, 0)).astype(F32)
        pooled = acc / cnt - ug[HALO:HALO + TL]
        mixed = _dot(pooled.astype(BF16), pw_ref[g])
        ob_ref[:, sl] = (mixed * ps[:, sl]).astype(BF16)


def _local(proj, pool_w, pool_scale, conv_w, nb, s_len, c_len):
    r = proj.shape[0]
    lat_tiles = nb * s_len // TL
    hpt = TL // HALO
    nhalo = r // HALO

    def cur(off):
        return pl.BlockSpec((TL, 512), lambda i: (i, off // 512))

    def prev(off):
        return pl.BlockSpec((HALO, 512), lambda i: (jnp.maximum(i * hpt - 1, 0), off // 512))

    def nxt(off):
        return pl.BlockSpec((HALO, 512), lambda i: (jnp.minimum((i + 1) * hpt, nhalo - 1), off // 512))

    specs = []
    for off in (OFF_UP, OFF_CG, OFF_XC):
        specs += [prev(off), cur(off), nxt(off)]
    specs += [cur(OFF_BG),
              pl.BlockSpec((4, POOL_GROUP, POOL_GROUP), lambda i: (0, 0, 0)),
              pl.BlockSpec((1, 512), lambda i: (0, 0)),
              pl.BlockSpec((CONV_K, 512), lambda i: (0, 0))]
    return pl.pallas_call(
        functools.partial(_local_kernel, lat_tiles=lat_tiles, tiles_lat_seq=s_len // TL,
                          s_len=s_len, c_len=c_len),
        grid=(r // TL,),
        in_specs=specs,
        out_specs=[pl.BlockSpec((TL, 512), lambda i: (i, 0)), pl.BlockSpec((TL, 512), lambda i: (i, 0))],
        out_shape=[jax.ShapeDtypeStruct((r, 512), BF16), jax.ShapeDtypeStruct((r, 512), BF16)],
        compiler_params=_cparams(("parallel",)),
        name="local_mix",
    )(*([proj] * 10), pool_w, pool_scale, conv_w)


def _merge_kernel(oa_ref, ob_ref, oc_ref, od_ref, gate_ref, wb_ref, wo_ref, x_ref, gm_ref, g_ref, o_ref):
    merged = None
    for k, o in enumerate((oa_ref, ob_ref, oc_ref, od_ref)):
        proj = _dot(o[...], wb_ref[k])
        gk = jax.nn.sigmoid(gate_ref[:, 1024 * k:1024 * (k + 1)].astype(F32))
        merged = gk * proj if merged is None else merged + gk * proj
    y = _dot(merged.astype(BF16), wo_ref[...])
    o_ref[...] = x_ref[...] + gm_ref[...] * _rms(y, g_ref[...])


def _merge(oa, ob, oc, od, proj, wb, wo, xall, gate_mod, g, n_rows, tiles_per_batch):
    d = xall.shape[1]
    tm = 256
    tpb = tiles_per_batch * (TM // tm)
    row = lambda w: pl.BlockSpec((tm, w), lambda i: (i, 0))
    return pl.pallas_call(
        _merge_kernel,
        grid=(n_rows // tm,),
        in_specs=[row(512), row(512), row(512), row(512), row(4096),
                  pl.BlockSpec((4, 512, d), lambda i: (0, 0, 0)),
                  pl.BlockSpec((d, d), lambda i: (0, 0)),
                  row(d),
                  pl.BlockSpec((None, 1, d), lambda i: (i // tpb, 0, 0)),
                  pl.BlockSpec((1, d), lambda i: (0, 0))],
        out_specs=row(d),
        out_shape=jax.ShapeDtypeStruct((n_rows, d), F32),
        compiler_params=_cparams(("parallel",)),
        name="merge",
    )(oa, ob, oc, od, proj, wb, wo, xall, gate_mod, g.reshape(1, d))


def _ffn_kernel(x_ref, g2_ref, sh_ref, sc_ref, gm_ref, g3_ref, w1_ref, w3_ref, w2_ref, o_ref, h_ref, acc_ref):
    f = pl.program_id(1)

    @pl.when(f == 0)
    def _():
        h = _rms(x_ref[...], g2_ref[...]) * (1.0 + sc_ref[...]) + sh_ref[...]
        h_ref[...] = h.astype(BF16)
        acc_ref[...] = jnp.zeros_like(acc_ref)

    h = h_ref[...]
    a = _dot(h, w1_ref[...])
    b = _dot(h, w3_ref[...])
    acc_ref[...] += _dot((a * jax.nn.sigmoid(a) * b).astype(BF16), w2_ref[...])

    @pl.when(f == pl.num_programs(1) - 1)
    def _():
        o_ref[...] = x_ref[...] + gm_ref[...] * _rms(acc_ref[...], g3_ref[...])


def _ffn(xall, g2, shift, scale, gate_mod, g3, w1, w3, w2, tiles_per_batch):
    r, d = xall.shape
    dff = w1.shape[1]
    tf = dff // 2
    mod = pl.BlockSpec((None, 1, d), lambda i, f: (i // tiles_per_batch, 0, 0))
    vec = pl.BlockSpec((1, d), lambda i, f: (0, 0))
    return pl.pallas_call(
        _ffn_kernel,
        grid=(r // TM, dff // tf),
        in_specs=[pl.BlockSpec((TM, d), lambda i, f: (i, 0)), vec, mod, mod, mod, vec,
                  pl.BlockSpec((d, tf), lambda i, f: (0, f)),
                  pl.BlockSpec((d, tf), lambda i, f: (0, f)),
                  pl.BlockSpec((tf, d), lambda i, f: (f, 0))],
        out_specs=pl.BlockSpec((TM, d), lambda i, f: (i, 0)),
        out_shape=jax.ShapeDtypeStruct((r, d), F32),
        scratch_shapes=[pltpu.VMEM((TM, d), BF16), pltpu.VMEM((TM, d), F32)],
        compiler_params=_cparams(("parallel", "arbitrary")),
        name="ffn_dense",
    )(xall, g2.reshape(1, d), shift, scale, gate_mod, g3.reshape(1, d), w1, w3, w2)


def _router_kernel(x_ref, g2_ref, sh_ref, sc_ref, rt_ref, t_ref, idx_ref, w_ref):
    t = _rms(x_ref[...], g2_ref[...]) * (1.0 + sc_ref[...]) + sh_ref[...]
    t_ref[...] = t
    logits = lax.dot_general(rt_ref[...], t, (((1,), (1,)), ((), ())), preferred_element_type=F32,
                             precision=lax.Precision.HIGHEST)
    e = lax.broadcasted_iota(jnp.int32, logits.shape, 0).astype(F32)
    m1 = jnp.max(logits, axis=0, keepdims=True)
    i1 = jnp.min(jnp.where(logits == m1, e, float(N_EXPERTS)), axis=0, keepdims=True)
    rest = jnp.where(e == i1, -jnp.inf, logits)
    m2 = jnp.max(rest, axis=0, keepdims=True)
    i2 = jnp.min(jnp.where(rest == m2, e, float(N_EXPERTS)), axis=0, keepdims=True)
    ex = jnp.exp(m2 - m1)
    w1 = 1.0 / (1.0 + ex)
    idx_ref[0:1, :] = i1.astype(jnp.int32)
    idx_ref[1:2, :] = i2.astype(jnp.int32)
    w_ref[0:1, :] = w1
    w_ref[1:2, :] = ex * w1


def _router(x, g2, shift, scale, router_t, tiles_per_batch):
    t_rows, d = x.shape
    mod = pl.BlockSpec((None, 1, d), lambda i: (i // tiles_per_batch, 0, 0))
    return pl.pallas_call(
        _router_kernel,
        grid=(t_rows // TM,),
        in_specs=[pl.BlockSpec((TM, d), lambda i: (i, 0)),
                  pl.BlockSpec((1, d), lambda i: (0, 0)), mod, mod,
                  pl.BlockSpec((N_EXPERTS, d), lambda i: (0, 0))],
        out_specs=[pl.BlockSpec((TM, d), lambda i: (i, 0)),
                   pl.BlockSpec((TOP_K, TM), lambda i: (0, i)),
                   pl.BlockSpec((TOP_K, TM), lambda i: (0, i))],
        out_shape=[jax.ShapeDtypeStruct((t_rows, d), F32),
                   jax.ShapeDtypeStruct((TOP_K, t_rows), jnp.int32),
                   jax.ShapeDtypeStruct((TOP_K, t_rows), F32)],
        compiler_params=_cparams(("parallel",)),
        name="router",
    )(x, g2.reshape(1, d), shift, scale, router_t)


def _row_gather_start(ids_ref, src_ref, buf_ref, sem, n):
    def issue(r, c):
        pltpu.make_async_copy(src_ref.at[pl.ds(ids_ref[0, r], 1)], buf_ref.at[pl.ds(r, 1)], sem).start()
        return c

    lax.fori_loop(0, n, issue, 0, unroll=8)


def _row_gather_wait(src_ref, buf_ref, sem, n):
    pltpu.make_async_copy(src_ref.at[pl.ds(0, n)], buf_ref, sem).wait()


def _moe_ffn_kernel(eid_ref, nused_ref, ids_ref, ids_next_ref, t_ref, w1_ref, w3_ref, w2_ref, o_ref,
                    xbuf_ref, xs_ref, acc_ref, sems):
    i = pl.program_id(0)
    f = pl.program_id(1)
    n_f = pl.num_programs(1)
    slot = lax.rem(i, 2)

    def row_copy(ids, r, s):
        return pltpu.make_async_copy(t_ref.at[pl.ds(ids[0, r], 1)], xbuf_ref.at[s, pl.ds(r, 1)], sems.at[s])

    def all_rows(s):
        return pltpu.make_async_copy(t_ref.at[pl.ds(0, TM)], xbuf_ref.at[s], sems.at[s])

    @pl.when((i == 0) & (f == 0))
    def _():
        def issue(r, c):
            row_copy(ids_ref, r, 0).start()
            return c

        lax.fori_loop(0, TM, issue, 0)

    @pl.when(i < nused_ref[0])
    def _():
        @pl.when(f == 0)
        def _():
            all_rows(slot).wait()
            xs_ref[...] = xbuf_ref[slot].astype(BF16)
            acc_ref[...] = jnp.zeros_like(acc_ref)

        for rr in range(TM // MOE_F_STEPS):
            row_copy(ids_next_ref, f * (TM // MOE_F_STEPS) + rr, 1 - slot).start()

        x = xs_ref[...]
        a = _dot(x, w1_ref[...])
        b = _dot(x, w3_ref[...])
        acc_ref[...] += _dot((a * jax.nn.sigmoid(a) * b).astype(BF16), w2_ref[...])

        @pl.when(f == n_f - 1)
        def _():
            o_ref[...] = acc_ref[...]

    @pl.when((i + 1 == nused_ref[0]) & (f == n_f - 1))
    def _():
        all_rows(1 - slot).wait()

    @pl.when((i >= nused_ref[0]) & (f == n_f - 1))
    def _():
        o_ref[...] = jnp.zeros_like(o_ref)


def _moe_ffn(t, slot_tok, tile_eid, n_used, w1, w3, w2):
    n_tiles = slot_tok.shape[0]
    d = t.shape[1]
    dff = w1.shape[2]
    tf = dff // MOE_F_STEPS
    wf = lambda i, f, nu: jnp.where(i < nu[0], f, MOE_F_STEPS - 1)
    ids = lambda nxt: pl.BlockSpec((None, 1, TM),
                                   lambda i, f, eid, nu: (jnp.minimum(i + nxt, n_tiles - 1), 0, 0),
                                   memory_space=pltpu.SMEM)
    grid_spec = pltpu.PrefetchScalarGridSpec(
        num_scalar_prefetch=2,
        grid=(n_tiles, MOE_F_STEPS),
        in_specs=[ids(0), ids(1),
                  pl.BlockSpec(memory_space=pl.ANY),
                  pl.BlockSpec((None, d, tf), lambda i, f, eid, nu: (eid[i], 0, wf(i, f, nu))),
                  pl.BlockSpec((None, d, tf), lambda i, f, eid, nu: (eid[i], 0, wf(i, f, nu))),
                  pl.BlockSpec((None, tf, d), lambda i, f, eid, nu: (eid[i], wf(i, f, nu), 0))],
        out_specs=pl.BlockSpec((TM, d), lambda i, f, eid, nu: (i, 0)),
        scratch_shapes=[pltpu.VMEM((2, TM, d), F32), pltpu.VMEM((TM, d), BF16), pltpu.VMEM((TM, d), F32),
                        pltpu.SemaphoreType.DMA((2,))])
    return pl.pallas_call(
        _moe_ffn_kernel,
        grid_spec=grid_spec,
        out_shape=jax.ShapeDtypeStruct((n_tiles * TM, d), F32),
        compiler_params=_cparams(("arbitrary", "arbitrary")),
        name="moe_ffn",
    )(tile_eid, n_used, slot_tok, slot_tok, t, w1, w3, w2)


def _combine_kernel(p0_ref, p1_ref, ys_ref, w_ref, x_ref, gm_ref, g3_ref, o_ref, b0_ref, b1_ref, sem0, sem1):
    _row_gather_start(p0_ref, ys_ref, b0_ref, sem0, TM)
    _row_gather_start(p1_ref, ys_ref, b1_ref, sem1, TM)
    _row_gather_wait(ys_ref, b0_ref, sem0, TM)
    _row_gather_wait(ys_ref, b1_ref, sem1, TM)
    w = w_ref[...]
    y = w[:, 0:1] * b0_ref[...] + w[:, 1:2] * b1_ref[...]
    o_ref[...] = x_ref[...] + gm_ref[...] * _rms(y, g3_ref[...])


def _combine(pos0, pos1, ys, wcol, x, gate_mod, g3, tiles_per_batch):
    t_rows, d = x.shape
    ids = pl.BlockSpec((None, 1, TM), lambda i: (i, 0, 0), memory_space=pltpu.SMEM)
    return pl.pallas_call(
        _combine_kernel,
        grid=(t_rows // TM,),
        in_specs=[ids, ids,
                  pl.BlockSpec(memory_space=pl.ANY),
                  pl.BlockSpec((TM, TOP_K), lambda i: (i, 0)),
                  pl.BlockSpec((TM, d), lambda i: (i, 0)),
                  pl.BlockSpec((None, 1, d), lambda i: (i // tiles_per_batch, 0, 0)),
                  pl.BlockSpec((1, d), lambda i: (0, 0))],
        out_specs=pl.BlockSpec((TM, d), lambda i: (i, 0)),
        out_shape=jax.ShapeDtypeStruct((t_rows, d), F32),
        scratch_shapes=[pltpu.VMEM((TM, d), F32), pltpu.VMEM((TM, d), F32),
                        pltpu.SemaphoreType.DMA(()), pltpu.SemaphoreType.DMA(())],
        compiler_params=_cparams(("arbitrary",)),
        name="moe_combine",
    )(pos0, pos1, ys, wcol, x, gate_mod, g3.reshape(1, d))


def _routing_tables(top_i, t_rows):
    n_assign = TOP_K * t_rows
    n_tiles = n_assign // TM + N_EXPERTS
    e_flat = top_i.reshape(n_assign)
    onehot = (e_flat[:, None] == jnp.arange(N_EXPERTS)[None, :]).astype(jnp.int32)
    csum = jnp.cumsum(onehot, axis=0)
    counts = csum[-1]
    rank = jnp.sum((csum - onehot) * onehot, axis=1)
    tiles_e = (counts + TM - 1) // TM
    tile_end = jnp.cumsum(tiles_e)
    tile_start = tile_end - tiles_e
    slot = (tile_start * TM)[e_flat] + rank
    tok = jnp.tile(jnp.arange(t_rows, dtype=jnp.int32), TOP_K)
    slot_tok = jnp.zeros((n_tiles * TM,), jnp.int32).at[slot].set(tok)
    n_used = tile_end[-1]
    tile_ids = jnp.arange(n_tiles)
    tile_eid = jnp.sum(tile_ids[:, None] >= tile_end[None, :], axis=1)
    last_eid = jnp.sum(n_used - 1 >= tile_end)
    tile_eid = jnp.where(tile_ids < n_used, tile_eid, last_eid).astype(jnp.int32)
    pos = slot.reshape(TOP_K, t_rows).astype(jnp.int32)
    return (slot_tok.reshape(n_tiles, 1, TM), tile_eid, n_used.reshape(1).astype(jnp.int32),
            pos[0].reshape(t_rows // TM, 1, TM), pos[1].reshape(t_rows // TM, 1, TM))


def _prep_w_in(w):
    d = w.shape[0]
    q_a, k_a, v_a, u_p, c_q, c_kv, k_r, b_g, c_g, x_c, gate = jnp.split(w, IN_SPLITS, axis=1)
    z = lambda n: jnp.zeros((d, n), w.dtype)
    kr_e, kr_o = k_r[:, 0::2], k_r[:, 1::2]
    cols = [gate, q_a * NA_SCALE, k_a, v_a, u_p, b_g, c_g, x_c, c_q, z(128), c_kv,
            z(64), kr_e, kr_o, -kr_o, kr_e, z(128)]
    return jnp.concatenate(cols, axis=1).astype(BF16)


def _prep_w_uq(w_uq):
    wq = w_uq.reshape(MLA_Q_LORA, MLA_HEADS, MLA_NOPE + MLA_ROPE)
    nope, r = wq[..., :MLA_NOPE], wq[..., MLA_NOPE:]
    re, ro = r[..., 0::2], r[..., 1::2]
    ext = jnp.concatenate([nope, re, ro, -ro, re], axis=-1).reshape(MLA_Q_LORA, MLA_HEADS * 128)
    return jnp.pad(ext, ((0, 512 - MLA_Q_LORA), (0, 0))).T.astype(BF16)


def _prep_w_ukv(w_ukv):
    wkv = w_ukv.reshape(MLA_KV_LORA, MLA_HEADS, MLA_NOPE + MLA_V)
    zeros = jnp.zeros((MLA_KV_LORA, MLA_HEADS, 64), w_ukv.dtype)
    kn = jnp.concatenate([wkv[..., :MLA_NOPE], zeros], axis=-1).reshape(MLA_KV_LORA, MLA_HEADS * 128)
    v = wkv[..., MLA_NOPE:]
    odd = (jnp.arange(MLA_HEADS) % 2 == 1)[None, :, None]
    vt = jnp.where(odd, jnp.concatenate([zeros, v], axis=-1), jnp.concatenate([v, zeros], axis=-1))
    return kn.astype(BF16), vt.reshape(MLA_KV_LORA, MLA_HEADS * 128).T.astype(BF16)


def _rope_tables(nb, s_len, n_ctx_rows):
    pos = jnp.arange(s_len)
    row = (pos // GRID_W).astype(F32)
    col = (pos % GRID_W).astype(F32)
    n_pairs = MLA_ROPE // 4
    inv_freq = ROPE_BASE ** (-jnp.arange(n_pairs, dtype=F32) / n_pairs)
    ang = jnp.concatenate([row[:, None] * inv_freq, col[:, None] * inv_freq], axis=-1)
    cos = jnp.concatenate([jnp.tile(jnp.cos(ang), (nb, 1)), jnp.ones((n_ctx_rows, 16), F32)], axis=0)
    sin = jnp.concatenate([jnp.tile(jnp.sin(ang), (nb, 1)), jnp.zeros((n_ctx_rows, 16), F32)], axis=0)
    r = cos.shape[0]
    cos2 = jnp.concatenate([cos, cos], axis=1)
    sin2 = jnp.concatenate([sin, sin], axis=1)
    pad = lambda t, lead: jnp.concatenate([lead, t, jnp.zeros((r, 32), F32)], axis=1)
    return (pad(cos2, jnp.ones((r, 64), F32)), pad(sin2, jnp.zeros((r, 64), F32)), cos2.T, sin2.T)


def kernel(x, c, ctx, c_ctx, w_ada, b_ada, g_norm, w_in, na_rpb, pool_w, pool_scale, mla_g_q, mla_w_uq,
           mla_g_kv, mla_w_ukv, conv_w, w_branch, w_out, ffn_w1, ffn_w3, ffn_w2, moe_router, moe_w1,
           moe_w3, moe_w2):
    nb, s_len, d = x.shape
    c_len = ctx.shape[1]
    depth = w_in.shape[0]
    t_rows = nb * s_len
    n_ctx = nb * c_len
    assert s_len % TM == 0 and n_ctx == TM and s_len % GRID_W == 0 and t_rows % c_len == 0
    assert c_len == KC and c_len == TL
    tpb = s_len // TM
    rows = s_len // GRID_W

    xall = jnp.concatenate([x.reshape(t_rows, d), ctx.reshape(n_ctx, d)], axis=0)
    c8 = jnp.zeros((8, d), F32).at[:nb].set(c).at[nb].set(c_ctx)
    cos_t, sin_t, cos_tt, sin_tt = _rope_tables(nb, s_len, n_ctx)

    for l in range(depth):
        last = l == depth - 1
        mod = _adaln(c8, w_ada[l], b_ada[l]).reshape(8, 6, 1, d)
        mods = [mod[:, k] for k in range(6)]

        proj = _inproj(xall, g_norm[l, 0], mods[0], mods[1], _prep_w_in(w_in[l]), tpb)

        wkn, wvt = _prep_w_ukv(mla_w_ukv[l])
        gq = jnp.pad(mla_g_q[l], (0, 512 - MLA_Q_LORA)).reshape(1, 512)
        qt_m, k_m, vt_m = _mla_prep(proj, gq, mla_g_kv[l].reshape(1, MLA_KV_LORA), _prep_w_uq(mla_w_uq[l]),
                                    wkn, wvt, cos_t, sin_t, cos_tt, sin_tt)
        o_c = _mla_attn(qt_m, k_m, vt_m, nb, s_len, c_len)
        o_a = _na_attn(proj, _na_bias_table(na_rpb[l]), nb, s_len, c_len)
        o_b, o_d = _local(proj, pool_w[l].astype(BF16), pool_scale[l].reshape(1, 512), conv_w[l],
                          nb, s_len, c_len)

        wb = w_branch[l].astype(BF16)
        wo = w_out[l].astype(BF16)
        if not last:
            ctx_blk = t_rows // c_len
            o_cc = _mla_ctx_attn(qt_m, k_m, vt_m, nb, t_rows, c_len)
            o_ac = _ctx_attn(proj, proj, proj, nb, c_len, ctx_blk, 64, OFF_QA, OFF_KA, OFF_VA, "na_ctx_attn")
            o_a = jnp.concatenate([o_a, o_ac], axis=0)
            o_c = jnp.concatenate([o_c, o_cc], axis=0)
            n_rows = t_rows + n_ctx
        else:
            n_rows = t_rows
        xall = _merge(o_a, o_b, o_c, o_d, proj, wb, wo, xall, mods[2], g_norm[l, 1], n_rows, tpb)

        if l % 2 == 0:
            j = l // 2
            xall = _ffn(xall, g_norm[l, 2], mods[3], mods[4], mods[5], g_norm[l, 3],
                        ffn_w1[j].astype(BF16), ffn_w3[j].astype(BF16), ffn_w2[j].astype(BF16), tpb)
        else:
            j = l // 2
            assert last, "context rows are not routed through the experts"
            t_f32, top_i, top_w = _router(xall, g_norm[l, 2], mods[3], mods[4], moe_router[j].T, tpb)
            slot_tok, tile_eid, n_used, pos0, pos1 = _routing_tables(top_i, t_rows)
            ys = _moe_ffn(t_f32, slot_tok, tile_eid, n_used, moe_w1[j].astype(BF16), moe_w3[j].astype(BF16),
                          moe_w2[j].astype(BF16))
            xall = _combine(pos0, pos1, ys, top_w.T, xall, mods[5], g_norm[l, 3], tpb)

    return xall[:t_rows].reshape(nb, s_len, d)
```

```python
import functools

import numpy as np
import jax
import jax.numpy as jnp
from jax import lax
from jax.experimental import pallas as pl
from jax.experimental.pallas import tpu as pltpu

F32 = jnp.float32
BF16 = jnp.bfloat16

GRID_W = 64
N_BRANCH = 4
BRANCH_W = 512
EPS = 1e-6
NA_HEADS = 8
NA_HEAD_DIM = 64
NA_ROWS = 8
NA_COLS = 16
NA_G = 4
NA_WIN = NA_G + NA_ROWS
POOL_WINDOWS = (2, 4, 8, 16)
POOL_GROUP = 128
MLA_HEADS = 8
MLA_Q_LORA = 384
MLA_KV_LORA = 256
MLA_NOPE = 64
MLA_ROPE = 32
MLA_V = 64
ROPE_BASE = 10000.0
CONV_K = 3
N_EXPERTS = 8
TOP_K = 2

IN_SIZES = (512, 512, 512, 512, MLA_Q_LORA, MLA_KV_LORA, MLA_ROPE, 512, 512, 512, 4096)
IN_SPLITS = tuple(int(s) for s in np.cumsum(IN_SIZES)[:-1])

OFF_GATE = 0
OFF_QA = 4096
OFF_KA = 4608
OFF_VA = 5120
OFF_UP = 5632
OFF_BG = 6144
OFF_CG = 6656
OFF_XC = 7168
OFF_CQ = 7680
OFF_CKV = 8192
OFF_KR = 8448
N_IN = 8704

NA_SCALE = NA_HEAD_DIM ** -0.5
MLA_SCALE = (MLA_NOPE + MLA_ROPE) ** -0.5
LOG2E = 1.4426950408889634
NEG = -1e30

VMEM_LIMIT = 52 * 1024 * 1024
TM = 512
TL = 256
HALO = 16
KC = 256
MOE_F_STEPS = 4


def _cparams(sem):
    return pltpu.CompilerParams(dimension_semantics=sem, vmem_limit_bytes=VMEM_LIMIT)


def _rms(xf, g):
    ms = jnp.mean(xf * xf, axis=-1, keepdims=True)
    return xf * lax.rsqrt(ms + EPS) * g


def _dot(a, b):
    return jnp.dot(a, b, preferred_element_type=F32)


def _dot_nt(a, b):
    return lax.dot_general(a, b, (((1,), (1,)), ((), ())), preferred_element_type=F32)


def _adaln_kernel(c_ref, w_ref, b_ref, o_ref):
    c = c_ref[...]
    sc = c * jax.nn.sigmoid(c)
    o_ref[...] = jnp.dot(sc, w_ref[...], preferred_element_type=F32,
                         precision=lax.Precision.HIGHEST) + b_ref[...]


def _adaln(c8, w, b):
    d = c8.shape[1]
    n = w.shape[1]
    tn = 1536
    return pl.pallas_call(
        _adaln_kernel,
        grid=(n // tn,),
        in_specs=[pl.BlockSpec((8, d), lambda j: (0, 0)),
                  pl.BlockSpec((d, tn), lambda j: (0, j)),
                  pl.BlockSpec((1, tn), lambda j: (0, j))],
        out_specs=pl.BlockSpec((8, tn), lambda j: (0, j)),
        out_shape=jax.ShapeDtypeStruct((8, n), F32),
        compiler_params=_cparams(("arbitrary",)),
        name="adaln",
    )(c8, w, b.reshape(1, n))


def _inproj_kernel(x_ref, g_ref, sh_ref, sc_ref, w_ref, o_ref, h_ref):
    @pl.when(pl.program_id(1) == 0)
    def _():
        h = _rms(x_ref[...], g_ref[...]) * (1.0 + sc_ref[...]) + sh_ref[...]
        h_ref[...] = h.astype(BF16)

    o_ref[...] = _dot(h_ref[...], w_ref[...]).astype(BF16)


def _inproj(xall, g, shift, scale, w_p, tiles_per_batch):
    r, d = xall.shape
    n = w_p.shape[1]
    tn = n // 4
    mod_spec = pl.BlockSpec((None, 1, d), lambda i, j: (i // tiles_per_batch, 0, 0))
    return pl.pallas_call(
        _inproj_kernel,
        grid=(r // TM, n // tn),
        in_specs=[pl.BlockSpec((TM, d), lambda i, j: (i, 0)),
                  pl.BlockSpec((1, d), lambda i, j: (0, 0)),
                  mod_spec, mod_spec,
                  pl.BlockSpec((d, tn), lambda i, j: (0, j))],
        out_specs=pl.BlockSpec((TM, tn), lambda i, j: (i, j)),
        out_shape=jax.ShapeDtypeStruct((r, n), BF16),
        scratch_shapes=[pltpu.VMEM((TM, d), BF16)],
        compiler_params=_cparams(("parallel", "arbitrary")),
        name="inproj",
    )(xall, g.reshape(1, d), shift, scale, w_p)


def _mla_prep_kernel(cq_ref, ckv_ref, kr_ref, gq_ref, gkv_ref, wqt_ref, wkn_ref, wvt_ref,
                     cos_ref, sin_ref, cost_ref, sint_ref, qt_out, k_out, vt_out):
    cq = cq_ref[...].astype(F32)
    ms = jnp.sum(cq * cq, axis=-1, keepdims=True) * (1.0 / MLA_Q_LORA)
    cqn = (cq * lax.rsqrt(ms + EPS) * gq_ref[...]).astype(BF16)
    qt = _dot_nt(wqt_ref[...], cqn) * (MLA_SCALE * LOG2E)
    cost = cost_ref[...]
    sint = sint_ref[...]
    for h in range(MLA_HEADS):
        b = 128 * h
        qt_out[b:b + 64, :] = qt[b:b + 64].astype(BF16)
        qt_out[b + 64:b + 96, :] = (qt[b + 64:b + 96] * cost + qt[b + 96:b + 128] * sint).astype(BF16)
        qt_out[b + 96:b + 128, :] = jnp.zeros((32, TM), BF16)

    ckvn = _rms(ckv_ref[...].astype(F32), gkv_ref[...]).astype(BF16)
    kn = _dot(ckvn, wkn_ref[...])
    kseg = kr_ref[...].astype(F32)
    krt = kseg * cos_ref[...] + pltpu.roll(kseg, 96, 1) * sin_ref[...]
    for h in range(MLA_HEADS):
        sl = slice(128 * h, 128 * (h + 1))
        k_out[:, sl] = (kn[:, sl] + krt).astype(BF16)

    vt = _dot_nt(wvt_ref[...], ckvn)
    row = lax.broadcasted_iota(jnp.int32, vt.shape, 0)
    ones = ((row & 127) >= 64) != (((row >> 7) & 1) == 1)
    vt = jnp.where(ones, 1.0, vt).astype(BF16)
    for j in range(TM // KC):
        vt_out[j] = vt[:, KC * j:KC * (j + 1)]


def _mla_prep(proj, gq, gkv, wqt, wkn, wvt, cos_t, sin_t, cos_tt, sin_tt):
    r = proj.shape[0]
    full = lambda shape: pl.BlockSpec(shape, lambda i: (0, 0))
    return pl.pallas_call(
        _mla_prep_kernel,
        grid=(r // TM,),
        in_specs=[pl.BlockSpec((TM, 512), lambda i: (i, OFF_CQ // 512)),
                  pl.BlockSpec((TM, 256), lambda i: (i, OFF_CKV // 256)),
                  pl.BlockSpec((TM, 128), lambda i: (i, OFF_KR // 128)),
                  full((1, 512)), full((1, 256)),
                  full((1024, 512)), full((256, 1024)), full((1024, 256)),
                  pl.BlockSpec((TM, 128), lambda i: (i, 0)),
                  pl.BlockSpec((TM, 128), lambda i: (i, 0)),
                  pl.BlockSpec((32, TM), lambda i: (0, i)),
                  pl.BlockSpec((32, TM), lambda i: (0, i))],
        out_specs=[pl.BlockSpec((1024, TM), lambda i: (0, i)),
                   pl.BlockSpec((TM, 1024), lambda i: (i, 0)),
                   pl.BlockSpec((TM // KC, 1024, KC), lambda i: (i, 0, 0))],
        out_shape=[jax.ShapeDtypeStruct((1024, r), BF16),
                   jax.ShapeDtypeStruct((r, 1024), BF16),
                   jax.ShapeDtypeStruct((r // KC, 1024, KC), BF16)],
        compiler_params=_cparams(("parallel",)),
        name="mla_prep",
    )(proj, proj, proj, gq, gkv, wqt, wkn, wvt, cos_t, sin_t, cos_tt, sin_tt)


def _pick_heads(o0, o1):
    lane = lax.broadcasted_iota(jnp.int32, o0.shape, 1)
    return jnp.where(lane < 64, o0, o1)


def _mla_update(s, vt, m_old, acc_ref):
    m_new = jnp.maximum(m_old, jnp.max(s, axis=0, keepdims=True))
    alpha = jnp.exp2(m_old - m_new)
    p = jnp.exp2(s - m_new).astype(BF16)
    acc_ref[...] = alpha * acc_ref[...] + _dot(vt, p)
    return m_new


def _mla_finish(a0, a1):
    return _pick_heads(a0 / a0[:, 64:65], a1 / a1[:, 0:1]).astype(BF16)


def _mla_attn_kernel(qt_ref, k_ref, vt_ref, kc_ref, vtc_ref, o_ref, *acc_refs, cpi):
    nt = qt_ref.shape[1] // 256
    units = [(t, h) for h in range(2) for t in range(nt)]
    qts = [qt_ref[128 * h:128 * (h + 1), 256 * t:256 * (t + 1)] for t, h in units]
    n_chunks = k_ref.shape[0] // KC
    for acc_ref in acc_refs:
        acc_ref[...] = jnp.zeros_like(acc_ref)

    def scores(kget):
        return tuple(_dot(kget(h), qts[u]) for u, (t, h) in enumerate(units))

    def latent_keys(j):
        off = pl.multiple_of(j * KC, KC)
        return lambda h: k_ref[pl.ds(off, KC), 128 * h:128 * (h + 1)]

    def step(s_cur, vget, ms, kget_next):
        s_next, ms_new = [], []
        for u, (t, h) in enumerate(units):
            if kget_next is not None:
                s_next.append(_dot(kget_next(h), qts[u]))
            ms_new.append(_mla_update(s_cur[u], vget(h), ms[u], acc_refs[u]))
        return tuple(s_next), tuple(ms_new)

    ms = tuple(jnp.full((1, 256), -jnp.inf, F32) for _ in units)
    s_ctx = scores(lambda h: kc_ref[:, 128 * h:128 * (h + 1)])
    s_cur, ms = step(s_ctx, lambda h: vtc_ref[0, 128 * h:128 * (h + 1), :], ms, latent_keys(0))

    def values(j):
        return lambda h: vt_ref[j, 128 * h:128 * (h + 1), :]

    def body(i, carry):
        ms, s_cur = carry
        for c in range(cpi):
            j = i * cpi + c
            s_cur, ms = step(s_cur, values(j), ms, latent_keys(j + 1))
        return ms, s_cur

    n_iter = n_chunks // cpi - 1
    ms, s_cur = lax.fori_loop(0, n_iter, body, (ms, s_cur))
    for j in range(n_iter * cpi, n_chunks):
        s_cur, ms = step(s_cur, values(j), ms, latent_keys(j + 1) if j + 1 < n_chunks else None)
    for t in range(nt):
        o_ref[256 * t:256 * (t + 1), :] = _mla_finish(acc_refs[units.index((t, 0))][...].T,
                                                      acc_refs[units.index((t, 1))][...].T)


def _mla_attn(qt, k, vt, nb, s_len, c_len):
    tq = min(512, s_len)
    nq = s_len // tq
    t_rows = nb * s_len
    cpi = 4 if (s_len // KC) % 4 == 0 else 1
    return pl.pallas_call(
        functools.partial(_mla_attn_kernel, cpi=cpi),
        grid=(nb, 4, nq),
        in_specs=[pl.BlockSpec((256, tq), lambda b, hp, i: (hp, b * nq + i)),
                  pl.BlockSpec((s_len, 256), lambda b, hp, i: (b, hp)),
                  pl.BlockSpec((s_len // KC, 256, KC), lambda b, hp, i: (b, hp, 0)),
                  pl.BlockSpec((c_len, 256), lambda b, hp, i: (t_rows // c_len + b, hp)),
                  pl.BlockSpec((1, 256, KC), lambda b, hp, i: (t_rows // KC + b, hp, 0))],
        out_specs=pl.BlockSpec((tq, 128), lambda b, hp, i: (b * nq + i, hp)),
        out_shape=jax.ShapeDtypeStruct((t_rows, 512), BF16),
        scratch_shapes=[pltpu.VMEM((128, 256), F32)] * (2 * (tq // 256)),
        compiler_params=_cparams(("parallel", "parallel", "arbitrary")),
        name="mla_attn",
    )(qt, k, vt, k, vt)


def _mla_ctx_attn_kernel(qt_ref, kc_ref, vtc_ref, o_ref):
    outs = []
    for h in range(2):
        s = _dot(kc_ref[:, 128 * h:128 * (h + 1)], qt_ref[128 * h:128 * (h + 1), :])
        p = jnp.exp2(s - jnp.max(s, axis=0, keepdims=True)).astype(BF16)
        outs.append(_dot(vtc_ref[0, 128 * h:128 * (h + 1), :], p).T)
    o_ref[...] = _mla_finish(outs[0], outs[1])


def _mla_ctx_attn(qt, k, vt, nb, t_rows, c_len):
    blk0 = t_rows // c_len
    return pl.pallas_call(
        _mla_ctx_attn_kernel,
        grid=(nb, 4),
        in_specs=[pl.BlockSpec((256, c_len), lambda b, hp: (hp, blk0 + b)),
                  pl.BlockSpec((c_len, 256), lambda b, hp: (blk0 + b, hp)),
                  pl.BlockSpec((1, 256, KC), lambda b, hp: (blk0 + b, hp, 0))],
        out_specs=pl.BlockSpec((c_len, 128), lambda b, hp: (b, hp)),
        out_shape=jax.ShapeDtypeStruct((nb * c_len, 512), BF16),
        compiler_params=_cparams(("parallel", "parallel")),
        name="mla_ctx_attn",
    )(qt, k, vt)


def _ctx_attn_kernel(q_ref, k_ref, v_ref, o_ref, *, dqk):
    v = v_ref[...]
    outs = []
    for h in range(2):
        sl = slice(dqk * h, dqk * (h + 1))
        s = _dot_nt(q_ref[:, sl], k_ref[:, sl])
        m = jnp.max(s, axis=-1, keepdims=True)
        p = jnp.exp(s - m)
        l = jnp.sum(p, axis=-1, keepdims=True)
        outs.append(_dot(p.astype(BF16), v) / l)
    o_ref[...] = _pick_heads(outs[0], outs[1]).astype(BF16)


def _ctx_attn(q, k, v, nb, c_len, row_blk0, dqk, qcol, kcol, vcol, name):
    w = 2 * dqk
    return pl.pallas_call(
        functools.partial(_ctx_attn_kernel, dqk=dqk),
        grid=(nb, 4),
        in_specs=[pl.BlockSpec((c_len, w), lambda b, hp: (row_blk0 + b, qcol // w + hp)),
                  pl.BlockSpec((c_len, w), lambda b, hp: (row_blk0 + b, kcol // w + hp)),
                  pl.BlockSpec((c_len, 128), lambda b, hp: (row_blk0 + b, vcol // 128 + hp))],
        out_specs=pl.BlockSpec((c_len, 128), lambda b, hp: (b, hp)),
        out_shape=jax.ShapeDtypeStruct((nb * c_len, 512), BF16),
        compiler_params=_cparams(("parallel", "parallel")),
        name=name,
    )(q, k, v)


def _na_attn_kernel(q_ref, k_ref, v_ref, kc_ref, vc_ref, bias_ref, o_ref, *, rows):
    nq = NA_G * GRID_W
    nk = NA_WIN * GRID_W
    vc = vc_ref[...]

    def body(g, carry):
        r0 = g * NA_G
        ws = jnp.clip(r0 - NA_ROWS // 2, 0, rows - NA_WIN)
        pat = lax.shift_right_logical(r0 - ws, 2)
        qoff = pl.multiple_of(r0 * GRID_W, nq)
        koff = pl.multiple_of(ws * GRID_W, GRID_W)
        q = q_ref[pl.ds(qoff, nq), :]
        kw = k_ref[pl.ds(koff, nk), :]
        vw = v_ref[pl.ds(koff, nk), :]
        scores = []
        for h in range(2):
            sl = slice(64 * h, 64 * (h + 1))
            scores.append((_dot_nt(q[:, sl], kw[:, sl]) + bias_ref[pat, h], _dot_nt(q[:, sl], kc_ref[:, sl])))
        outs = []
        for s, sc in scores:
            m = jnp.maximum(jnp.max(s, axis=-1, keepdims=True), jnp.max(sc, axis=-1, keepdims=True))
            p = jnp.exp(s - m)
            pc = jnp.exp(sc - m)
            l = jnp.sum(p, axis=-1, keepdims=True) + jnp.sum(pc, axis=-1, keepdims=True)
            outs.append((_dot(p.astype(BF16), vw) + _dot(pc.astype(BF16), vc)) / l)
        o_ref[pl.ds(qoff, nq), :] = _pick_heads(outs[0], outs[1]).astype(BF16)
        return carry

    lax.fori_loop(0, rows // NA_G, body, 0)


def _na_attn(proj, bias, nb, s_len, c_len):
    rows = s_len // GRID_W
    assert NA_G == 4 and rows % NA_G == 0 and rows >= NA_WIN
    ctx_blk = nb * s_len // c_len
    return pl.pallas_call(
        functools.partial(_na_attn_kernel, rows=rows),
        grid=(nb, 4),
        in_specs=[pl.BlockSpec((s_len, 128), lambda b, hp: (b, OFF_QA // 128 + hp)),
                  pl.BlockSpec((s_len, 128), lambda b, hp: (b, OFF_KA // 128 + hp)),
                  pl.BlockSpec((s_len, 128), lambda b, hp: (b, OFF_VA // 128 + hp)),
                  pl.BlockSpec((c_len, 128), lambda b, hp: (ctx_blk + b, OFF_KA // 128 + hp)),
                  pl.BlockSpec((c_len, 128), lambda b, hp: (ctx_blk + b, OFF_VA // 128 + hp)),
                  pl.BlockSpec((3, 2, NA_G * GRID_W, NA_WIN * GRID_W), lambda b, hp: (0, hp, 0, 0))],
        out_specs=pl.BlockSpec((s_len, 128), lambda b, hp: (b, hp)),
        out_shape=jax.ShapeDtypeStruct((nb * s_len, 512), BF16),
        compiler_params=_cparams(("parallel", "parallel")),
        name="na_attn",
    )(proj, proj, proj, proj, proj, bias)


def _na_bias_table(rpb):
    col = jnp.arange(GRID_W)
    cstart = jnp.clip(col - NA_COLS // 2, 0, GRID_W - NA_COLS)
    kc = jnp.arange(GRID_W)
    valid_c = (kc[None, :] >= cstart[:, None]) & (kc[None, :] < cstart[:, None] + NA_COLS)
    dcol = jnp.clip(kc[None, :] - col[:, None] + (NA_COLS - 1), 0, 2 * NA_COLS - 2)
    tab = jnp.where(valid_c[None, None], rpb[:, :, dcol], NEG)
    j = jnp.arange(NA_G)
    off = (NA_G * jnp.arange(3))[:, None, None]
    kr_lo = jnp.stack([jnp.zeros_like(j), j, jnp.full_like(j, NA_G)])[:, :, None]
    kr = jnp.arange(NA_WIN)[None, None, :]
    valid_r = (kr >= kr_lo) & (kr < kr_lo + NA_ROWS)
    drow = jnp.clip(kr - off - j[None, :, None] + (NA_ROWS - 1), 0, 2 * NA_ROWS - 2)
    t = jnp.where(valid_r[None, :, :, :, None, None], tab[:, drow], NEG)
    t = t.transpose(1, 0, 2, 4, 3, 5)
    return t.reshape(3, NA_HEADS, NA_G * GRID_W, NA_WIN * GRID_W).astype(F32)


def _local_kernel(up_p, up_c, up_n, cg_p, cg_c, cg_n, xc_p, xc_c, xc_n, bg_ref,
                  pw_ref, ps_ref, cw_ref, ob_ref, od_ref, *, lat_tiles, tiles_lat_seq, s_len, c_len):
    i = pl.program_id(0)
    is_lat = i < lat_tiles
    j = jnp.where(is_lat, i % tiles_lat_seq, 0)
    n_seq = jnp.where(is_lat, s_len, c_len)
    first = j == 0
    last = (j + 1) * TL == n_seq

    def ext(p_ref, c_ref, n_ref):
        p = jnp.where(first, 0.0, p_ref[...].astype(F32))
        n = jnp.where(last, 0.0, n_ref[...].astype(F32))
        return jnp.concatenate([p, c_ref[...].astype(F32), n], axis=0)

    z = ext(cg_p, cg_c, cg_n) * ext(xc_p, xc_c, xc_n)
    cw = cw_ref[...]
    y = (cw[0:1] * z[HALO - 1:HALO - 1 + TL] + cw[1:2] * z[HALO:HALO + TL]
         + cw[2:3] * z[HALO + 1:HALO + 1 + TL])
    od_ref[...] = (bg_ref[...].astype(F32) * y).astype(BF16)

    u = ext(up_p, up_c, up_n)
    t = j * TL + lax.broadcasted_iota(jnp.int32, (TL, 1), 0)
    ps = ps_ref[...]
    for g, w in enumerate(POOL_WINDOWS):
        sl = slice(POOL_GROUP * g, POOL_GROUP * (g + 1))
        ug = u[:, sl]
        acc = ug[HALO - w // 2:HALO - w // 2 + TL]
        for d in range(-w // 2 + 1, w // 2):
            acc = acc + ug[HALO + d:HALO + d + TL]
        cnt = (jnp.minimum(t - w // 2 + w, n_seq) - jnp.maximum(t - w // 2, 0)).astype(F32)
        pooled = acc / cnt - ug[HALO:HALO + TL]
        mixed = _dot(pooled.astype(BF16), pw_ref[g])
        ob_ref[:, sl] = (mixed * ps[:, sl]).astype(BF16)


def _local(proj, pool_w, pool_scale, conv_w, nb, s_len, c_len):
    r = proj.shape[0]
    lat_tiles = nb * s_len // TL
    hpt = TL // HALO
    nhalo = r // HALO

    def cur(off):
        return pl.BlockSpec((TL, 512), lambda i: (i, off // 512))

    def prev(off):
        return pl.BlockSpec((HALO, 512), lambda i: (jnp.maximum(i * hpt - 1, 0), off // 512))

    def nxt(off):
        return pl.BlockSpec((HALO, 512), lambda i: (jnp.minimum((i + 1) * hpt, nhalo - 1), off // 512))

    specs = []
    for off in (OFF_UP, OFF_CG, OFF_XC):
        specs += [prev(off), cur(off), nxt(off)]
    specs += [cur(OFF_BG),
              pl.BlockSpec((4, POOL_GROUP, POOL_GROUP), lambda i: (0, 0, 0)),
              pl.BlockSpec((1, 512), lambda i: (0, 0)),
              pl.BlockSpec((CONV_K, 512), lambda i: (0, 0))]
    return pl.pallas_call(
        functools.partial(_local_kernel, lat_tiles=lat_tiles, tiles_lat_seq=s_len // TL,
                          s_len=s_len, c_len=c_len),
        grid=(r // TL,),
        in_specs=specs,
        out_specs=[pl.BlockSpec((TL, 512), lambda i: (i, 0)), pl.BlockSpec((TL, 512), lambda i: (i, 0))],
        out_shape=[jax.ShapeDtypeStruct((r, 512), BF16), jax.ShapeDtypeStruct((r, 512), BF16)],
        compiler_params=_cparams(("parallel",)),
        name="local_mix",
    )(*([proj] * 10), pool_w, pool_scale, conv_w)


def _merge_kernel(oa_ref, ob_ref, oc_ref, od_ref, gate_ref, wb_ref, wo_ref, x_ref, gm_ref, g_ref, o_ref):
    merged = None
    for k, o in enumerate((oa_ref, ob_ref, oc_ref, od_ref)):
        proj = _dot(o[...], wb_ref[k])
        gk = jax.nn.sigmoid(gate_ref[:, 1024 * k:1024 * (k + 1)].astype(F32))
        merged = gk * proj if merged is None else merged + gk * proj
    y = _dot(merged.astype(BF16), wo_ref[...])
    o_ref[...] = x_ref[...] + gm_ref[...] * _rms(y, g_ref[...])


def _merge(oa, ob, oc, od, proj, wb, wo, xall, gate_mod, g, n_rows, tiles_per_batch):
    d = xall.shape[1]
    tm = 256
    tpb = tiles_per_batch * (TM // tm)
    row = lambda w: pl.BlockSpec((tm, w), lambda i: (i, 0))
    return pl.pallas_call(
        _merge_kernel,
        grid=(n_rows // tm,),
        in_specs=[row(512), row(512), row(512), row(512), row(4096),
                  pl.BlockSpec((4, 512, d), lambda i: (0, 0, 0)),
                  pl.BlockSpec((d, d), lambda i: (0, 0)),
                  row(d),
                  pl.BlockSpec((None, 1, d), lambda i: (i // tpb, 0, 0)),
                  pl.BlockSpec((1, d), lambda i: (0, 0))],
        out_specs=row(d),
        out_shape=jax.ShapeDtypeStruct((n_rows, d), F32),
        compiler_params=_cparams(("parallel",)),
        name="merge",
    )(oa, ob, oc, od, proj, wb, wo, xall, gate_mod, g.reshape(1, d))


def _ffn_kernel(x_ref, g2_ref, sh_ref, sc_ref, gm_ref, g3_ref, w1_ref, w3_ref, w2_ref, o_ref, h_ref, acc_ref):
    f = pl.program_id(1)

    @pl.when(f == 0)
    def _():
        h = _rms(x_ref[...], g2_ref[...]) * (1.0 + sc_ref[...]) + sh_ref[...]
        h_ref[...] = h.astype(BF16)
        acc_ref[...] = jnp.zeros_like(acc_ref)

    h = h_ref[...]
    a = _dot(h, w1_ref[...])
    b = _dot(h, w3_ref[...])
    acc_ref[...] += _dot((a * jax.nn.sigmoid(a) * b).astype(BF16), w2_ref[...])

    @pl.when(f == pl.num_programs(1) - 1)
    def _():
        o_ref[...] = x_ref[...] + gm_ref[...] * _rms(acc_ref[...], g3_ref[...])


def _ffn(xall, g2, shift, scale, gate_mod, g3, w1, w3, w2, tiles_per_batch):
    r, d = xall.shape
    dff = w1.shape[1]
    tf = dff // 2
    mod = pl.BlockSpec((None, 1, d), lambda i, f: (i // tiles_per_batch, 0, 0))
    vec = pl.BlockSpec((1, d), lambda i, f: (0, 0))
    return pl.pallas_call(
        _ffn_kernel,
        grid=(r // TM, dff // tf),
        in_specs=[pl.BlockSpec((TM, d), lambda i, f: (i, 0)), vec, mod, mod, mod, vec,
                  pl.BlockSpec((d, tf), lambda i, f: (0, f)),
                  pl.BlockSpec((d, tf), lambda i, f: (0, f)),
                  pl.BlockSpec((tf, d), lambda i, f: (f, 0))],
        out_specs=pl.BlockSpec((TM, d), lambda i, f: (i, 0)),
        out_shape=jax.ShapeDtypeStruct((r, d), F32),
        scratch_shapes=[pltpu.VMEM((TM, d), BF16), pltpu.VMEM((TM, d), F32)],
        compiler_params=_cparams(("parallel", "arbitrary")),
        name="ffn_dense",
    )(xall, g2.reshape(1, d), shift, scale, gate_mod, g3.reshape(1, d), w1, w3, w2)


def _router_kernel(x_ref, g2_ref, sh_ref, sc_ref, rt_ref, t_ref, idx_ref, w_ref):
    t = _rms(x_ref[...], g2_ref[...]) * (1.0 + sc_ref[...]) + sh_ref[...]
    t_ref[...] = t
    logits = lax.dot_general(rt_ref[...], t, (((1,), (1,)), ((), ())), preferred_element_type=F32,
                             precision=lax.Precision.HIGHEST)
    e = lax.broadcasted_iota(jnp.int32, logits.shape, 0).astype(F32)
    m1 = jnp.max(logits, axis=0, keepdims=True)
    i1 = jnp.min(jnp.where(logits == m1, e, float(N_EXPERTS)), axis=0, keepdims=True)
    rest = jnp.where(e == i1, -jnp.inf, logits)
    m2 = jnp.max(rest, axis=0, keepdims=True)
    i2 = jnp.min(jnp.where(rest == m2, e, float(N_EXPERTS)), axis=0, keepdims=True)
    ex = jnp.exp(m2 - m1)
    w1 = 1.0 / (1.0 + ex)
    idx_ref[0:1, :] = i1.astype(jnp.int32)
    idx_ref[1:2, :] = i2.astype(jnp.int32)
    w_ref[0:1, :] = w1
    w_ref[1:2, :] = ex * w1


def _router(x, g2, shift, scale, router_t, tiles_per_batch):
    t_rows, d = x.shape
    mod = pl.BlockSpec((None, 1, d), lambda i: (i // tiles_per_batch, 0, 0))
    return pl.pallas_call(
        _router_kernel,
        grid=(t_rows // TM,),
        in_specs=[pl.BlockSpec((TM, d), lambda i: (i, 0)),
                  pl.BlockSpec((1, d), lambda i: (0, 0)), mod, mod,
                  pl.BlockSpec((N_EXPERTS, d), lambda i: (0, 0))],
        out_specs=[pl.BlockSpec((TM, d), lambda i: (i, 0)),
                   pl.BlockSpec((TOP_K, TM), lambda i: (0, i)),
                   pl.BlockSpec((TOP_K, TM), lambda i: (0, i))],
        out_shape=[jax.ShapeDtypeStruct((t_rows, d), F32),
                   jax.ShapeDtypeStruct((TOP_K, t_rows), jnp.int32),
                   jax.ShapeDtypeStruct((TOP_K, t_rows), F32)],
        compiler_params=_cparams(("parallel",)),
        name="router",
    )(x, g2.reshape(1, d), shift, scale, router_t)


def _row_gather_start(ids_ref, src_ref, buf_ref, sem, n):
    def issue(r, c):
        pltpu.make_async_copy(src_ref.at[pl.ds(ids_ref[0, r], 1)], buf_ref.at[pl.ds(r, 1)], sem).start()
        return c

    lax.fori_loop(0, n, issue, 0, unroll=8)


def _row_gather_wait(src_ref, buf_ref, sem, n):
    pltpu.make_async_copy(src_ref.at[pl.ds(0, n)], buf_ref, sem).wait()


def _moe_ffn_kernel(eid_ref, nused_ref, ids_ref, ids_next_ref, t_ref, w1_ref, w3_ref, w2_ref, o_ref,
                    xbuf_ref, xs_ref, acc_ref, sems):
    i = pl.program_id(0)
    f = pl.program_id(1)
    n_f = pl.num_programs(1)
    slot = lax.rem(i, 2)

    def row_copy(ids, r, s):
        return pltpu.make_async_copy(t_ref.at[pl.ds(ids[0, r], 1)], xbuf_ref.at[s, pl.ds(r, 1)], sems.at[s])

    def all_rows(s):
        return pltpu.make_async_copy(t_ref.at[pl.ds(0, TM)], xbuf_ref.at[s], sems.at[s])

    @pl.when((i == 0) & (f == 0))
    def _():
        def issue(r, c):
            row_copy(ids_ref, r, 0).start()
            return c

        lax.fori_loop(0, TM, issue, 0)

    @pl.when(i < nused_ref[0])
    def _():
        @pl.when(f == 0)
        def _():
            all_rows(slot).wait()
            xs_ref[...] = xbuf_ref[slot].astype(BF16)
            acc_ref[...] = jnp.zeros_like(acc_ref)

        for rr in range(TM // MOE_F_STEPS):
            row_copy(ids_next_ref, f * (TM // MOE_F_STEPS) + rr, 1 - slot).start()

        x = xs_ref[...]
        a = _dot(x, w1_ref[...])
        b = _dot(x, w3_ref[...])
        acc_ref[...] += _dot((a * jax.nn.sigmoid(a) * b).astype(BF16), w2_ref[...])

        @pl.when(f == n_f - 1)
        def _():
            o_ref[...] = acc_ref[...]

    @pl.when((i + 1 == nused_ref[0]) & (f == n_f - 1))
    def _():
        all_rows(1 - slot).wait()

    @pl.when((i >= nused_ref[0]) & (f == n_f - 1))
    def _():
        o_ref[...] = jnp.zeros_like(o_ref)


def _moe_ffn(t, slot_tok, tile_eid, n_used, w1, w3, w2):
    n_tiles = slot_tok.shape[0]
    d = t.shape[1]
    dff = w1.shape[2]
    tf = dff // MOE_F_STEPS
    wf = lambda i, f, nu: jnp.where(i < nu[0], f, MOE_F_STEPS - 1)
    ids = lambda nxt: pl.BlockSpec((None, 1, TM),
                                   lambda i, f, eid, nu: (jnp.minimum(i + nxt, n_tiles - 1), 0, 0),
                                   memory_space=pltpu.SMEM)
    grid_spec = pltpu.PrefetchScalarGridSpec(
        num_scalar_prefetch=2,
        grid=(n_tiles, MOE_F_STEPS),
        in_specs=[ids(0), ids(1),
                  pl.BlockSpec(memory_space=pl.ANY),
                  pl.BlockSpec((None, d, tf), lambda i, f, eid, nu: (eid[i], 0, wf(i, f, nu))),
                  pl.BlockSpec((None, d, tf), lambda i, f, eid, nu: (eid[i], 0, wf(i, f, nu))),
                  pl.BlockSpec((None, tf, d), lambda i, f, eid, nu: (eid[i], wf(i, f, nu), 0))],
        out_specs=pl.BlockSpec((TM, d), lambda i, f, eid, nu: (i, 0)),
        scratch_shapes=[pltpu.VMEM((2, TM, d), F32), pltpu.VMEM((TM, d), BF16), pltpu.VMEM((TM, d), F32),
                        pltpu.SemaphoreType.DMA((2,))])
    return pl.pallas_call(
        _moe_ffn_kernel,
        grid_spec=grid_spec,
        out_shape=jax.ShapeDtypeStruct((n_tiles * TM, d), F32),
        compiler_params=_cparams(("arbitrary", "arbitrary")),
        name="moe_ffn",
    )(tile_eid, n_used, slot_tok, slot_tok, t, w1, w3, w2)


def _combine_kernel(p0_ref, p1_ref, ys_ref, w_ref, x_ref, gm_ref, g3_ref, o_ref, b0_ref, b1_ref, sem0, sem1):
    _row_gather_start(p0_ref, ys_ref, b0_ref, sem0, TM)
    _row_gather_start(p1_ref, ys_ref, b1_ref, sem1, TM)
    _row_gather_wait(ys_ref, b0_ref, sem0, TM)
    _row_gather_wait(ys_ref, b1_ref, sem1, TM)
    w = w_ref[...]
    y = w[:, 0:1] * b0_ref[...] + w[:, 1:2] * b1_ref[...]
    o_ref[...] = x_ref[...] + gm_ref[...] * _rms(y, g3_ref[...])


def _combine(pos0, pos1, ys, wcol, x, gate_mod, g3, tiles_per_batch):
    t_rows, d = x.shape
    ids = pl.BlockSpec((None, 1, TM), lambda i: (i, 0, 0), memory_space=pltpu.SMEM)
    return pl.pallas_call(
        _combine_kernel,
        grid=(t_rows // TM,),
        in_specs=[ids, ids,
                  pl.BlockSpec(memory_space=pl.ANY),
                  pl.BlockSpec((TM, TOP_K), lambda i: (i, 0)),
                  pl.BlockSpec((TM, d), lambda i: (i, 0)),
                  pl.BlockSpec((None, 1, d), lambda i: (i // tiles_per_batch, 0, 0)),
                  pl.BlockSpec((1, d), lambda i: (0, 0))],
        out_specs=pl.BlockSpec((TM, d), lambda i: (i, 0)),
        out_shape=jax.ShapeDtypeStruct((t_rows, d), F32),
        scratch_shapes=[pltpu.VMEM((TM, d), F32), pltpu.VMEM((TM, d), F32),
                        pltpu.SemaphoreType.DMA(()), pltpu.SemaphoreType.DMA(())],
        compiler_params=_cparams(("arbitrary",)),
        name="moe_combine",
    )(pos0, pos1, ys, wcol, x, gate_mod, g3.reshape(1, d))


def _routing_tables(top_i, t_rows):
    n_assign = TOP_K * t_rows
    n_tiles = n_assign // TM + N_EXPERTS
    e_flat = top_i.reshape(n_assign)
    onehot = (e_flat[:, None] == jnp.arange(N_EXPERTS)[None, :]).astype(jnp.int32)
    csum = jnp.cumsum(onehot, axis=0)
    counts = csum[-1]
    rank = jnp.sum((csum - onehot) * onehot, axis=1)
    tiles_e = (counts + TM - 1) // TM
    tile_end = jnp.cumsum(tiles_e)
    tile_start = tile_end - tiles_e
    slot = (tile_start * TM)[e_flat] + rank
    tok = jnp.tile(jnp.arange(t_rows, dtype=jnp.int32), TOP_K)
    slot_tok = jnp.zeros((n_tiles * TM,), jnp.int32).at[slot].set(tok)
    n_used = tile_end[-1]
    tile_ids = jnp.arange(n_tiles)
    tile_eid = jnp.sum(tile_ids[:, None] >= tile_end[None, :], axis=1)
    last_eid = jnp.sum(n_used - 1 >= tile_end)
    tile_eid = jnp.where(tile_ids < n_used, tile_eid, last_eid).astype(jnp.int32)
    pos = slot.reshape(TOP_K, t_rows).astype(jnp.int32)
    return (slot_tok.reshape(n_tiles, 1, TM), tile_eid, n_used.reshape(1).astype(jnp.int32),
            pos[0].reshape(t_rows // TM, 1, TM), pos[1].reshape(t_rows // TM, 1, TM))


def _prep_w_in(w):
    d = w.shape[0]
    q_a, k_a, v_a, u_p, c_q, c_kv, k_r, b_g, c_g, x_c, gate = jnp.split(w, IN_SPLITS, axis=1)
    z = lambda n: jnp.zeros((d, n), w.dtype)
    kr_e, kr_o = k_r[:, 0::2], k_r[:, 1::2]
    cols = [gate, q_a * NA_SCALE, k_a, v_a, u_p, b_g, c_g, x_c, c_q, z(128), c_kv,
            z(64), kr_e, kr_o, -kr_o, kr_e, z(128)]
    return jnp.concatenate(cols, axis=1).astype(BF16)


def _prep_w_uq(w_uq):
    wq = w_uq.reshape(MLA_Q_LORA, MLA_HEADS, MLA_NOPE + MLA_ROPE)
    nope, r = wq[..., :MLA_NOPE], wq[..., MLA_NOPE:]
    re, ro = r[..., 0::2], r[..., 1::2]
    ext = jnp.concatenate([nope, re, ro, -ro, re], axis=-1).reshape(MLA_Q_LORA, MLA_HEADS * 128)
    return jnp.pad(ext, ((0, 512 - MLA_Q_LORA), (0, 0))).T.astype(BF16)


def _prep_w_ukv(w_ukv):
    wkv = w_ukv.reshape(MLA_KV_LORA, MLA_HEADS, MLA_NOPE + MLA_V)
    zeros = jnp.zeros((MLA_KV_LORA, MLA_HEADS, 64), w_ukv.dtype)
    kn = jnp.concatenate([wkv[..., :MLA_NOPE], zeros], axis=-1).reshape(MLA_KV_LORA, MLA_HEADS * 128)
    v = wkv[..., MLA_NOPE:]
    odd = (jnp.arange(MLA_HEADS) % 2 == 1)[None, :, None]
    vt = jnp.where(odd, jnp.concatenate([zeros, v], axis=-1), jnp.concatenate([v, zeros], axis=-1))
    return kn.astype(BF16), vt.reshape(MLA_KV_LORA, MLA_HEADS * 128).T.astype(BF16)


def _rope_tables(nb, s_len, n_ctx_rows):
    pos = jnp.arange(s_len)
    row = (pos // GRID_W).astype(F32)
    col = (pos % GRID_W).astype(F32)
    n_pairs = MLA_ROPE // 4
    inv_freq = ROPE_BASE ** (-jnp.arange(n_pairs, dtype=F32) / n_pairs)
    ang = jnp.concatenate([row[:, None] * inv_freq, col[:, None] * inv_freq], axis=-1)
    cos = jnp.concatenate([jnp.tile(jnp.cos(ang), (nb, 1)), jnp.ones((n_ctx_rows, 16), F32)], axis=0)
    sin = jnp.concatenate([jnp.tile(jnp.sin(ang), (nb, 1)), jnp.zeros((n_ctx_rows, 16), F32)], axis=0)
    r = cos.shape[0]
    cos2 = jnp.concatenate([cos, cos], axis=1)
    sin2 = jnp.concatenate([sin, sin], axis=1)
    pad = lambda t, lead: jnp.concatenate([lead, t, jnp.zeros((r, 32), F32)], axis=1)
    return (pad(cos2, jnp.ones((r, 64), F32)), pad(sin2, jnp.zeros((r, 64), F32)), cos2.T, sin2.T)


def kernel(x, c, ctx, c_ctx, w_ada, b_ada, g_norm, w_in, na_rpb, pool_w, pool_scale, mla_g_q, mla_w_uq,
           mla_g_kv, mla_w_ukv, conv_w, w_branch, w_out, ffn_w1, ffn_w3, ffn_w2, moe_router, moe_w1,
           moe_w3, moe_w2):
    nb, s_len, d = x.shape
    c_len = ctx.shape[1]
    depth = w_in.shape[0]
    t_rows = nb * s_len
    n_ctx = nb * c_len
    assert s_len % TM == 0 and n_ctx == TM and s_len % GRID_W == 0 and t_rows % c_len == 0
    assert c_len == KC and c_len == TL
    tpb = s_len // TM
    rows = s_len // GRID_W

    xall = jnp.concatenate([x.reshape(t_rows, d), ctx.reshape(n_ctx, d)], axis=0)
    c8 = jnp.zeros((8, d), F32).at[:nb].set(c).at[nb].set(c_ctx)
    cos_t, sin_t, cos_tt, sin_tt = _rope_tables(nb, s_len, n_ctx)

    for l in range(depth):
        last = l == depth - 1
        mod = _adaln(c8, w_ada[l], b_ada[l]).reshape(8, 6, 1, d)
        mods = [mod[:, k] for k in range(6)]

        proj = _inproj(xall, g_norm[l, 0], mods[0], mods[1], _prep_w_in(w_in[l]), tpb)

        wkn, wvt = _prep_w_ukv(mla_w_ukv[l])
        gq = jnp.pad(mla_g_q[l], (0, 512 - MLA_Q_LORA)).reshape(1, 512)
        qt_m, k_m, vt_m = _mla_prep(proj, gq, mla_g_kv[l].reshape(1, MLA_KV_LORA), _prep_w_uq(mla_w_uq[l]),
                                    wkn, wvt, cos_t, sin_t, cos_tt, sin_tt)
        o_c = _mla_attn(qt_m, k_m, vt_m, nb, s_len, c_len)
        o_a = _na_attn(proj, _na_bias_table(na_rpb[l]), nb, s_len, c_len)
        o_b, o_d = _local(proj, pool_w[l].astype(BF16), pool_scale[l].reshape(1, 512), conv_w[l],
                          nb, s_len, c_len)

        wb = w_branch[l].astype(BF16)
        wo = w_out[l].astype(BF16)
        if not last:
            ctx_blk = t_rows // c_len
            o_cc = _mla_ctx_attn(qt_m, k_m, vt_m, nb, t_rows, c_len)
            o_ac = _ctx_attn(proj, proj, proj, nb, c_len, ctx_blk, 64, OFF_QA, OFF_KA, OFF_VA, "na_ctx_attn")
            o_a = jnp.concatenate([o_a, o_ac], axis=0)
            o_c = jnp.concatenate([o_c, o_cc], axis=0)
            n_rows = t_rows + n_ctx
        else:
            n_rows = t_rows
        xall = _merge(o_a, o_b, o_c, o_d, proj, wb, wo, xall, mods[2], g_norm[l, 1], n_rows, tpb)

        if l % 2 == 0:
            j = l // 2
            xall = _ffn(xall, g_norm[l, 2], mods[3], mods[4], mods[5], g_norm[l, 3],
                        ffn_w1[j].astype(BF16), ffn_w3[j].astype(BF16), ffn_w2[j].astype(BF16), tpb)
        else:
            j = l // 2
            assert last, "context rows are not routed through the experts"
            t_f32, top_i, top_w = _router(xall, g_norm[l, 2], mods[3], mods[4], moe_router[j].T, tpb)
            slot_tok, tile_eid, n_used, pos0, pos1 = _routing_tables(top_i, t_rows)
            ys = _moe_ffn(t_f32, slot_tok, tile_eid, n_used, moe_w1[j].astype(BF16), moe_w3[j].astype(BF16),
                          moe_w2[j].astype(BF16))
            xall = _combine(pos0, pos1, ys, top_w.T, xall, mods[5], g_norm[l, 3], tpb)

    return xall[:t_rows].reshape(nb, s_len, d)
```

```python
import functools

import numpy as np
import jax
import jax.numpy as jnp
from jax import lax
from jax.experimental import pallas as pl
from jax.experimental.pallas import tpu as pltpu

F32 = jnp.float32
BF16 = jnp.bfloat16

GRID_W = 64
N_BRANCH = 4
BRANCH_W = 512
EPS = 1e-6
NA_HEADS = 8
NA_HEAD_DIM = 64
NA_ROWS = 8
NA_COLS = 16
NA_G = 4
NA_WIN = NA_G + NA_ROWS
POOL_WINDOWS = (2, 4, 8, 16)
POOL_GROUP = 128
MLA_HEADS = 8
MLA_Q_LORA = 384
MLA_KV_LORA = 256
MLA_NOPE = 64
MLA_ROPE = 32
MLA_V = 64
ROPE_BASE = 10000.0
CONV_K = 3
N_EXPERTS = 8
TOP_K = 2

IN_SIZES = (512, 512, 512, 512, MLA_Q_LORA, MLA_KV_LORA, MLA_ROPE, 512, 512, 512, 4096)
IN_SPLITS = tuple(int(s) for s in np.cumsum(IN_SIZES)[:-1])

OFF_GATE = 0
OFF_QA = 4096
OFF_KA = 4608
OFF_VA = 5120
OFF_UP = 5632
OFF_BG = 6144
OFF_CG = 6656
OFF_XC = 7168
OFF_CQ = 7680
OFF_CKV = 8192
OFF_KR = 8448
N_IN = 8704

NA_SCALE = NA_HEAD_DIM ** -0.5
MLA_SCALE = (MLA_NOPE + MLA_ROPE) ** -0.5
LOG2E = 1.4426950408889634
NEG = -1e30

VMEM_LIMIT = 52 * 1024 * 1024
TM = 512
TL = 256
HALO = 16
KC = 256
MOE_F_STEPS = 2


def _cparams(sem):
    return pltpu.CompilerParams(dimension_semantics=sem, vmem_limit_bytes=VMEM_LIMIT)


def _rms(xf, g):
    ms = jnp.mean(xf * xf, axis=-1, keepdims=True)
    return xf * lax.rsqrt(ms + EPS) * g


def _dot(a, b):
    return jnp.dot(a, b, preferred_element_type=F32)


def _dot_nt(a, b):
    return lax.dot_general(a, b, (((1,), (1,)), ((), ())), preferred_element_type=F32)


def _adaln_kernel(c_ref, w_ref, b_ref, o_ref):
    c = c_ref[...]
    sc = c * jax.nn.sigmoid(c)
    o_ref[...] = jnp.dot(sc, w_ref[...], preferred_element_type=F32,
                         precision=lax.Precision.HIGHEST) + b_ref[...]


def _adaln(c8, w, b):
    d = c8.shape[1]
    n = w.shape[1]
    tn = 1536
    return pl.pallas_call(
        _adaln_kernel,
        grid=(n // tn,),
        in_specs=[pl.BlockSpec((8, d), lambda j: (0, 0)),
                  pl.BlockSpec((d, tn), lambda j: (0, j)),
                  pl.BlockSpec((1, tn), lambda j: (0, j))],
        out_specs=pl.BlockSpec((8, tn), lambda j: (0, j)),
        out_shape=jax.ShapeDtypeStruct((8, n), F32),
        compiler_params=_cparams(("arbitrary",)),
        name="adaln",
    )(c8, w, b.reshape(1, n))


def _inproj_kernel(x_ref, g_ref, sh_ref, sc_ref, w_ref, o_ref, h_ref):
    @pl.when(pl.program_id(1) == 0)
    def _():
        h = _rms(x_ref[...], g_ref[...]) * (1.0 + sc_ref[...]) + sh_ref[...]
        h_ref[...] = h.astype(BF16)

    o_ref[...] = _dot(h_ref[...], w_ref[...]).astype(BF16)


def _inproj(xall, g, shift, scale, w_p, tiles_per_batch):
    r, d = xall.shape
    n = w_p.shape[1]
    tn = n // 4
    mod_spec = pl.BlockSpec((None, 1, d), lambda i, j: (i // tiles_per_batch, 0, 0))
    return pl.pallas_call(
        _inproj_kernel,
        grid=(r // TM, n // tn),
        in_specs=[pl.BlockSpec((TM, d), lambda i, j: (i, 0)),
                  pl.BlockSpec((1, d), lambda i, j: (0, 0)),
                  mod_spec, mod_spec,
                  pl.BlockSpec((d, tn), lambda i, j: (0, j))],
        out_specs=pl.BlockSpec((TM, tn), lambda i, j: (i, j)),
        out_shape=jax.ShapeDtypeStruct((r, n), BF16),
        scratch_shapes=[pltpu.VMEM((TM, d), BF16)],
        compiler_params=_cparams(("parallel", "arbitrary")),
        name="inproj",
    )(xall, g.reshape(1, d), shift, scale, w_p)


def _mla_prep_kernel(cq_ref, ckv_ref, kr_ref, gq_ref, gkv_ref, wqt_ref, wkn_ref, wvt_ref,
                     cos_ref, sin_ref, cost_ref, sint_ref, qt_out, k_out, vt_out):
    cq = cq_ref[...].astype(F32)
    ms = jnp.sum(cq * cq, axis=-1, keepdims=True) * (1.0 / MLA_Q_LORA)
    cqn = (cq * lax.rsqrt(ms + EPS) * gq_ref[...]).astype(BF16)
    qt = _dot_nt(wqt_ref[...], cqn) * (MLA_SCALE * LOG2E)
    cost = cost_ref[...]
    sint = sint_ref[...]
    for h in range(MLA_HEADS):
        b = 128 * h
        qt_out[b:b + 64, :] = qt[b:b + 64].astype(BF16)
        qt_out[b + 64:b + 96, :] = (qt[b + 64:b + 96] * cost + qt[b + 96:b + 128] * sint).astype(BF16)
        qt_out[b + 96:b + 128, :] = jnp.zeros((32, TM), BF16)

    ckvn = _rms(ckv_ref[...].astype(F32), gkv_ref[...]).astype(BF16)
    kn = _dot(ckvn, wkn_ref[...])
    kseg = kr_ref[...].astype(F32)
    krt = kseg * cos_ref[...] + pltpu.roll(kseg, 96, 1) * sin_ref[...]
    for h in range(MLA_HEADS):
        sl = slice(128 * h, 128 * (h + 1))
        k_out[:, sl] = (kn[:, sl] + krt).astype(BF16)

    vt = _dot_nt(wvt_ref[...], ckvn)
    row = lax.broadcasted_iota(jnp.int32, vt.shape, 0)
    ones = ((row & 127) >= 64) != (((row >> 7) & 1) == 1)
    vt = jnp.where(ones, 1.0, vt).astype(BF16)
    for j in range(TM // KC):
        vt_out[j] = vt[:, KC * j:KC * (j + 1)]


def _mla_prep(proj, gq, gkv, wqt, wkn, wvt, cos_t, sin_t, cos_tt, sin_tt):
    r = proj.shape[0]
    full = lambda shape: pl.BlockSpec(shape, lambda i: (0, 0))
    return pl.pallas_call(
        _mla_prep_kernel,
        grid=(r // TM,),
        in_specs=[pl.BlockSpec((TM, 512), lambda i: (i, OFF_CQ // 512)),
                  pl.BlockSpec((TM, 256), lambda i: (i, OFF_CKV // 256)),
                  pl.BlockSpec((TM, 128), lambda i: (i, OFF_KR // 128)),
                  full((1, 512)), full((1, 256)),
                  full((1024, 512)), full((256, 1024)), full((1024, 256)),
                  pl.BlockSpec((TM, 128), lambda i: (i, 0)),
                  pl.BlockSpec((TM, 128), lambda i: (i, 0)),
                  pl.BlockSpec((32, TM), lambda i: (0, i)),
                  pl.BlockSpec((32, TM), lambda i: (0, i))],
        out_specs=[pl.BlockSpec((1024, TM), lambda i: (0, i)),
                   pl.BlockSpec((TM, 1024), lambda i: (i, 0)),
                   pl.BlockSpec((TM // KC, 1024, KC), lambda i: (i, 0, 0))],
        out_shape=[jax.ShapeDtypeStruct((1024, r), BF16),
                   jax.ShapeDtypeStruct((r, 1024), BF16),
                   jax.ShapeDtypeStruct((r // KC, 1024, KC), BF16)],
        compiler_params=_cparams(("parallel",)),
        name="mla_prep",
    )(proj, proj, proj, gq, gkv, wqt, wkn, wvt, cos_t, sin_t, cos_tt, sin_tt)


def _pick_heads(o0, o1):
    lane = lax.broadcasted_iota(jnp.int32, o0.shape, 1)
    return jnp.where(lane < 64, o0, o1)


def _mla_update(s, vt, m_old, acc_ref):
    m_new = jnp.maximum(m_old, jnp.max(s, axis=0, keepdims=True))
    alpha = jnp.exp2(m_old - m_new)
    p = jnp.exp2(s - m_new).astype(BF16)
    acc_ref[...] = alpha * acc_ref[...] + _dot(vt, p)
    return m_new


def _mla_finish(a0, a1):
    return _pick_heads(a0 / a0[:, 64:65], a1 / a1[:, 0:1]).astype(BF16)


def _mla_attn_kernel(qt_ref, k_ref, vt_ref, kc_ref, vtc_ref, o_ref, *acc_refs, cpi):
    nt = qt_ref.shape[1] // 256
    units = [(t, h) for h in range(2) for t in range(nt)]
    qts = [qt_ref[128 * h:128 * (h + 1), 256 * t:256 * (t + 1)] for t, h in units]
    n_chunks = k_ref.shape[0] // KC
    for acc_ref in acc_refs:
        acc_ref[...] = jnp.zeros_like(acc_ref)

    def scores(kget):
        return tuple(_dot(kget(h), qts[u]) for u, (t, h) in enumerate(units))

    def latent_keys(j):
        off = pl.multiple_of(j * KC, KC)
        return lambda h: k_ref[pl.ds(off, KC), 128 * h:128 * (h + 1)]

    def step(s_cur, vget, ms, kget_next):
        s_next, ms_new = [], []
        for u, (t, h) in enumerate(units):
            if kget_next is not None:
                s_next.append(_dot(kget_next(h), qts[u]))
            ms_new.append(_mla_update(s_cur[u], vget(h), ms[u], acc_refs[u]))
        return tuple(s_next), tuple(ms_new)

    ms = tuple(jnp.full((1, 256), -jnp.inf, F32) for _ in units)
    s_ctx = scores(lambda h: kc_ref[:, 128 * h:128 * (h + 1)])
    s_cur, ms = step(s_ctx, lambda h: vtc_ref[0, 128 * h:128 * (h + 1), :], ms, latent_keys(0))

    def values(j):
        return lambda h: vt_ref[j, 128 * h:128 * (h + 1), :]

    def body(i, carry):
        ms, s_cur = carry
        for c in range(cpi):
            j = i * cpi + c
            s_cur, ms = step(s_cur, values(j), ms, latent_keys(j + 1))
        return ms, s_cur

    n_iter = n_chunks // cpi - 1
    ms, s_cur = lax.fori_loop(0, n_iter, body, (ms, s_cur))
    for j in range(n_iter * cpi, n_chunks):
        s_cur, ms = step(s_cur, values(j), ms, latent_keys(j + 1) if j + 1 < n_chunks else None)
    for t in range(nt):
        o_ref[256 * t:256 * (t + 1), :] = _mla_finish(acc_refs[units.index((t, 0))][...].T,
                                                      acc_refs[units.index((t, 1))][...].T)


def _mla_attn(qt, k, vt, nb, s_len, c_len):
    tq = min(512, s_len)
    nq = s_len // tq
    t_rows = nb * s_len
    cpi = s_len // KC
    return pl.pallas_call(
        functools.partial(_mla_attn_kernel, cpi=cpi),
        grid=(nb, 4, nq),
        in_specs=[pl.BlockSpec((256, tq), lambda b, hp, i: (hp, b * nq + i)),
                  pl.BlockSpec((s_len, 256), lambda b, hp, i: (b, hp)),
                  pl.BlockSpec((s_len // KC, 256, KC), lambda b, hp, i: (b, hp, 0)),
                  pl.BlockSpec((c_len, 256), lambda b, hp, i: (t_rows // c_len + b, hp)),
                  pl.BlockSpec((1, 256, KC), lambda b, hp, i: (t_rows // KC + b, hp, 0))],
        out_specs=pl.BlockSpec((tq, 128), lambda b, hp, i: (b * nq + i, hp)),
        out_shape=jax.ShapeDtypeStruct((t_rows, 512), BF16),
        scratch_shapes=[pltpu.VMEM((128, 256), F32)] * (2 * (tq // 256)),
        compiler_params=_cparams(("parallel", "parallel", "arbitrary")),
        name="mla_attn",
    )(qt, k, vt, k, vt)


def _mla_ctx_attn_kernel(qt_ref, kc_ref, vtc_ref, o_ref):
    outs = []
    for h in range(2):
        s = _dot(kc_ref[:, 128 * h:128 * (h + 1)], qt_ref[128 * h:128 * (h + 1), :])
        p = jnp.exp2(s - jnp.max(s, axis=0, keepdims=True)).astype(BF16)
        outs.append(_dot(vtc_ref[0, 128 * h:128 * (h + 1), :], p).T)
    o_ref[...] = _mla_finish(outs[0], outs[1])


def _mla_ctx_attn(qt, k, vt, nb, t_rows, c_len):
    blk0 = t_rows // c_len
    return pl.pallas_call(
        _mla_ctx_attn_kernel,
        grid=(nb, 4),
        in_specs=[pl.BlockSpec((256, c_len), lambda b, hp: (hp, blk0 + b)),
                  pl.BlockSpec((c_len, 256), lambda b, hp: (blk0 + b, hp)),
                  pl.BlockSpec((1, 256, KC), lambda b, hp: (blk0 + b, hp, 0))],
        out_specs=pl.BlockSpec((c_len, 128), lambda b, hp: (b, hp)),
        out_shape=jax.ShapeDtypeStruct((nb * c_len, 512), BF16),
        compiler_params=_cparams(("parallel", "parallel")),
        name="mla_ctx_attn",
    )(qt, k, vt)


def _ctx_attn_kernel(q_ref, k_ref, v_ref, o_ref, *, dqk):
    v = v_ref[...]
    outs = []
    for h in range(2):
        sl = slice(dqk * h, dqk * (h + 1))
        s = _dot_nt(q_ref[:, sl], k_ref[:, sl])
        m = jnp.max(s, axis=-1, keepdims=True)
        p = jnp.exp(s - m)
        l = jnp.sum(p, axis=-1, keepdims=True)
        outs.append(_dot(p.astype(BF16), v) / l)
    o_ref[...] = _pick_heads(outs[0], outs[1]).astype(BF16)


def _ctx_attn(q, k, v, nb, c_len, row_blk0, dqk, qcol, kcol, vcol, name):
    w = 2 * dqk
    return pl.pallas_call(
        functools.partial(_ctx_attn_kernel, dqk=dqk),
        grid=(nb, 4),
        in_specs=[pl.BlockSpec((c_len, w), lambda b, hp: (row_blk0 + b, qcol // w + hp)),
                  pl.BlockSpec((c_len, w), lambda b, hp: (row_blk0 + b, kcol // w + hp)),
                  pl.BlockSpec((c_len, 128), lambda b, hp: (row_blk0 + b, vcol // 128 + hp))],
        out_specs=pl.BlockSpec((c_len, 128), lambda b, hp: (b, hp)),
        out_shape=jax.ShapeDtypeStruct((nb * c_len, 512), BF16),
        compiler_params=_cparams(("parallel", "parallel")),
        name=name,
    )(q, k, v)


def _na_attn_kernel(q_ref, k_ref, v_ref, kc_ref, vc_ref, bias_ref, o_ref, *, rows):
    nq = NA_G * GRID_W
    nk = NA_WIN * GRID_W
    vc = vc_ref[...]

    def body(g, carry):
        r0 = g * NA_G
        ws = jnp.clip(r0 - NA_ROWS // 2, 0, rows - NA_WIN)
        pat = lax.shift_right_logical(r0 - ws, 2)
        qoff = pl.multiple_of(r0 * GRID_W, nq)
        koff = pl.multiple_of(ws * GRID_W, GRID_W)
        q = q_ref[pl.ds(qoff, nq), :]
        kw = k_ref[pl.ds(koff, nk), :]
        vw = v_ref[pl.ds(koff, nk), :]
        scores = []
        for h in range(2):
            sl = slice(64 * h, 64 * (h + 1))
            scores.append((_dot_nt(q[:, sl], kw[:, sl]) + bias_ref[pat, h], _dot_nt(q[:, sl], kc_ref[:, sl])))
        outs = []
        for s, sc in scores:
            m = jnp.maximum(jnp.max(s, axis=-1, keepdims=True), jnp.max(sc, axis=-1, keepdims=True))
            p = jnp.exp(s - m)
            pc = jnp.exp(sc - m)
            l = jnp.sum(p, axis=-1, keepdims=True) + jnp.sum(pc, axis=-1, keepdims=True)
            outs.append((_dot(p.astype(BF16), vw) + _dot(pc.astype(BF16), vc)) / l)
        o_ref[pl.ds(qoff, nq), :] = _pick_heads(outs[0], outs[1]).astype(BF16)
        return carry

    lax.fori_loop(0, rows // NA_G, body, 0)


def _na_attn(proj, bias, nb, s_len, c_len):
    rows = s_len // GRID_W
    assert NA_G == 4 and rows % NA_G == 0 and rows >= NA_WIN
    ctx_blk = nb * s_len // c_len
    return pl.pallas_call(
        functools.partial(_na_attn_kernel, rows=rows),
        grid=(nb, 4),
        in_specs=[pl.BlockSpec((s_len, 128), lambda b, hp: (b, OFF_QA // 128 + hp)),
                  pl.BlockSpec((s_len, 128), lambda b, hp: (b, OFF_KA // 128 + hp)),
                  pl.BlockSpec((s_len, 128), lambda b, hp: (b, OFF_VA // 128 + hp)),
                  pl.BlockSpec((c_len, 128), lambda b, hp: (ctx_blk + b, OFF_KA // 128 + hp)),
                  pl.BlockSpec((c_len, 128), lambda b, hp: (ctx_blk + b, OFF_VA // 128 + hp)),
                  pl.BlockSpec((3, 2, NA_G * GRID_W, NA_WIN * GRID_W), lambda b, hp: (0, hp, 0, 0))],
        out_specs=pl.BlockSpec((s_len, 128), lambda b, hp: (b, hp)),
        out_shape=jax.ShapeDtypeStruct((nb * s_len, 512), BF16),
        compiler_params=_cparams(("parallel", "parallel")),
        name="na_attn",
    )(proj, proj, proj, proj, proj, bias)


def _na_bias_table(rpb):
    col = jnp.arange(GRID_W)
    cstart = jnp.clip(col - NA_COLS // 2, 0, GRID_W - NA_COLS)
    kc = jnp.arange(GRID_W)
    valid_c = (kc[None, :] >= cstart[:, None]) & (kc[None, :] < cstart[:, None] + NA_COLS)
    dcol = jnp.clip(kc[None, :] - col[:, None] + (NA_COLS - 1), 0, 2 * NA_COLS - 2)
    tab = jnp.where(valid_c[None, None], rpb[:, :, dcol], NEG)
    j = jnp.arange(NA_G)
    off = (NA_G * jnp.arange(3))[:, None, None]
    kr_lo = jnp.stack([jnp.zeros_like(j), j, jnp.full_like(j, NA_G)])[:, :, None]
    kr = jnp.arange(NA_WIN)[None, None, :]
    valid_r = (kr >= kr_lo) & (kr < kr_lo + NA_ROWS)
    drow = jnp.clip(kr - off - j[None, :, None] + (NA_ROWS - 1), 0, 2 * NA_ROWS - 2)
    t = jnp.where(valid_r[None, :, :, :, None, None], tab[:, drow], NEG)
    t = t.transpose(1, 0, 2, 4, 3, 5)
    return t.reshape(3, NA_HEADS, NA_G * GRID_W, NA_WIN * GRID_W).astype(F32)


def _local_kernel(up_p, up_c, up_n, cg_p, cg_c, cg_n, xc_p, xc_c, xc_n, bg_ref,
                  pw_ref, ps_ref, cw_ref, ob_ref, od_ref, *, lat_tiles, tiles_lat_seq, s_len, c_len):
    i = pl.program_id(0)
    is_lat = i < lat_tiles
    j = jnp.where(is_lat, i % tiles_lat_seq, 0)
    n_seq = jnp.where(is_lat, s_len, c_len)
    first = j == 0
    last = (j + 1) * TL == n_seq

    def ext(p_ref, c_ref, n_ref):
        p = jnp.where(first, 0.0, p_ref[...].astype(F32))
        n = jnp.where(last, 0.0, n_ref[...].astype(F32))
        return jnp.concatenate([p, c_ref[...].astype(F32), n], axis=0)

    z = ext(cg_p, cg_c, cg_n) * ext(xc_p, xc_c, xc_n)
    cw = cw_ref[...]
    y = (cw[0:1] * z[HALO - 1:HALO - 1 + TL] + cw[1:2] * z[HALO:HALO + TL]
         + cw[2:3] * z[HALO + 1:HALO + 1 + TL])
    od_ref[...] = (bg_ref[...].astype(F32) * y).astype(BF16)

    u = ext(up_p, up_c, up_n)
    t = j * TL + lax.broadcasted_iota(jnp.int32, (TL, 1), 0)
    ps = ps_ref[...]
    for g, w in enumerate(POOL_WINDOWS):
        sl = slice(POOL_GROUP * g, POOL_GROUP * (g + 1))
        ug = u[:, sl]
        acc = ug[HALO - w // 2:HALO - w // 2 + TL]
        for d in range(-w // 2 + 1, w // 2):
            acc = acc + ug[HALO + d:HALO + d + TL]
        cnt = (jnp.minimum(t - w // 2 + w, n_seq) - jnp.maximum(t - w // 2, 0)).astype(F32)
        pooled = acc / cnt - ug[HALO:HALO + TL]
        mixed = _dot(pooled.astype(BF16), pw_ref[g])
        ob_ref[:, sl] = (mixed * ps[:, sl]).astype(BF16)


def _local(proj, pool_w, pool_scale, conv_w, nb, s_len, c_len):
    r = proj.shape[0]
    lat_tiles = nb * s_len // TL
    hpt = TL // HALO
    nhalo = r // HALO

    def cur(off):
        return pl.BlockSpec((TL, 512), lambda i: (i, off // 512))

    def prev(off):
        return pl.BlockSpec((HALO, 512), lambda i: (jnp.maximum(i * hpt - 1, 0), off // 512))

    def nxt(off):
        return pl.BlockSpec((HALO, 512), lambda i: (jnp.minimum((i + 1) * hpt, nhalo - 1), off // 512))

    specs = []
    for off in (OFF_UP, OFF_CG, OFF_XC):
        specs += [prev(off), cur(off), nxt(off)]
    specs += [cur(OFF_BG),
              pl.BlockSpec((4, POOL_GROUP, POOL_GROUP), lambda i: (0, 0, 0)),
              pl.BlockSpec((1, 512), lambda i: (0, 0)),
              pl.BlockSpec((CONV_K, 512), lambda i: (0, 0))]
    return pl.pallas_call(
        functools.partial(_local_kernel, lat_tiles=lat_tiles, tiles_lat_seq=s_len // TL,
                          s_len=s_len, c_len=c_len),
        grid=(r // TL,),
        in_specs=specs,
        out_specs=[pl.BlockSpec((TL, 512), lambda i: (i, 0)), pl.BlockSpec((TL, 512), lambda i: (i, 0))],
        out_shape=[jax.ShapeDtypeStruct((r, 512), BF16), jax.ShapeDtypeStruct((r, 512), BF16)],
        compiler_params=_cparams(("parallel",)),
        name="local_mix",
    )(*([proj] * 10), pool_w, pool_scale, conv_w)


def _merge_kernel(oa_ref, ob_ref, oc_ref, od_ref, gate_ref, wb_ref, wo_ref, x_ref, gm_ref, g_ref, o_ref):
    merged = None
    for k, o in enumerate((oa_ref, ob_ref, oc_ref, od_ref)):
        proj = _dot(o[...], wb_ref[k])
        gk = jax.nn.sigmoid(gate_ref[:, 1024 * k:1024 * (k + 1)].astype(F32))
        merged = gk * proj if merged is None else merged + gk * proj
    y = _dot(merged.astype(BF16), wo_ref[...])
    o_ref[...] = x_ref[...] + gm_ref[...] * _rms(y, g_ref[...])


def _merge(oa, ob, oc, od, proj, wb, wo, xall, gate_mod, g, n_rows, tiles_per_batch):
    d = xall.shape[1]
    tm = 256
    tpb = tiles_per_batch * (TM // tm)
    row = lambda w: pl.BlockSpec((tm, w), lambda i: (i, 0))
    return pl.pallas_call(
        _merge_kernel,
        grid=(n_rows // tm,),
        in_specs=[row(512), row(512), row(512), row(512), row(4096),
                  pl.BlockSpec((4, 512, d), lambda i: (0, 0, 0)),
                  pl.BlockSpec((d, d), lambda i: (0, 0)),
                  row(d),
                  pl.BlockSpec((None, 1, d), lambda i: (i // tpb, 0, 0)),
                  pl.BlockSpec((1, d), lambda i: (0, 0))],
        out_specs=row(d),
        out_shape=jax.ShapeDtypeStruct((n_rows, d), F32),
        compiler_params=_cparams(("parallel",)),
        name="merge",
    )(oa, ob, oc, od, proj, wb, wo, xall, gate_mod, g.reshape(1, d))


def _ffn_kernel(x_ref, g2_ref, sh_ref, sc_ref, gm_ref, g3_ref, w1_ref, w3_ref, w2_ref, o_ref, h_ref, acc_ref):
    f = pl.program_id(1)

    @pl.when(f == 0)
    def _():
        h = _rms(x_ref[...], g2_ref[...]) * (1.0 + sc_ref[...]) + sh_ref[...]
        h_ref[...] = h.astype(BF16)
        acc_ref[...] = jnp.zeros_like(acc_ref)

    h = h_ref[...]
    a = _dot(h, w1_ref[...])
    b = _dot(h, w3_ref[...])
    acc_ref[...] += _dot((a * jax.nn.sigmoid(a) * b).astype(BF16), w2_ref[...])

    @pl.when(f == pl.num_programs(1) - 1)
    def _():
        o_ref[...] = x_ref[...] + gm_ref[...] * _rms(acc_ref[...], g3_ref[...])


def _ffn(xall, g2, shift, scale, gate_mod, g3, w1, w3, w2, tiles_per_batch):
    r, d = xall.shape
    dff = w1.shape[1]
    tf = dff // 2
    mod = pl.BlockSpec((None, 1, d), lambda i, f: (i // tiles_per_batch, 0, 0))
    vec = pl.BlockSpec((1, d), lambda i, f: (0, 0))
    return pl.pallas_call(
        _ffn_kernel,
        grid=(r // TM, dff // tf),
        in_specs=[pl.BlockSpec((TM, d), lambda i, f: (i, 0)), vec, mod, mod, mod, vec,
                  pl.BlockSpec((d, tf), lambda i, f: (0, f)),
                  pl.BlockSpec((d, tf), lambda i, f: (0, f)),
                  pl.BlockSpec((tf, d), lambda i, f: (f, 0))],
        out_specs=pl.BlockSpec((TM, d), lambda i, f: (i, 0)),
        out_shape=jax.ShapeDtypeStruct((r, d), F32),
        scratch_shapes=[pltpu.VMEM((TM, d), BF16), pltpu.VMEM((TM, d), F32)],
        compiler_params=_cparams(("parallel", "arbitrary")),
        name="ffn_dense",
    )(xall, g2.reshape(1, d), shift, scale, gate_mod, g3.reshape(1, d), w1, w3, w2)


def _router_kernel(x_ref, g2_ref, sh_ref, sc_ref, rt_ref, t_ref, idx_ref, w_ref):
    t = _rms(x_ref[...], g2_ref[...]) * (1.0 + sc_ref[...]) + sh_ref[...]
    t_ref[...] = t
    logits = lax.dot_general(rt_ref[...], t, (((1,), (1,)), ((), ())), preferred_element_type=F32,
                             precision=lax.Precision.HIGHEST)
    e = lax.broadcasted_iota(jnp.int32, logits.shape, 0).astype(F32)
    m1 = jnp.max(logits, axis=0, keepdims=True)
    i1 = jnp.min(jnp.where(logits == m1, e, float(N_EXPERTS)), axis=0, keepdims=True)
    rest = jnp.where(e == i1, -jnp.inf, logits)
    m2 = jnp.max(rest, axis=0, keepdims=True)
    i2 = jnp.min(jnp.where(rest == m2, e, float(N_EXPERTS)), axis=0, keepdims=True)
    ex = jnp.exp(m2 - m1)
    w1 = 1.0 / (1.0 + ex)
    idx_ref[0:1, :] = i1.astype(jnp.int32)
    idx_ref[1:2, :] = i2.astype(jnp.int32)
    w_ref[0:1, :] = w1
    w_ref[1:2, :] = ex * w1


def _router(x, g2, shift, scale, router_t, tiles_per_batch):
    t_rows, d = x.shape
    mod = pl.BlockSpec((None, 1, d), lambda i: (i // tiles_per_batch, 0, 0))
    return pl.pallas_call(
        _router_kernel,
        grid=(t_rows // TM,),
        in_specs=[pl.BlockSpec((TM, d), lambda i: (i, 0)),
                  pl.BlockSpec((1, d), lambda i: (0, 0)), mod, mod,
                  pl.BlockSpec((N_EXPERTS, d), lambda i: (0, 0))],
        out_specs=[pl.BlockSpec((TM, d), lambda i: (i, 0)),
                   pl.BlockSpec((TOP_K, TM), lambda i: (0, i)),
                   pl.BlockSpec((TOP_K, TM), lambda i: (0, i))],
        out_shape=[jax.ShapeDtypeStruct((t_rows, d), F32),
                   jax.ShapeDtypeStruct((TOP_K, t_rows), jnp.int32),
                   jax.ShapeDtypeStruct((TOP_K, t_rows), F32)],
        compiler_params=_cparams(("parallel",)),
        name="router",
    )(x, g2.reshape(1, d), shift, scale, router_t)


def _row_gather_start(ids_ref, src_ref, buf_ref, sem, n):
    def issue(r, c):
        pltpu.make_async_copy(src_ref.at[pl.ds(ids_ref[0, r], 1)], buf_ref.at[pl.ds(r, 1)], sem).start()
        return c

    lax.fori_loop(0, n, issue, 0, unroll=8)


def _row_gather_wait(src_ref, buf_ref, sem, n):
    pltpu.make_async_copy(src_ref.at[pl.ds(0, n)], buf_ref, sem).wait()


def _moe_ffn_kernel(eid_ref, nused_ref, ids_ref, ids_next_ref, t_ref, w1_ref, w3_ref, w2_ref, o_ref,
                    xbuf_ref, xs_ref, acc_ref, sems):
    i = pl.program_id(0)
    f = pl.program_id(1)
    n_f = pl.num_programs(1)
    slot = lax.rem(i, 2)

    def row_copy(ids, r, s):
        return pltpu.make_async_copy(t_ref.at[pl.ds(ids[0, r], 1)], xbuf_ref.at[s, pl.ds(r, 1)], sems.at[s])

    def all_rows(s):
        return pltpu.make_async_copy(t_ref.at[pl.ds(0, TM)], xbuf_ref.at[s], sems.at[s])

    @pl.when((i == 0) & (f == 0))
    def _():
        def issue(r, c):
            row_copy(ids_ref, r, 0).start()
            return c

        lax.fori_loop(0, TM, issue, 0)

    @pl.when(i < nused_ref[0])
    def _():
        @pl.when(f == 0)
        def _():
            all_rows(slot).wait()
            xs_ref[...] = xbuf_ref[slot].astype(BF16)
            acc_ref[...] = jnp.zeros_like(acc_ref)

        for rr in range(TM // MOE_F_STEPS):
            row_copy(ids_next_ref, f * (TM // MOE_F_STEPS) + rr, 1 - slot).start()

        x = xs_ref[...]
        a = _dot(x, w1_ref[...])
        b = _dot(x, w3_ref[...])
        acc_ref[...] += _dot((a * jax.nn.sigmoid(a) * b).astype(BF16), w2_ref[...])

        @pl.when(f == n_f - 1)
        def _():
            o_ref[...] = acc_ref[...]

    @pl.when((i + 1 == nused_ref[0]) & (f == n_f - 1))
    def _():
        all_rows(1 - slot).wait()

    @pl.when((i >= nused_ref[0]) & (f == n_f - 1))
    def _():
        o_ref[...] = jnp.zeros_like(o_ref)


def _moe_ffn(t, slot_tok, tile_eid, n_used, w1, w3, w2):
    n_tiles = slot_tok.shape[0]
    d = t.shape[1]
    dff = w1.shape[2]
    tf = dff // MOE_F_STEPS
    wf = lambda i, f, nu: jnp.where(i < nu[0], f, MOE_F_STEPS - 1)
    ids = lambda nxt: pl.BlockSpec((None, 1, TM),
                                   lambda i, f, eid, nu: (jnp.minimum(i + nxt, n_tiles - 1), 0, 0),
                                   memory_space=pltpu.SMEM)
    grid_spec = pltpu.PrefetchScalarGridSpec(
        num_scalar_prefetch=2,
        grid=(n_tiles, MOE_F_STEPS),
        in_specs=[ids(0), ids(1),
                  pl.BlockSpec(memory_space=pl.ANY),
                  pl.BlockSpec((None, d, tf), lambda i, f, eid, nu: (eid[i], 0, wf(i, f, nu))),
                  pl.BlockSpec((None, d, tf), lambda i, f, eid, nu: (eid[i], 0, wf(i, f, nu))),
                  pl.BlockSpec((None, tf, d), lambda i, f, eid, nu: (eid[i], wf(i, f, nu), 0))],
        out_specs=pl.BlockSpec((TM, d), lambda i, f, eid, nu: (i, 0)),
        scratch_shapes=[pltpu.VMEM((2, TM, d), F32), pltpu.VMEM((TM, d), BF16), pltpu.VMEM((TM, d), F32),
                        pltpu.SemaphoreType.DMA((2,))])
    return pl.pallas_call(
        _moe_ffn_kernel,
        grid_spec=grid_spec,
        out_shape=jax.ShapeDtypeStruct((n_tiles * TM, d), F32),
        compiler_params=_cparams(("arbitrary", "arbitrary")),
        name="moe_ffn",
    )(tile_eid, n_used, slot_tok, slot_tok, t, w1, w3, w2)


def _combine_kernel(p0_ref, p1_ref, p0n_ref, p1n_ref, ys_ref, w_ref, x_ref, gm_ref, g3_ref, o_ref,
                    b0_ref, b1_ref, sems):
    i = pl.program_id(0)
    slot = lax.rem(i, 2)

    def start(s, ids0, ids1):
        _row_gather_start(ids0, ys_ref, b0_ref.at[s], sems.at[0, s], TM)
        _row_gather_start(ids1, ys_ref, b1_ref.at[s], sems.at[1, s], TM)

    def wait(s):
        _row_gather_wait(ys_ref, b0_ref.at[s], sems.at[0, s], TM)
        _row_gather_wait(ys_ref, b1_ref.at[s], sems.at[1, s], TM)

    @pl.when(i == 0)
    def _():
        start(0, p0_ref, p1_ref)

    start(1 - slot, p0n_ref, p1n_ref)
    wait(slot)
    w = w_ref[...]
    y = w[:, 0:1] * b0_ref[slot] + w[:, 1:2] * b1_ref[slot]
    o_ref[...] = x_ref[...] + gm_ref[...] * _rms(y, g3_ref[...])

    @pl.when(i == pl.num_programs(0) - 1)
    def _():
        wait(1 - slot)


def _combine(pos0, pos1, ys, wcol, x, gate_mod, g3, tiles_per_batch):
    t_rows, d = x.shape
    n_tiles = t_rows // TM
    ids = lambda nxt: pl.BlockSpec((None, 1, TM), lambda i: (jnp.minimum(i + nxt, n_tiles - 1), 0, 0),
                                   memory_space=pltpu.SMEM)
    return pl.pallas_call(
        _combine_kernel,
        grid=(n_tiles,),
        in_specs=[ids(0), ids(0), ids(1), ids(1),
                  pl.BlockSpec(memory_space=pl.ANY),
                  pl.BlockSpec((TM, TOP_K), lambda i: (i, 0)),
                  pl.BlockSpec((TM, d), lambda i: (i, 0)),
                  pl.BlockSpec((None, 1, d), lambda i: (i // tiles_per_batch, 0, 0)),
                  pl.BlockSpec((1, d), lambda i: (0, 0))],
        out_specs=pl.BlockSpec((TM, d), lambda i: (i, 0)),
        out_shape=jax.ShapeDtypeStruct((t_rows, d), F32),
        scratch_shapes=[pltpu.VMEM((2, TM, d), F32), pltpu.VMEM((2, TM, d), F32),
                        pltpu.SemaphoreType.DMA((2, 2))],
        compiler_params=_cparams(("arbitrary",)),
        name="moe_combine",
    )(pos0, pos1, pos0, pos1, ys, wcol, x, gate_mod, g3.reshape(1, d))


def _routing_tables(top_i, t_rows):
    n_assign = TOP_K * t_rows
    n_tiles = n_assign // TM + N_EXPERTS
    e_flat = top_i.reshape(n_assign)
    onehot = (e_flat[:, None] == jnp.arange(N_EXPERTS)[None, :]).astype(jnp.int32)
    csum = jnp.cumsum(onehot, axis=0)
    counts = csum[-1]
    rank = jnp.sum((csum - onehot) * onehot, axis=1)
    tiles_e = (counts + TM - 1) // TM
    tile_end = jnp.cumsum(tiles_e)
    tile_start = tile_end - tiles_e
    slot = (tile_start * TM)[e_flat] + rank
    tok = jnp.tile(jnp.arange(t_rows, dtype=jnp.int32), TOP_K)
    slot_tok = jnp.zeros((n_tiles * TM,), jnp.int32).at[slot].set(tok)
    n_used = tile_end[-1]
    tile_ids = jnp.arange(n_tiles)
    tile_eid = jnp.sum(tile_ids[:, None] >= tile_end[None, :], axis=1)
    last_eid = jnp.sum(n_used - 1 >= tile_end)
    tile_eid = jnp.where(tile_ids < n_used, tile_eid, last_eid).astype(jnp.int32)
    pos = slot.reshape(TOP_K, t_rows).astype(jnp.int32)
    return (slot_tok.reshape(n_tiles, 1, TM), tile_eid, n_used.reshape(1).astype(jnp.int32),
            pos[0].reshape(t_rows // TM, 1, TM), pos[1].reshape(t_rows // TM, 1, TM))


def _prep_w_in(w):
    d = w.shape[0]
    q_a, k_a, v_a, u_p, c_q, c_kv, k_r, b_g, c_g, x_c, gate = jnp.split(w, IN_SPLITS, axis=1)
    z = lambda n: jnp.zeros((d, n), w.dtype)
    kr_e, kr_o = k_r[:, 0::2], k_r[:, 1::2]
    cols = [gate, q_a * NA_SCALE, k_a, v_a, u_p, b_g, c_g, x_c, c_q, z(128), c_kv,
            z(64), kr_e, kr_o, -kr_o, kr_e, z(128)]
    return jnp.concatenate(cols, axis=1).astype(BF16)


def _prep_w_uq(w_uq):
    wq = w_uq.reshape(MLA_Q_LORA, MLA_HEADS, MLA_NOPE + MLA_ROPE)
    nope, r = wq[..., :MLA_NOPE], wq[..., MLA_NOPE:]
    re, ro = r[..., 0::2], r[..., 1::2]
    ext = jnp.concatenate([nope, re, ro, -ro, re], axis=-1).reshape(MLA_Q_LORA, MLA_HEADS * 128)
    return jnp.pad(ext, ((0, 512 - MLA_Q_LORA), (0, 0))).T.astype(BF16)


def _prep_w_ukv(w_ukv):
    wkv = w_ukv.reshape(MLA_KV_LORA, MLA_HEADS, MLA_NOPE + MLA_V)
    zeros = jnp.zeros((MLA_KV_LORA, MLA_HEADS, 64), w_ukv.dtype)
    kn = jnp.concatenate([wkv[..., :MLA_NOPE], zeros], axis=-1).reshape(MLA_KV_LORA, MLA_HEADS * 128)
    v = wkv[..., MLA_NOPE:]
    odd = (jnp.arange(MLA_HEADS) % 2 == 1)[None, :, None]
    vt = jnp.where(odd, jnp.concatenate([zeros, v], axis=-1), jnp.concatenate([v, zeros], axis=-1))
    return kn.astype(BF16), vt.reshape(MLA_KV_LORA, MLA_HEADS * 128).T.astype(BF16)


def _rope_tables(nb, s_len, n_ctx_rows):
    pos = jnp.arange(s_len)
    row = (pos // GRID_W).astype(F32)
    col = (pos % GRID_W).astype(F32)
    n_pairs = MLA_ROPE // 4
    inv_freq = ROPE_BASE ** (-jnp.arange(n_pairs, dtype=F32) / n_pairs)
    ang = jnp.concatenate([row[:, None] * inv_freq, col[:, None] * inv_freq], axis=-1)
    cos = jnp.concatenate([jnp.tile(jnp.cos(ang), (nb, 1)), jnp.ones((n_ctx_rows, 16), F32)], axis=0)
    sin = jnp.concatenate([jnp.tile(jnp.sin(ang), (nb, 1)), jnp.zeros((n_ctx_rows, 16), F32)], axis=0)
    r = cos.shape[0]
    cos2 = jnp.concatenate([cos, cos], axis=1)
    sin2 = jnp.concatenate([sin, sin], axis=1)
    pad = lambda t, lead: jnp.concatenate([lead, t, jnp.zeros((r, 32), F32)], axis=1)
    return (pad(cos2, jnp.ones((r, 64), F32)), pad(sin2, jnp.zeros((r, 64), F32)), cos2.T, sin2.T)


def kernel(x, c, ctx, c_ctx, w_ada, b_ada, g_norm, w_in, na_rpb, pool_w, pool_scale, mla_g_q, mla_w_uq,
           mla_g_kv, mla_w_ukv, conv_w, w_branch, w_out, ffn_w1, ffn_w3, ffn_w2, moe_router, moe_w1,
           moe_w3, moe_w2):
    nb, s_len, d = x.shape
    c_len = ctx.shape[1]
    depth = w_in.shape[0]
    t_rows = nb * s_len
    n_ctx = nb * c_len
    assert s_len % TM == 0 and n_ctx == TM and s_len % GRID_W == 0 and t_rows % c_len == 0
    assert c_len == KC and c_len == TL
    tpb = s_len // TM
    rows = s_len // GRID_W

    xall = jnp.concatenate([x.reshape(t_rows, d), ctx.reshape(n_ctx, d)], axis=0)
    c8 = jnp.zeros((8, d), F32).at[:nb].set(c).at[nb].set(c_ctx)
    cos_t, sin_t, cos_tt, sin_tt = _rope_tables(nb, s_len, n_ctx)

    for l in range(depth):
        last = l == depth - 1
        mod = _adaln(c8, w_ada[l], b_ada[l]).reshape(8, 6, 1, d)
        mods = [mod[:, k] for k in range(6)]

        proj = _inproj(xall, g_norm[l, 0], mods[0], mods[1], _prep_w_in(w_in[l]), tpb)

        wkn, wvt = _prep_w_ukv(mla_w_ukv[l])
        gq = jnp.pad(mla_g_q[l], (0, 512 - MLA_Q_LORA)).reshape(1, 512)
        qt_m, k_m, vt_m = _mla_prep(proj, gq, mla_g_kv[l].reshape(1, MLA_KV_LORA), _prep_w_uq(mla_w_uq[l]),
                                    wkn, wvt, cos_t, sin_t, cos_tt, sin_tt)
        o_c = _mla_attn(qt_m, k_m, vt_m, nb, s_len, c_len)
        o_a = _na_attn(proj, _na_bias_table(na_rpb[l]), nb, s_len, c_len)
        o_b, o_d = _local(proj, pool_w[l].astype(BF16), pool_scale[l].reshape(1, 512), conv_w[l],
                          nb, s_len, c_len)

        wb = w_branch[l].astype(BF16)
        wo = w_out[l].astype(BF16)
        if not last:
            ctx_blk = t_rows // c_len
            o_cc = _mla_ctx_attn(qt_m, k_m, vt_m, nb, t_rows, c_len)
            o_ac = _ctx_attn(proj, proj, proj, nb, c_len, ctx_blk, 64, OFF_QA, OFF_KA, OFF_VA, "na_ctx_attn")
            o_a = jnp.concatenate([o_a, o_ac], axis=0)
            o_c = jnp.concatenate([o_c, o_cc], axis=0)
            n_rows = t_rows + n_ctx
        else:
            n_rows = t_rows
        xall = _merge(o_a, o_b, o_c, o_d, proj, wb, wo, xall, mods[2], g_norm[l, 1], n_rows, tpb)

        if l % 2 == 0:
            j = l // 2
            xall = _ffn(xall, g_norm[l, 2], mods[3], mods[4], mods[5], g_norm[l, 3],
                        ffn_w1[j].astype(BF16), ffn_w3[j].astype(BF16), ffn_w2[j].astype(BF16), tpb)
        else:
            j = l // 2
            assert last, "context rows are not routed through the experts"
            t_f32, top_i, top_w = _router(xall, g_norm[l, 2], mods[3], mods[4], moe_router[j].T, tpb)
            slot_tok, tile_eid, n_used, pos0, pos1 = _routing_tables(top_i, t_rows)
            ys = _moe_ffn(t_f32, slot_tok, tile_eid, n_used, moe_w1[j].astype(BF16), moe_w3[j].astype(BF16),
                          moe_w2[j].astype(BF16))
            xall = _combine(pos0, pos1, ys, top_w.T, xall, mods[5], g_norm[l, 3], tpb)

    return xall[:t_rows].reshape(nb, s_len, d)
```

```python
import functools

import numpy as np
import jax
import jax.numpy as jnp
from jax import lax
from jax.experimental import pallas as pl
from jax.experimental.pallas import tpu as pltpu

F32 = jnp.float32
BF16 = jnp.bfloat16

GRID_W = 64
N_BRANCH = 4
BRANCH_W = 512
EPS = 1e-6
NA_HEADS = 8
NA_HEAD_DIM = 64
NA_ROWS = 8
NA_COLS = 16
NA_G = 4
NA_WIN = NA_G + NA_ROWS
POOL_WINDOWS = (2, 4, 8, 16)
POOL_GROUP = 128
MLA_HEADS = 8
MLA_Q_LORA = 384
MLA_KV_LORA = 256
MLA_NOPE = 64
MLA_ROPE = 32
MLA_V = 64
ROPE_BASE = 10000.0
CONV_K = 3
N_EXPERTS = 8
TOP_K = 2

IN_SIZES = (512, 512, 512, 512, MLA_Q_LORA, MLA_KV_LORA, MLA_ROPE, 512, 512, 512, 4096)
IN_SPLITS = tuple(int(s) for s in np.cumsum(IN_SIZES)[:-1])

OFF_GATE = 0
OFF_QA = 4096
OFF_KA = 4608
OFF_VA = 5120
OFF_UP = 5632
OFF_BG = 6144
OFF_CG = 6656
OFF_XC = 7168
OFF_CQ = 7680
OFF_CKV = 8192
OFF_KR = 8448
N_IN = 8704

NA_SCALE = NA_HEAD_DIM ** -0.5
MLA_SCALE = (MLA_NOPE + MLA_ROPE) ** -0.5
LOG2E = 1.4426950408889634
NEG = -1e30

VMEM_LIMIT = 52 * 1024 * 1024
TM = 512
TL = 256
HALO = 16
KC = 256
MOE_F_STEPS = 2


def _cparams(sem):
    return pltpu.CompilerParams(dimension_semantics=sem, vmem_limit_bytes=VMEM_LIMIT)


def _rms(xf, g):
    ms = jnp.mean(xf * xf, axis=-1, keepdims=True)
    return xf * lax.rsqrt(ms + EPS) * g


def _dot(a, b):
    return jnp.dot(a, b, preferred_element_type=F32)


def _dot_nt(a, b):
    return lax.dot_general(a, b, (((1,), (1,)), ((), ())), preferred_element_type=F32)


def _adaln_kernel(c_ref, w_ref, b_ref, o_ref):
    c = c_ref[...]
    sc = c * jax.nn.sigmoid(c)
    o_ref[...] = jnp.dot(sc, w_ref[...], preferred_element_type=F32,
                         precision=lax.Precision.HIGHEST) + b_ref[...]


def _adaln(c8, w, b):
    d = c8.shape[1]
    n = w.shape[1]
    tn = 1536
    return pl.pallas_call(
        _adaln_kernel,
        grid=(n // tn,),
        in_specs=[pl.BlockSpec((8, d), lambda j: (0, 0)),
                  pl.BlockSpec((d, tn), lambda j: (0, j)),
                  pl.BlockSpec((1, tn), lambda j: (0, j))],
        out_specs=pl.BlockSpec((8, tn), lambda j: (0, j)),
        out_shape=jax.ShapeDtypeStruct((8, n), F32),
        compiler_params=_cparams(("arbitrary",)),
        name="adaln",
    )(c8, w, b.reshape(1, n))


def _inproj_kernel(x_ref, g_ref, sh_ref, sc_ref, w_ref, o_ref, h_ref):
    @pl.when(pl.program_id(1) == 0)
    def _():
        h = _rms(x_ref[...], g_ref[...]) * (1.0 + sc_ref[...]) + sh_ref[...]
        h_ref[...] = h.astype(BF16)

    o_ref[...] = _dot(h_ref[...], w_ref[...]).astype(BF16)


def _inproj(xall, g, shift, scale, w_p, tiles_per_batch):
    r, d = xall.shape
    n = w_p.shape[1]
    tn = n // 4
    mod_spec = pl.BlockSpec((None, 1, d), lambda i, j: (i // tiles_per_batch, 0, 0))
    return pl.pallas_call(
        _inproj_kernel,
        grid=(r // TM, n // tn),
        in_specs=[pl.BlockSpec((TM, d), lambda i, j: (i, 0)),
                  pl.BlockSpec((1, d), lambda i, j: (0, 0)),
                  mod_spec, mod_spec,
                  pl.BlockSpec((d, tn), lambda i, j: (0, j))],
        out_specs=pl.BlockSpec((TM, tn), lambda i, j: (i, j)),
        out_shape=jax.ShapeDtypeStruct((r, n), BF16),
        scratch_shapes=[pltpu.VMEM((TM, d), BF16)],
        compiler_params=_cparams(("parallel", "arbitrary")),
        name="inproj",
    )(xall, g.reshape(1, d), shift, scale, w_p)


def _mla_prep_kernel(cq_ref, ckv_ref, kr_ref, gq_ref, gkv_ref, wqt_ref, wkn_ref, wvt_ref,
                     cos_ref, sin_ref, cost_ref, sint_ref, qt_out, k_out, vt_out):
    cq = cq_ref[...].astype(F32)
    ms = jnp.sum(cq * cq, axis=-1, keepdims=True) * (1.0 / MLA_Q_LORA)
    cqn = (cq * lax.rsqrt(ms + EPS) * gq_ref[...]).astype(BF16)
    qt = _dot_nt(wqt_ref[...], cqn) * (MLA_SCALE * LOG2E)
    cost = cost_ref[...]
    sint = sint_ref[...]
    for h in range(MLA_HEADS):
        b = 128 * h
        qt_out[b:b + 64, :] = qt[b:b + 64].astype(BF16)
        qt_out[b + 64:b + 96, :] = (qt[b + 64:b + 96] * cost + qt[b + 96:b + 128] * sint).astype(BF16)
        qt_out[b + 96:b + 128, :] = jnp.zeros((32, TM), BF16)

    ckvn = _rms(ckv_ref[...].astype(F32), gkv_ref[...]).astype(BF16)
    kn = _dot(ckvn, wkn_ref[...])
    kseg = kr_ref[...].astype(F32)
    krt = kseg * cos_ref[...] + pltpu.roll(kseg, 96, 1) * sin_ref[...]
    for h in range(MLA_HEADS):
        sl = slice(128 * h, 128 * (h + 1))
        k_out[:, sl] = (kn[:, sl] + krt).astype(BF16)

    vt = _dot_nt(wvt_ref[...], ckvn)
    row = lax.broadcasted_iota(jnp.int32, vt.shape, 0)
    ones = ((row & 127) >= 64) != (((row >> 7) & 1) == 1)
    vt = jnp.where(ones, 1.0, vt).astype(BF16)
    for j in range(TM // KC):
        vt_out[j] = vt[:, KC * j:KC * (j + 1)]


def _mla_prep(proj, gq, gkv, wqt, wkn, wvt, cos_t, sin_t, cos_tt, sin_tt):
    r = proj.shape[0]
    full = lambda shape: pl.BlockSpec(shape, lambda i: (0, 0))
    return pl.pallas_call(
        _mla_prep_kernel,
        grid=(r // TM,),
        in_specs=[pl.BlockSpec((TM, 512), lambda i: (i, OFF_CQ // 512)),
                  pl.BlockSpec((TM, 256), lambda i: (i, OFF_CKV // 256)),
                  pl.BlockSpec((TM, 128), lambda i: (i, OFF_KR // 128)),
                  full((1, 512)), full((1, 256)),
                  full((1024, 512)), full((256, 1024)), full((1024, 256)),
                  pl.BlockSpec((TM, 128), lambda i: (i, 0)),
                  pl.BlockSpec((TM, 128), lambda i: (i, 0)),
                  pl.BlockSpec((32, TM), lambda i: (0, i)),
                  pl.BlockSpec((32, TM), lambda i: (0, i))],
        out_specs=[pl.BlockSpec((1024, TM), lambda i: (0, i)),
                   pl.BlockSpec((TM, 1024), lambda i: (i, 0)),
                   pl.BlockSpec((TM // KC, 1024, KC), lambda i: (i, 0, 0))],
        out_shape=[jax.ShapeDtypeStruct((1024, r), BF16),
                   jax.ShapeDtypeStruct((r, 1024), BF16),
                   jax.ShapeDtypeStruct((r // KC, 1024, KC), BF16)],
        compiler_params=_cparams(("parallel",)),
        name="mla_prep",
    )(proj, proj, proj, gq, gkv, wqt, wkn, wvt, cos_t, sin_t, cos_tt, sin_tt)


def _pick_heads(o0, o1):
    lane = lax.broadcasted_iota(jnp.int32, o0.shape, 1)
    return jnp.where(lane < 64, o0, o1)


def _mla_update(s, vt, m_old, acc_ref):
    m_new = jnp.maximum(m_old, jnp.max(s, axis=0, keepdims=True))
    alpha = jnp.exp2(m_old - m_new)
    p = jnp.exp2(s - m_new).astype(BF16)
    acc_ref[...] = alpha * acc_ref[...] + _dot(vt, p)
    return m_new


def _mla_finish(a0, a1):
    return _pick_heads(a0 / a0[:, 64:65], a1 / a1[:, 0:1]).astype(BF16)


def _mla_attn_kernel(qt_ref, k_ref, vt_ref, kc_ref, vtc_ref, o_ref, *acc_refs, cpi):
    nt = qt_ref.shape[1] // 256
    units = [(t, h) for h in range(2) for t in range(nt)]
    qts = [qt_ref[128 * h:128 * (h + 1), 256 * t:256 * (t + 1)] for t, h in units]
    n_chunks = k_ref.shape[0] // KC
    for acc_ref in acc_refs:
        acc_ref[...] = jnp.zeros_like(acc_ref)

    def scores(kget):
        return tuple(_dot(kget(h), qts[u]) for u, (t, h) in enumerate(units))

    def latent_keys(j):
        off = pl.multiple_of(j * KC, KC)
        return lambda h: k_ref[pl.ds(off, KC), 128 * h:128 * (h + 1)]

    def step(s_cur, vget, ms, kget_next):
        s_next, ms_new = [], []
        for u, (t, h) in enumerate(units):
            if kget_next is not None:
                s_next.append(_dot(kget_next(h), qts[u]))
            ms_new.append(_mla_update(s_cur[u], vget(h), ms[u], acc_refs[u]))
        return tuple(s_next), tuple(ms_new)

    ms = tuple(jnp.full((1, 256), -jnp.inf, F32) for _ in units)
    s_ctx = scores(lambda h: kc_ref[:, 128 * h:128 * (h + 1)])
    s_cur, ms = step(s_ctx, lambda h: vtc_ref[0, 128 * h:128 * (h + 1), :], ms, latent_keys(0))

    def values(j):
        return lambda h: vt_ref[j, 128 * h:128 * (h + 1), :]

    def body(i, carry):
        ms, s_cur = carry
        for c in range(cpi):
            j = i * cpi + c
            s_cur, ms = step(s_cur, values(j), ms, latent_keys(j + 1))
        return ms, s_cur

    n_iter = n_chunks // cpi - 1
    ms, s_cur = lax.fori_loop(0, n_iter, body, (ms, s_cur))
    for j in range(n_iter * cpi, n_chunks):
        s_cur, ms = step(s_cur, values(j), ms, latent_keys(j + 1) if j + 1 < n_chunks else None)
    for t in range(nt):
        o_ref[256 * t:256 * (t + 1), :] = _mla_finish(acc_refs[units.index((t, 0))][...].T,
                                                      acc_refs[units.index((t, 1))][...].T)


def _mla_attn(qt, k, vt, nb, s_len, c_len, out_rows):
    tq = min(512, s_len)
    nq = s_len // tq
    t_rows = nb * s_len
    cpi = s_len // KC
    return pl.pallas_call(
        functools.partial(_mla_attn_kernel, cpi=cpi),
        grid=(nb, 4, nq),
        in_specs=[pl.BlockSpec((256, tq), lambda b, hp, i: (hp, b * nq + i)),
                  pl.BlockSpec((s_len, 256), lambda b, hp, i: (b, hp)),
                  pl.BlockSpec((s_len // KC, 256, KC), lambda b, hp, i: (b, hp, 0)),
                  pl.BlockSpec((c_len, 256), lambda b, hp, i: (t_rows // c_len + b, hp)),
                  pl.BlockSpec((1, 256, KC), lambda b, hp, i: (t_rows // KC + b, hp, 0))],
        out_specs=pl.BlockSpec((tq, 128), lambda b, hp, i: (b * nq + i, hp)),
        out_shape=jax.ShapeDtypeStruct((out_rows, 512), BF16),
        scratch_shapes=[pltpu.VMEM((128, 256), F32)] * (2 * (tq // 256)),
        compiler_params=_cparams(("parallel", "parallel", "arbitrary")),
        name="mla_attn",
    )(qt, k, vt, k, vt)


def _mla_ctx_attn_kernel(qt_ref, kc_ref, vtc_ref, o_full_ref, o_ref):
    del o_full_ref
    outs = []
    for h in range(2):
        s = _dot(kc_ref[:, 128 * h:128 * (h + 1)], qt_ref[128 * h:128 * (h + 1), :])
        p = jnp.exp2(s - jnp.max(s, axis=0, keepdims=True)).astype(BF16)
        outs.append(_dot(vtc_ref[0, 128 * h:128 * (h + 1), :], p).T)
    o_ref[...] = _mla_finish(outs[0], outs[1])


def _mla_ctx_attn(qt, k, vt, o_full, nb, t_rows, c_len):
    blk0 = t_rows // c_len
    return pl.pallas_call(
        _mla_ctx_attn_kernel,
        grid=(nb, 4),
        in_specs=[pl.BlockSpec((256, c_len), lambda b, hp: (hp, blk0 + b)),
                  pl.BlockSpec((c_len, 256), lambda b, hp: (blk0 + b, hp)),
                  pl.BlockSpec((1, 256, KC), lambda b, hp: (blk0 + b, hp, 0)),
                  pl.BlockSpec(memory_space=pl.ANY)],
        out_specs=pl.BlockSpec((c_len, 128), lambda b, hp: (blk0 + b, hp)),
        out_shape=jax.ShapeDtypeStruct(o_full.shape, BF16),
        input_output_aliases={3: 0},
        compiler_params=_cparams(("parallel", "parallel")),
        name="mla_ctx_attn",
    )(qt, k, vt, o_full)


def _ctx_attn_kernel(q_ref, k_ref, v_ref, o_full_ref, o_ref, *, dqk):
    del o_full_ref
    v = v_ref[...]
    outs = []
    for h in range(2):
        sl = slice(dqk * h, dqk * (h + 1))
        s = _dot_nt(q_ref[:, sl], k_ref[:, sl])
        m = jnp.max(s, axis=-1, keepdims=True)
        p = jnp.exp(s - m)
        l = jnp.sum(p, axis=-1, keepdims=True)
        outs.append(_dot(p.astype(BF16), v) / l)
    o_ref[...] = _pick_heads(outs[0], outs[1]).astype(BF16)


def _ctx_attn(q, k, v, o_full, nb, c_len, row_blk0, dqk, qcol, kcol, vcol, name):
    w = 2 * dqk
    return pl.pallas_call(
        functools.partial(_ctx_attn_kernel, dqk=dqk),
        grid=(nb, 4),
        in_specs=[pl.BlockSpec((c_len, w), lambda b, hp: (row_blk0 + b, qcol // w + hp)),
                  pl.BlockSpec((c_len, w), lambda b, hp: (row_blk0 + b, kcol // w + hp)),
                  pl.BlockSpec((c_len, 128), lambda b, hp: (row_blk0 + b, vcol // 128 + hp)),
                  pl.BlockSpec(memory_space=pl.ANY)],
        out_specs=pl.BlockSpec((c_len, 128), lambda b, hp: (row_blk0 + b, hp)),
        out_shape=jax.ShapeDtypeStruct(o_full.shape, BF16),
        input_output_aliases={3: 0},
        compiler_params=_cparams(("parallel", "parallel")),
        name=name,
    )(q, k, v, o_full)


def _na_fill_bias(tab_ref, bias_ref):
    lane = lax.broadcasted_iota(jnp.int32, (GRID_W, 2 * GRID_W), 1)
    neg = jnp.full((GRID_W, 2 * GRID_W), NEG, F32)
    for pat in range(3):
        for j in range(NA_G):
            kr_lo = (0, j, NA_G)[pat]
            for h in range(2):
                for kp in range(NA_WIN // 2):
                    halves = []
                    for kr in (2 * kp, 2 * kp + 1):
                        d_row = kr - NA_G * pat - j + NA_ROWS - 1
                        halves.append(tab_ref[h, d_row] if kr_lo <= kr < kr_lo + NA_ROWS else neg)
                    bias_ref[pat, h, GRID_W * j:GRID_W * (j + 1), 2 * GRID_W * kp:2 * GRID_W * (kp + 1)] = (
                        jnp.where(lane < GRID_W, halves[0], halves[1]))


def _na_attn_kernel(q_ref, k_ref, v_ref, kc_ref, vc_ref, tab_ref, o_ref, bias_ref, *, rows):
    nq = NA_G * GRID_W
    nk = NA_WIN * GRID_W
    vc = vc_ref[...]
    _na_fill_bias(tab_ref, bias_ref)

    def body(g, carry):
        r0 = g * NA_G
        ws = jnp.clip(r0 - NA_ROWS // 2, 0, rows - NA_WIN)
        pat = lax.shift_right_logical(r0 - ws, 2)
        qoff = pl.multiple_of(r0 * GRID_W, nq)
        koff = pl.multiple_of(ws * GRID_W, GRID_W)
        q = q_ref[pl.ds(qoff, nq), :]
        kw = k_ref[pl.ds(koff, nk), :]
        vw = v_ref[pl.ds(koff, nk), :]
        scores = []
        for h in range(2):
            sl = slice(64 * h, 64 * (h + 1))
            scores.append((_dot_nt(q[:, sl], kw[:, sl]) + bias_ref[pat, h], _dot_nt(q[:, sl], kc_ref[:, sl])))
        outs = []
        for s, sc in scores:
            m = jnp.maximum(jnp.max(s, axis=-1, keepdims=True), jnp.max(sc, axis=-1, keepdims=True))
            p = jnp.exp(s - m)
            pc = jnp.exp(sc - m)
            l = jnp.sum(p, axis=-1, keepdims=True) + jnp.sum(pc, axis=-1, keepdims=True)
            outs.append((_dot(p.astype(BF16), vw) + _dot(pc.astype(BF16), vc)) / l)
        o_ref[pl.ds(qoff, nq), :] = _pick_heads(outs[0], outs[1]).astype(BF16)
        return carry

    lax.fori_loop(0, rows // NA_G, body, 0)


def _na_attn(proj, tab, nb, s_len, c_len, out_rows):
    rows = s_len // GRID_W
    assert NA_G == 4 and rows % NA_G == 0 and rows >= NA_WIN and NA_WIN % 2 == 0
    ctx_blk = nb * s_len // c_len
    return pl.pallas_call(
        functools.partial(_na_attn_kernel, rows=rows),
        grid=(nb, 4),
        in_specs=[pl.BlockSpec((s_len, 128), lambda b, hp: (b, OFF_QA // 128 + hp)),
                  pl.BlockSpec((s_len, 128), lambda b, hp: (b, OFF_KA // 128 + hp)),
                  pl.BlockSpec((s_len, 128), lambda b, hp: (b, OFF_VA // 128 + hp)),
                  pl.BlockSpec((c_len, 128), lambda b, hp: (ctx_blk + b, OFF_KA // 128 + hp)),
                  pl.BlockSpec((c_len, 128), lambda b, hp: (ctx_blk + b, OFF_VA // 128 + hp)),
                  pl.BlockSpec((2, 2 * NA_ROWS - 1, GRID_W, 2 * GRID_W), lambda b, hp: (hp, 0, 0, 0))],
        out_specs=pl.BlockSpec((s_len, 128), lambda b, hp: (b, hp)),
        out_shape=jax.ShapeDtypeStruct((out_rows, 512), BF16),
        scratch_shapes=[pltpu.VMEM((3, 2, NA_G * GRID_W, NA_WIN * GRID_W), F32)],
        compiler_params=_cparams(("parallel", "parallel")),
        name="na_attn",
    )(proj, proj, proj, proj, proj, tab)


def _na_bias_table(rpb):
    col = jnp.arange(GRID_W)
    cstart = jnp.clip(col - NA_COLS // 2, 0, GRID_W - NA_COLS)
    kc = jnp.arange(GRID_W)
    valid_c = (kc[None, :] >= cstart[:, None]) & (kc[None, :] < cstart[:, None] + NA_COLS)
    dcol = jnp.clip(kc[None, :] - col[:, None] + (NA_COLS - 1), 0, 2 * NA_COLS - 2)
    tab = jnp.where(valid_c[None, None], rpb[:, :, dcol], NEG).astype(F32)
    return jnp.concatenate([tab, tab], axis=-1)


def _local_kernel(up_p, up_c, up_n, cg_p, cg_c, cg_n, xc_p, xc_c, xc_n, bg_ref,
                  pw_ref, ps_ref, cw_ref, ob_ref, od_ref, *, lat_tiles, tiles_lat_seq, s_len, c_len):
    i = pl.program_id(0)
    is_lat = i < lat_tiles
    j = jnp.where(is_lat, i % tiles_lat_seq, 0)
    n_seq = jnp.where(is_lat, s_len, c_len)
    first = j == 0
    last = (j + 1) * TL == n_seq

    def ext(p_ref, c_ref, n_ref):
        p = jnp.where(first, 0.0, p_ref[...].astype(F32))
        n = jnp.where(last, 0.0, n_ref[...].astype(F32))
        return jnp.concatenate([p, c_ref[...].astype(F32), n], axis=0)

    z = ext(cg_p, cg_c, cg_n) * ext(xc_p, xc_c, xc_n)
    cw = cw_ref[...]
    y = (cw[0:1] * z[HALO - 1:HALO - 1 + TL] + cw[1:2] * z[HALO:HALO + TL]
         + cw[2:3] * z[HALO + 1:HALO + 1 + TL])
    od_ref[...] = (bg_ref[...].astype(F32) * y).astype(BF16)

    u = ext(up_p, up_c, up_n)
    t = j * TL + lax.broadcasted_iota(jnp.int32, (TL, 1), 0)
    ps = ps_ref[...]
    for g, w in enumerate(POOL_WINDOWS):
        sl = slice(POOL_GROUP * g, POOL_GROUP * (g + 1))
        ug = u[:, sl]
        acc = ug[HALO - w // 2:HALO - w // 2 + TL]
        for d in range(-w // 2 + 1, w // 2):
            acc = acc + ug[HALO + d:HALO + d + TL]
        cnt = (jnp.minimum(t - w // 2 + w, n_seq) - jnp.maximum(t - w // 2, 0)).astype(F32)
        pooled = acc / cnt - ug[HALO:HALO + TL]
        mixed = _dot(pooled.astype(BF16), pw_ref[g])
        ob_ref[:, sl] = (mixed * ps[:, sl]).astype(BF16)


def _local(proj, pool_w, pool_scale, conv_w, nb, s_len, c_len):
    r = proj.shape[0]
    lat_tiles = nb * s_len // TL
    hpt = TL // HALO
    nhalo = r // HALO

    def cur(off):
        return pl.BlockSpec((TL, 512), lambda i: (i, off // 512))

    def prev(off):
        return pl.BlockSpec((HALO, 512), lambda i: (jnp.maximum(i * hpt - 1, 0), off // 512))

    def nxt(off):
        return pl.BlockSpec((HALO, 512), lambda i: (jnp.minimum((i + 1) * hpt, nhalo - 1), off // 512))

    specs = []
    for off in (OFF_UP, OFF_CG, OFF_XC):
        specs += [prev(off), cur(off), nxt(off)]
    specs += [cur(OFF_BG),
              pl.BlockSpec((4, POOL_GROUP, POOL_GROUP), lambda i: (0, 0, 0)),
              pl.BlockSpec((1, 512), lambda i: (0, 0)),
              pl.BlockSpec((CONV_K, 512), lambda i: (0, 0))]
    return pl.pallas_call(
        functools.partial(_local_kernel, lat_tiles=lat_tiles, tiles_lat_seq=s_len // TL,
                          s_len=s_len, c_len=c_len),
        grid=(r // TL,),
        in_specs=specs,
        out_specs=[pl.BlockSpec((TL, 512), lambda i: (i, 0)), pl.BlockSpec((TL, 512), lambda i: (i, 0))],
        out_shape=[jax.ShapeDtypeStruct((r, 512), BF16), jax.ShapeDtypeStruct((r, 512), BF16)],
        compiler_params=_cparams(("parallel",)),
        name="local_mix",
    )(*([proj] * 10), pool_w, pool_scale, conv_w)


def _merge_kernel(oa_ref, ob_ref, oc_ref, od_ref, gate_ref, wb_ref, wo_ref, x_ref, gm_ref, g_ref, o_ref):
    merged = None
    for k, o in enumerate((oa_ref, ob_ref, oc_ref, od_ref)):
        proj = _dot(o[...], wb_ref[k])
        gk = jax.nn.sigmoid(gate_ref[:, 1024 * k:1024 * (k + 1)].astype(F32))
        merged = gk * proj if merged is None else merged + gk * proj
    y = _dot(merged.astype(BF16), wo_ref[...])
    o_ref[...] = x_ref[...] + gm_ref[...] * _rms(y, g_ref[...])


def _merge(oa, ob, oc, od, proj, wb, wo, xall, gate_mod, g, n_rows, tiles_per_batch):
    d = xall.shape[1]
    tm = 256
    tpb = tiles_per_batch * (TM // tm)
    row = lambda w: pl.BlockSpec((tm, w), lambda i: (i, 0))
    return pl.pallas_call(
        _merge_kernel,
        grid=(n_rows // tm,),
        in_specs=[row(512), row(512), row(512), row(512), row(4096),
                  pl.BlockSpec((4, 512, d), lambda i: (0, 0, 0)),
                  pl.BlockSpec((d, d), lambda i: (0, 0)),
                  row(d),
                  pl.BlockSpec((None, 1, d), lambda i: (i // tpb, 0, 0)),
                  pl.BlockSpec((1, d), lambda i: (0, 0))],
        out_specs=row(d),
        out_shape=jax.ShapeDtypeStruct((n_rows, d), F32),
        compiler_params=_cparams(("parallel",)),
        name="merge",
    )(oa, ob, oc, od, proj, wb, wo, xall, gate_mod, g.reshape(1, d))


def _ffn_kernel(x_ref, g2_ref, sh_ref, sc_ref, gm_ref, g3_ref, w1_ref, w3_ref, w2_ref, o_ref, h_ref, acc_ref):
    f = pl.program_id(1)

    @pl.when(f == 0)
    def _():
        h = _rms(x_ref[...], g2_ref[...]) * (1.0 + sc_ref[...]) + sh_ref[...]
        h_ref[...] = h.astype(BF16)
        acc_ref[...] = jnp.zeros_like(acc_ref)

    h = h_ref[...]
    a = _dot(h, w1_ref[...])
    b = _dot(h, w3_ref[...])
    acc_ref[...] += _dot((a * jax.nn.sigmoid(a) * b).astype(BF16), w2_ref[...])

    @pl.when(f == pl.num_programs(1) - 1)
    def _():
        o_ref[...] = x_ref[...] + gm_ref[...] * _rms(acc_ref[...], g3_ref[...])


def _ffn(xall, g2, shift, scale, gate_mod, g3, w1, w3, w2, tiles_per_batch):
    r, d = xall.shape
    dff = w1.shape[1]
    tf = dff // 2
    mod = pl.BlockSpec((None, 1, d), lambda i, f: (i // tiles_per_batch, 0, 0))
    vec = pl.BlockSpec((1, d), lambda i, f: (0, 0))
    return pl.pallas_call(
        _ffn_kernel,
        grid=(r // TM, dff // tf),
        in_specs=[pl.BlockSpec((TM, d), lambda i, f: (i, 0)), vec, mod, mod, mod, vec,
                  pl.BlockSpec((d, tf), lambda i, f: (0, f)),
                  pl.BlockSpec((d, tf), lambda i, f: (0, f)),
                  pl.BlockSpec((tf, d), lambda i, f: (f, 0))],
        out_specs=pl.BlockSpec((TM, d), lambda i, f: (i, 0)),
        out_shape=jax.ShapeDtypeStruct((r, d), F32),
        scratch_shapes=[pltpu.VMEM((TM, d), BF16), pltpu.VMEM((TM, d), F32)],
        compiler_params=_cparams(("parallel", "arbitrary")),
        name="ffn_dense",
    )(xall, g2.reshape(1, d), shift, scale, gate_mod, g3.reshape(1, d), w1, w3, w2)


def _router_kernel(x_ref, g2_ref, sh_ref, sc_ref, rt_ref, t_ref, idx_ref, w_ref):
    t = _rms(x_ref[...], g2_ref[...]) * (1.0 + sc_ref[...]) + sh_ref[...]
    t_ref[...] = t
    logits = lax.dot_general(rt_ref[...], t, (((1,), (1,)), ((), ())), preferred_element_type=F32,
                             precision=lax.Precision.HIGHEST)
    e = lax.broadcasted_iota(jnp.int32, logits.shape, 0).astype(F32)
    m1 = jnp.max(logits, axis=0, keepdims=True)
    i1 = jnp.min(jnp.where(logits == m1, e, float(N_EXPERTS)), axis=0, keepdims=True)
    rest = jnp.where(e == i1, -jnp.inf, logits)
    m2 = jnp.max(rest, axis=0, keepdims=True)
    i2 = jnp.min(jnp.where(rest == m2, e, float(N_EXPERTS)), axis=0, keepdims=True)
    ex = jnp.exp(m2 - m1)
    w1 = 1.0 / (1.0 + ex)
    idx_ref[0:1, :] = i1.astype(jnp.int32)
    idx_ref[1:2, :] = i2.astype(jnp.int32)
    w_ref[0:1, :] = w1
    w_ref[1:2, :] = ex * w1


def _router(x, g2, shift, scale, router_t, tiles_per_batch):
    t_rows, d = x.shape
    mod = pl.BlockSpec((None, 1, d), lambda i: (i // tiles_per_batch, 0, 0))
    return pl.pallas_call(
        _router_kernel,
        grid=(t_rows // TM,),
        in_specs=[pl.BlockSpec((TM, d), lambda i: (i, 0)),
                  pl.BlockSpec((1, d), lambda i: (0, 0)), mod, mod,
                  pl.BlockSpec((N_EXPERTS, d), lambda i: (0, 0))],
        out_specs=[pl.BlockSpec((TM, d), lambda i: (i, 0)),
                   pl.BlockSpec((TOP_K, TM), lambda i: (0, i)),
                   pl.BlockSpec((TOP_K, TM), lambda i: (0, i))],
        out_shape=[jax.ShapeDtypeStruct((t_rows, d), F32),
                   jax.ShapeDtypeStruct((TOP_K, t_rows), jnp.int32),
                   jax.ShapeDtypeStruct((TOP_K, t_rows), F32)],
        compiler_params=_cparams(("parallel",)),
        name="router",
    )(x, g2.reshape(1, d), shift, scale, router_t)


def _row_gather_start(ids_ref, src_ref, buf_ref, sem, n):
    def issue(r, c):
        pltpu.make_async_copy(src_ref.at[pl.ds(ids_ref[0, r], 1)], buf_ref.at[pl.ds(r, 1)], sem).start()
        return c

    lax.fori_loop(0, n, issue, 0, unroll=8)


def _row_gather_wait(src_ref, buf_ref, sem, n):
    pltpu.make_async_copy(src_ref.at[pl.ds(0, n)], buf_ref, sem).wait()


def _moe_ffn_kernel(eid_ref, nused_ref, ids_ref, ids_next_ref, t_ref, w1_ref, w3_ref, w2_ref, o_ref,
                    xbuf_ref, xs_ref, acc_ref, sems):
    i = pl.program_id(0)
    f = pl.program_id(1)
    n_f = pl.num_programs(1)
    slot = lax.rem(i, 2)

    def row_copy(ids, r, s):
        return pltpu.make_async_copy(t_ref.at[pl.ds(ids[0, r], 1)], xbuf_ref.at[s, pl.ds(r, 1)], sems.at[s])

    def all_rows(s):
        return pltpu.make_async_copy(t_ref.at[pl.ds(0, TM)], xbuf_ref.at[s], sems.at[s])

    @pl.when((i == 0) & (f == 0))
    def _():
        def issue(r, c):
            row_copy(ids_ref, r, 0).start()
            return c

        lax.fori_loop(0, TM, issue, 0)

    @pl.when(i < nused_ref[0])
    def _():
        @pl.when(f == 0)
        def _():
            all_rows(slot).wait()
            xs_ref[...] = xbuf_ref[slot].astype(BF16)
            acc_ref[...] = jnp.zeros_like(acc_ref)

        for rr in range(TM // MOE_F_STEPS):
            row_copy(ids_next_ref, f * (TM // MOE_F_STEPS) + rr, 1 - slot).start()

        x = xs_ref[...]
        a = _dot(x, w1_ref[...])
        b = _dot(x, w3_ref[...])
        acc_ref[...] += _dot((a * jax.nn.sigmoid(a) * b).astype(BF16), w2_ref[...])

        @pl.when(f == n_f - 1)
        def _():
            o_ref[...] = acc_ref[...]

    @pl.when((i + 1 == nused_ref[0]) & (f == n_f - 1))
    def _():
        all_rows(1 - slot).wait()

    @pl.when((i >= nused_ref[0]) & (f == n_f - 1))
    def _():
        o_ref[...] = jnp.zeros_like(o_ref)


def _moe_ffn(t, slot_tok, tile_eid, n_used, w1, w3, w2):
    n_tiles = slot_tok.shape[0]
    d = t.shape[1]
    dff = w1.shape[2]
    tf = dff // MOE_F_STEPS
    wf = lambda i, f, nu: jnp.where(i < nu[0], f, MOE_F_STEPS - 1)
    ids = lambda nxt: pl.BlockSpec((None, 1, TM),
                                   lambda i, f, eid, nu: (jnp.minimum(i + nxt, n_tiles - 1), 0, 0),
                                   memory_space=pltpu.SMEM)
    grid_spec = pltpu.PrefetchScalarGridSpec(
        num_scalar_prefetch=2,
        grid=(n_tiles, MOE_F_STEPS),
        in_specs=[ids(0), ids(1),
                  pl.BlockSpec(memory_space=pl.ANY),
                  pl.BlockSpec((None, d, tf), lambda i, f, eid, nu: (eid[i], 0, wf(i, f, nu))),
                  pl.BlockSpec((None, d, tf), lambda i, f, eid, nu: (eid[i], 0, wf(i, f, nu))),
                  pl.BlockSpec((None, tf, d), lambda i, f, eid, nu: (eid[i], wf(i, f, nu), 0))],
        out_specs=pl.BlockSpec((TM, d), lambda i, f, eid, nu: (i, 0)),
        scratch_shapes=[pltpu.VMEM((2, TM, d), F32), pltpu.VMEM((TM, d), BF16), pltpu.VMEM((TM, d), F32),
                        pltpu.SemaphoreType.DMA((2,))])
    return pl.pallas_call(
        _moe_ffn_kernel,
        grid_spec=grid_spec,
        out_shape=jax.ShapeDtypeStruct((n_tiles * TM, d), F32),
        compiler_params=_cparams(("arbitrary", "arbitrary")),
        name="moe_ffn",
    )(tile_eid, n_used, slot_tok, slot_tok, t, w1, w3, w2)


def _combine_kernel(p0_ref, p1_ref, p0n_ref, p1n_ref, ys_ref, w_ref, x_ref, gm_ref, g3_ref, o_ref,
                    b0_ref, b1_ref, sems):
    i = pl.program_id(0)
    slot = lax.rem(i, 2)

    def start(s, ids0, ids1):
        _row_gather_start(ids0, ys_ref, b0_ref.at[s], sems.at[0, s], TM)
        _row_gather_start(ids1, ys_ref, b1_ref.at[s], sems.at[1, s], TM)

    def wait(s):
        _row_gather_wait(ys_ref, b0_ref.at[s], sems.at[0, s], TM)
        _row_gather_wait(ys_ref, b1_ref.at[s], sems.at[1, s], TM)

    @pl.when(i == 0)
    def _():
        start(0, p0_ref, p1_ref)

    start(1 - slot, p0n_ref, p1n_ref)
    wait(slot)
    w = w_ref[...]
    y = w[:, 0:1] * b0_ref[slot] + w[:, 1:2] * b1_ref[slot]
    o_ref[...] = x_ref[...] + gm_ref[...] * _rms(y, g3_ref[...])

    @pl.when(i == pl.num_programs(0) - 1)
    def _():
        wait(1 - slot)


def _combine(pos0, pos1, ys, wcol, x, gate_mod, g3, tiles_per_batch):
    t_rows, d = x.shape
    n_tiles = t_rows // TM
    ids = lambda nxt: pl.BlockSpec((None, 1, TM), lambda i: (jnp.minimum(i + nxt, n_tiles - 1), 0, 0),
                                   memory_space=pltpu.SMEM)
    return pl.pallas_call(
        _combine_kernel,
        grid=(n_tiles,),
        in_specs=[ids(0), ids(0), ids(1), ids(1),
                  pl.BlockSpec(memory_space=pl.ANY),
                  pl.BlockSpec((TM, TOP_K), lambda i: (i, 0)),
                  pl.BlockSpec((TM, d), lambda i: (i, 0)),
                  pl.BlockSpec((None, 1, d), lambda i: (i // tiles_per_batch, 0, 0)),
                  pl.BlockSpec((1, d), lambda i: (0, 0))],
        out_specs=pl.BlockSpec((TM, d), lambda i: (i, 0)),
        out_shape=jax.ShapeDtypeStruct((t_rows, d), F32),
        scratch_shapes=[pltpu.VMEM((2, TM, d), F32), pltpu.VMEM((2, TM, d), F32),
                        pltpu.SemaphoreType.DMA((2, 2))],
        compiler_params=_cparams(("arbitrary",)),
        name="moe_combine",
    )(pos0, pos1, pos0, pos1, ys, wcol, x, gate_mod, g3.reshape(1, d))


def _routing_tables(top_i, t_rows):
    n_assign = TOP_K * t_rows
    n_tiles = n_assign // TM + N_EXPERTS
    e_flat = top_i.reshape(n_assign)
    onehot = (e_flat[:, None] == jnp.arange(N_EXPERTS)[None, :]).astype(jnp.int32)
    csum = jnp.cumsum(onehot, axis=0)
    counts = csum[-1]
    rank = jnp.sum((csum - onehot) * onehot, axis=1)
    tiles_e = (counts + TM - 1) // TM
    tile_end = jnp.cumsum(tiles_e)
    tile_start = tile_end - tiles_e
    slot = (tile_start * TM)[e_flat] + rank
    tok = jnp.tile(jnp.arange(t_rows, dtype=jnp.int32), TOP_K)
    slot_tok = jnp.zeros((n_tiles * TM,), jnp.int32).at[slot].set(tok)
    n_used = tile_end[-1]
    tile_ids = jnp.arange(n_tiles)
    tile_eid = jnp.sum(tile_ids[:, None] >= tile_end[None, :], axis=1)
    last_eid = jnp.sum(n_used - 1 >= tile_end)
    tile_eid = jnp.where(tile_ids < n_used, tile_eid, last_eid).astype(jnp.int32)
    pos = slot.reshape(TOP_K, t_rows).astype(jnp.int32)
    return (slot_tok.reshape(n_tiles, 1, TM), tile_eid, n_used.reshape(1).astype(jnp.int32),
            pos[0].reshape(t_rows // TM, 1, TM), pos[1].reshape(t_rows // TM, 1, TM))


def _prep_w_in(w):
    d = w.shape[0]
    q_a, k_a, v_a, u_p, c_q, c_kv, k_r, b_g, c_g, x_c, gate = jnp.split(w, IN_SPLITS, axis=1)
    z = lambda n: jnp.zeros((d, n), w.dtype)
    kr_e, kr_o = k_r[:, 0::2], k_r[:, 1::2]
    cols = [gate, q_a * NA_SCALE, k_a, v_a, u_p, b_g, c_g, x_c, c_q, z(128), c_kv,
            z(64), kr_e, kr_o, -kr_o, kr_e, z(128)]
    return jnp.concatenate(cols, axis=1).astype(BF16)


def _prep_w_uq(w_uq):
    wq = w_uq.reshape(MLA_Q_LORA, MLA_HEADS, MLA_NOPE + MLA_ROPE)
    nope, r = wq[..., :MLA_NOPE], wq[..., MLA_NOPE:]
    re, ro = r[..., 0::2], r[..., 1::2]
    ext = jnp.concatenate([nope, re, ro, -ro, re], axis=-1).reshape(MLA_Q_LORA, MLA_HEADS * 128)
    return jnp.pad(ext, ((0, 512 - MLA_Q_LORA), (0, 0))).T.astype(BF16)


def _prep_w_ukv(w_ukv):
    wkv = w_ukv.reshape(MLA_KV_LORA, MLA_HEADS, MLA_NOPE + MLA_V)
    zeros = jnp.zeros((MLA_KV_LORA, MLA_HEADS, 64), w_ukv.dtype)
    kn = jnp.concatenate([wkv[..., :MLA_NOPE], zeros], axis=-1).reshape(MLA_KV_LORA, MLA_HEADS * 128)
    v = wkv[..., MLA_NOPE:]
    odd = (jnp.arange(MLA_HEADS) % 2 == 1)[None, :, None]
    vt = jnp.where(odd, jnp.concatenate([zeros, v], axis=-1), jnp.concatenate([v, zeros], axis=-1))
    return kn.astype(BF16), vt.reshape(MLA_KV_LORA, MLA_HEADS * 128).T.astype(BF16)


def _rope_tables(nb, s_len, n_ctx_rows):
    pos = jnp.arange(s_len)
    row = (pos // GRID_W).astype(F32)
    col = (pos % GRID_W).astype(F32)
    n_pairs = MLA_ROPE // 4
    inv_freq = ROPE_BASE ** (-jnp.arange(n_pairs, dtype=F32) / n_pairs)
    ang = jnp.concatenate([row[:, None] * inv_freq, col[:, None] * inv_freq], axis=-1)
    cos = jnp.concatenate([jnp.tile(jnp.cos(ang), (nb, 1)), jnp.ones((n_ctx_rows, 16), F32)], axis=0)
    sin = jnp.concatenate([jnp.tile(jnp.sin(ang), (nb, 1)), jnp.zeros((n_ctx_rows, 16), F32)], axis=0)
    r = cos.shape[0]
    cos2 = jnp.concatenate([cos, cos], axis=1)
    sin2 = jnp.concatenate([sin, sin], axis=1)
    pad = lambda t, lead: jnp.concatenate([lead, t, jnp.zeros((r, 32), F32)], axis=1)
    return (pad(cos2, jnp.ones((r, 64), F32)), pad(sin2, jnp.zeros((r, 64), F32)), cos2.T, sin2.T)


def kernel(x, c, ctx, c_ctx, w_ada, b_ada, g_norm, w_in, na_rpb, pool_w, pool_scale, mla_g_q, mla_w_uq,
           mla_g_kv, mla_w_ukv, conv_w, w_branch, w_out, ffn_w1, ffn_w3, ffn_w2, moe_router, moe_w1,
           moe_w3, moe_w2):
    nb, s_len, d = x.shape
    c_len = ctx.shape[1]
    depth = w_in.shape[0]
    t_rows = nb * s_len
    n_ctx = nb * c_len
    assert s_len % TM == 0 and n_ctx == TM and s_len % GRID_W == 0 and t_rows % c_len == 0
    assert c_len == KC and c_len == TL
    tpb = s_len // TM
    rows = s_len // GRID_W

    xall = jnp.concatenate([x.reshape(t_rows, d), ctx.reshape(n_ctx, d)], axis=0)
    c8 = jnp.zeros((8, d), F32).at[:nb].set(c).at[nb].set(c_ctx)
    cos_t, sin_t, cos_tt, sin_tt = _rope_tables(nb, s_len, n_ctx)

    for l in range(depth):
        last = l == depth - 1
        mod = _adaln(c8, w_ada[l], b_ada[l]).reshape(8, 6, 1, d)
        mods = [mod[:, k] for k in range(6)]

        proj = _inproj(xall, g_norm[l, 0], mods[0], mods[1], _prep_w_in(w_in[l]), tpb)

        wkn, wvt = _prep_w_ukv(mla_w_ukv[l])
        gq = jnp.pad(mla_g_q[l], (0, 512 - MLA_Q_LORA)).reshape(1, 512)
        qt_m, k_m, vt_m = _mla_prep(proj, gq, mla_g_kv[l].reshape(1, MLA_KV_LORA), _prep_w_uq(mla_w_uq[l]),
                                    wkn, wvt, cos_t, sin_t, cos_tt, sin_tt)
        n_rows = t_rows if last else t_rows + n_ctx
        o_c = _mla_attn(qt_m, k_m, vt_m, nb, s_len, c_len, n_rows)
        o_a = _na_attn(proj, _na_bias_table(na_rpb[l]), nb, s_len, c_len, n_rows)
        o_b, o_d = _local(proj, pool_w[l].astype(BF16), pool_scale[l].reshape(1, 512), conv_w[l],
                          nb, s_len, c_len)

        wb = w_branch[l].astype(BF16)
        wo = w_out[l].astype(BF16)
        if not last:
            ctx_blk = t_rows // c_len
            o_c = _mla_ctx_attn(qt_m, k_m, vt_m, o_c, nb, t_rows, c_len)
            o_a = _ctx_attn(proj, proj, proj, o_a, nb, c_len, ctx_blk, 64, OFF_QA, OFF_KA, OFF_VA, "na_ctx_attn")
        xall = _merge(o_a, o_b, o_c, o_d, proj, wb, wo, xall, mods[2], g_norm[l, 1], n_rows, tpb)

        if l % 2 == 0:
            j = l // 2
            xall = _ffn(xall, g_norm[l, 2], mods[3], mods[4], mods[5], g_norm[l, 3],
                        ffn_w1[j].astype(BF16), ffn_w3[j].astype(BF16), ffn_w2[j].astype(BF16), tpb)
        else:
            j = l // 2
            assert last, "context rows are not routed through the experts"
            t_f32, top_i, top_w = _router(xall, g_norm[l, 2], mods[3], mods[4], moe_router[j].T, tpb)
            slot_tok, tile_eid, n_used, pos0, pos1 = _routing_tables(top_i, t_rows)
            ys = _moe_ffn(t_f32, slot_tok, tile_eid, n_used, moe_w1[j].astype(BF16), moe_w3[j].astype(BF16),
                          moe_w2[j].astype(BF16))
            xall = _combine(pos0, pos1, ys, top_w.T, xall, mods[5], g_norm[l, 3], tpb)

    return xall[:t_rows].reshape(nb, s_len, d)
```

```python
import functools

import numpy as np
import jax
import jax.numpy as jnp
from jax import lax
from jax.experimental import pallas as pl
from jax.experimental.pallas import tpu as pltpu

F32 = jnp.float32
BF16 = jnp.bfloat16

GRID_W = 64
N_BRANCH = 4
BRANCH_W = 512
EPS = 1e-6
NA_HEADS = 8
NA_HEAD_DIM = 64
NA_ROWS = 8
NA_COLS = 16
NA_G = 4
NA_WIN = NA_G + NA_ROWS
POOL_WINDOWS = (2, 4, 8, 16)
POOL_GROUP = 128
MLA_HEADS = 8
MLA_Q_LORA = 384
MLA_KV_LORA = 256
MLA_NOPE = 64
MLA_ROPE = 32
MLA_V = 64
ROPE_BASE = 10000.0
CONV_K = 3
N_EXPERTS = 8
TOP_K = 2

IN_SIZES = (512, 512, 512, 512, MLA_Q_LORA, MLA_KV_LORA, MLA_ROPE, 512, 512, 512, 4096)
IN_SPLITS = tuple(int(s) for s in np.cumsum(IN_SIZES)[:-1])

OFF_GATE = 0
OFF_QA = 4096
OFF_KA = 4608
OFF_VA = 5120
OFF_UP = 5632
OFF_BG = 6144
OFF_CG = 6656
OFF_XC = 7168
OFF_CQ = 7680
OFF_CKV = 8192
OFF_KR = 8448
N_IN = 8704

NA_SCALE = NA_HEAD_DIM ** -0.5
MLA_SCALE = (MLA_NOPE + MLA_ROPE) ** -0.5
LOG2E = 1.4426950408889634
NEG = -1e30

VMEM_LIMIT = 52 * 1024 * 1024
TM = 512
TL = 256
HALO = 16
KC = 256
VROWS = 80
MOE_F_STEPS = 2


def _cparams(sem):
    return pltpu.CompilerParams(dimension_semantics=sem, vmem_limit_bytes=VMEM_LIMIT)


def _rms(xf, g):
    ms = jnp.mean(xf * xf, axis=-1, keepdims=True)
    return xf * lax.rsqrt(ms + EPS) * g


def _dot(a, b):
    return jnp.dot(a, b, preferred_element_type=F32)


def _dot_nt(a, b):
    return lax.dot_general(a, b, (((1,), (1,)), ((), ())), preferred_element_type=F32)


def _adaln_kernel(c_ref, w_ref, b_ref, o_ref):
    c = c_ref[...]
    sc = c * jax.nn.sigmoid(c)
    o_ref[...] = jnp.dot(sc, w_ref[...], preferred_element_type=F32,
                         precision=lax.Precision.HIGHEST) + b_ref[...]


def _adaln(c8, w, b):
    d = c8.shape[1]
    n = w.shape[1]
    tn = 1536
    return pl.pallas_call(
        _adaln_kernel,
        grid=(n // tn,),
        in_specs=[pl.BlockSpec((8, d), lambda j: (0, 0)),
                  pl.BlockSpec((d, tn), lambda j: (0, j)),
                  pl.BlockSpec((1, tn), lambda j: (0, j))],
        out_specs=pl.BlockSpec((8, tn), lambda j: (0, j)),
        out_shape=jax.ShapeDtypeStruct((8, n), F32),
        compiler_params=_cparams(("arbitrary",)),
        name="adaln",
    )(c8, w, b.reshape(1, n))


def _inproj_kernel(x_ref, g_ref, sh_ref, sc_ref, w_ref, o_ref, h_ref):
    @pl.when(pl.program_id(1) == 0)
    def _():
        h = _rms(x_ref[...], g_ref[...]) * (1.0 + sc_ref[...]) + sh_ref[...]
        h_ref[...] = h.astype(BF16)

    o_ref[...] = _dot(h_ref[...], w_ref[...]).astype(BF16)


def _inproj(xall, g, shift, scale, w_p, tiles_per_batch):
    r, d = xall.shape
    n = w_p.shape[1]
    tn = n // 4
    mod_spec = pl.BlockSpec((None, 1, d), lambda i, j: (i // tiles_per_batch, 0, 0))
    return pl.pallas_call(
        _inproj_kernel,
        grid=(r // TM, n // tn),
        in_specs=[pl.BlockSpec((TM, d), lambda i, j: (i, 0)),
                  pl.BlockSpec((1, d), lambda i, j: (0, 0)),
                  mod_spec, mod_spec,
                  pl.BlockSpec((d, tn), lambda i, j: (0, j))],
        out_specs=pl.BlockSpec((TM, tn), lambda i, j: (i, j)),
        out_shape=jax.ShapeDtypeStruct((r, n), BF16),
        scratch_shapes=[pltpu.VMEM((TM, d), BF16)],
        compiler_params=_cparams(("parallel", "arbitrary")),
        name="inproj",
    )(xall, g.reshape(1, d), shift, scale, w_p)


def _mla_prep_kernel(cq_ref, ckv_ref, kr_ref, gq_ref, gkv_ref, wqt_ref, wkn_ref, wvt_ref, ones_ref,
                     cos_ref, sin_ref, cost_ref, sint_ref, qt_out, k_out, vt_out):
    cq = cq_ref[...].astype(F32)
    ms = jnp.sum(cq * cq, axis=-1, keepdims=True) * (1.0 / MLA_Q_LORA)
    cqn = (cq * lax.rsqrt(ms + EPS) * gq_ref[...]).astype(BF16)
    qt = _dot_nt(wqt_ref[...], cqn) * (MLA_SCALE * LOG2E)
    cost = cost_ref[...]
    sint = sint_ref[...]
    for h in range(MLA_HEADS):
        b = 128 * h
        qt_out[b:b + 64, :] = qt[b:b + 64].astype(BF16)
        qt_out[b + 64:b + 96, :] = (qt[b + 64:b + 96] * cost + qt[b + 96:b + 128] * sint).astype(BF16)
        qt_out[b + 96:b + 128, :] = jnp.zeros((32, TM), BF16)

    ckvn = _rms(ckv_ref[...].astype(F32), gkv_ref[...]).astype(BF16)
    kn = _dot(ckvn, wkn_ref[...])
    kseg = kr_ref[...].astype(F32)
    krt = kseg * cos_ref[...] + pltpu.roll(kseg, 96, 1) * sin_ref[...]
    for h in range(MLA_HEADS):
        sl = slice(128 * h, 128 * (h + 1))
        k_out[:, sl] = (kn[:, sl] + krt).astype(BF16)

    vt = (_dot_nt(wvt_ref[...], ckvn) + ones_ref[...]).astype(BF16)
    for j in range(TM // KC):
        vt_out[j] = vt[:, KC * j:KC * (j + 1)]


def _mla_prep(proj, gq, gkv, wqt, wkn, wvt, cos_t, sin_t, cos_tt, sin_tt):
    r = proj.shape[0]
    nv = MLA_HEADS * VROWS
    ones_col = ((jnp.arange(nv) % VROWS) >= MLA_V).astype(F32).reshape(nv, 1)
    full = lambda shape: pl.BlockSpec(shape, lambda i: (0, 0))
    return pl.pallas_call(
        _mla_prep_kernel,
        grid=(r // TM,),
        in_specs=[pl.BlockSpec((TM, 512), lambda i: (i, OFF_CQ // 512)),
                  pl.BlockSpec((TM, 256), lambda i: (i, OFF_CKV // 256)),
                  pl.BlockSpec((TM, 128), lambda i: (i, OFF_KR // 128)),
                  full((1, 512)), full((1, 256)),
                  full((1024, 512)), full((256, 1024)), full((nv, 256)), full((nv, 1)),
                  pl.BlockSpec((TM, 128), lambda i: (i, 0)),
                  pl.BlockSpec((TM, 128), lambda i: (i, 0)),
                  pl.BlockSpec((32, TM), lambda i: (0, i)),
                  pl.BlockSpec((32, TM), lambda i: (0, i))],
        out_specs=[pl.BlockSpec((1024, TM), lambda i: (0, i)),
                   pl.BlockSpec((TM, 1024), lambda i: (i, 0)),
                   pl.BlockSpec((TM // KC, nv, KC), lambda i: (i, 0, 0))],
        out_shape=[jax.ShapeDtypeStruct((1024, r), BF16),
                   jax.ShapeDtypeStruct((r, 1024), BF16),
                   jax.ShapeDtypeStruct((r // KC, nv, KC), BF16)],
        compiler_params=_cparams(("parallel",)),
        name="mla_prep",
    )(proj, proj, proj, gq, gkv, wqt, wkn, wvt, ones_col, cos_t, sin_t, cos_tt, sin_tt)


def _alias(o_init):
    return ([], []) if o_init is None else ([pl.BlockSpec(memory_space=pl.ANY)], [o_init])


def _pick_heads(o0, o1):
    lane = lax.broadcasted_iota(jnp.int32, o0.shape, 1)
    return jnp.where(lane < 64, o0, o1)


def _mla_update(s, vt, m_old, acc_ref):
    m_new = jnp.maximum(m_old, jnp.max(s, axis=0, keepdims=True))
    alpha = jnp.exp2(m_old - m_new)
    p = jnp.exp2(s - m_new).astype(BF16)
    acc_ref[...] = alpha * acc_ref[...] + _dot(vt, p)
    return m_new


def _mla_finish(acc0, acc1):
    def rows_to_lanes(acc):
        return jnp.concatenate([acc, jnp.zeros((128 - VROWS, acc.shape[1]), F32)], axis=0).T
    a0 = rows_to_lanes(acc0)
    a1 = pltpu.roll(rows_to_lanes(acc1), 64, 1)
    return _pick_heads(a0 / a0[:, 64:65], a1 / a1[:, 0:1]).astype(BF16)


def _mla_attn_kernel(qt_ref, k_ref, vt_ref, kc_ref, vtc_ref, *rest, cpi, aliased):
    o_ref, *acc_refs = rest[1:] if aliased else rest
    nt = qt_ref.shape[1] // 256
    units = [(t, h) for h in range(2) for t in range(nt)]
    qts = [qt_ref[128 * h:128 * (h + 1), 256 * t:256 * (t + 1)] for t, h in units]
    n_chunks = k_ref.shape[0] // KC
    for acc_ref in acc_refs:
        acc_ref[...] = jnp.zeros_like(acc_ref)

    def scores(kget):
        return tuple(_dot(kget(h), qts[u]) for u, (t, h) in enumerate(units))

    def latent_keys(j):
        off = pl.multiple_of(j * KC, KC)
        return lambda h: k_ref[pl.ds(off, KC), 128 * h:128 * (h + 1)]

    def step(s_cur, vget, ms, kget_next):
        s_next, ms_new = [], []
        for u, (t, h) in enumerate(units):
            if kget_next is not None:
                s_next.append(_dot(kget_next(h), qts[u]))
            ms_new.append(_mla_update(s_cur[u], vget(h), ms[u], acc_refs[u]))
        return tuple(s_next), tuple(ms_new)

    ms = tuple(jnp.full((1, 256), -jnp.inf, F32) for _ in units)
    s_ctx = scores(lambda h: kc_ref[:, 128 * h:128 * (h + 1)])
    s_cur, ms = step(s_ctx, lambda h: vtc_ref[0, VROWS * h:VROWS * (h + 1), :], ms, latent_keys(0))

    def values(j):
        return lambda h: vt_ref[j, VROWS * h:VROWS * (h + 1), :]

    def body(i, carry):
        ms, s_cur = carry
        for c in range(cpi):
            j = i * cpi + c
            s_cur, ms = step(s_cur, values(j), ms, latent_keys(j + 1))
        return ms, s_cur

    n_iter = n_chunks // cpi - 1
    ms, s_cur = lax.fori_loop(0, n_iter, body, (ms, s_cur))
    for j in range(n_iter * cpi, n_chunks):
        s_cur, ms = step(s_cur, values(j), ms, latent_keys(j + 1) if j + 1 < n_chunks else None)
    for t in range(nt):
        o_ref[256 * t:256 * (t + 1), :] = _mla_finish(acc_refs[units.index((t, 0))][...],
                                                      acc_refs[units.index((t, 1))][...])


def _mla_attn(qt, k, vt, nb, s_len, c_len, o_init):
    tq = min(512, s_len)
    nq = s_len // tq
    t_rows = nb * s_len
    cpi = s_len // KC
    alias_spec, alias_arg = _alias(o_init)
    return pl.pallas_call(
        functools.partial(_mla_attn_kernel, cpi=cpi, aliased=o_init is not None),
        grid=(nb, 4, nq),
        in_specs=[pl.BlockSpec((256, tq), lambda b, hp, i: (hp, b * nq + i)),
                  pl.BlockSpec((s_len, 256), lambda b, hp, i: (b, hp)),
                  pl.BlockSpec((s_len // KC, 2 * VROWS, KC), lambda b, hp, i: (b, hp, 0)),
                  pl.BlockSpec((c_len, 256), lambda b, hp, i: (t_rows // c_len + b, hp)),
                  pl.BlockSpec((1, 2 * VROWS, KC), lambda b, hp, i: (t_rows // KC + b, hp, 0))] + alias_spec,
        out_specs=pl.BlockSpec((tq, 128), lambda b, hp, i: (b * nq + i, hp)),
        out_shape=jax.ShapeDtypeStruct((t_rows if o_init is None else o_init.shape[0], 512), BF16),
        input_output_aliases={} if o_init is None else {5: 0},
        scratch_shapes=[pltpu.VMEM((VROWS, 256), F32)] * (2 * (tq // 256)),
        compiler_params=_cparams(("parallel", "parallel", "arbitrary")),
        name="mla_attn",
    )(qt, k, vt, k, vt, *alias_arg)


def _mla_ctx_attn_kernel(qt_ref, kc_ref, vtc_ref, o_full_ref, o_ref):
    del o_full_ref
    outs = []
    for h in range(2):
        s = _dot(kc_ref[:, 128 * h:128 * (h + 1)], qt_ref[128 * h:128 * (h + 1), :])
        p = jnp.exp2(s - jnp.max(s, axis=0, keepdims=True)).astype(BF16)
        outs.append(_dot(vtc_ref[0, VROWS * h:VROWS * (h + 1), :], p))
    o_ref[...] = _mla_finish(outs[0], outs[1])


def _mla_ctx_attn(qt, k, vt, o_full, nb, t_rows, c_len):
    blk0 = t_rows // c_len
    return pl.pallas_call(
        _mla_ctx_attn_kernel,
        grid=(nb, 4),
        in_specs=[pl.BlockSpec((256, c_len), lambda b, hp: (hp, blk0 + b)),
                  pl.BlockSpec((c_len, 256), lambda b, hp: (blk0 + b, hp)),
                  pl.BlockSpec((1, 2 * VROWS, KC), lambda b, hp: (blk0 + b, hp, 0)),
                  pl.BlockSpec(memory_space=pl.ANY)],
        out_specs=pl.BlockSpec((c_len, 128), lambda b, hp: (blk0 + b, hp)),
        out_shape=jax.ShapeDtypeStruct(o_full.shape, BF16),
        input_output_aliases={3: 0},
        compiler_params=_cparams(("parallel", "parallel")),
        name="mla_ctx_attn",
    )(qt, k, vt, o_full)


def _ctx_attn_kernel(q_ref, k_ref, v_ref, o_full_ref, o_ref, *, dqk):
    del o_full_ref
    v = v_ref[...]
    outs = []
    for h in range(2):
        sl = slice(dqk * h, dqk * (h + 1))
        s = _dot_nt(q_ref[:, sl], k_ref[:, sl])
        m = jnp.max(s, axis=-1, keepdims=True)
        p = jnp.exp(s - m)
        l = jnp.sum(p, axis=-1, keepdims=True)
        outs.append(_dot(p.astype(BF16), v) / l)
    o_ref[...] = _pick_heads(outs[0], outs[1]).astype(BF16)


def _ctx_attn(q, k, v, o_full, nb, c_len, row_blk0, dqk, qcol, kcol, vcol, name):
    w = 2 * dqk
    return pl.pallas_call(
        functools.partial(_ctx_attn_kernel, dqk=dqk),
        grid=(nb, 4),
        in_specs=[pl.BlockSpec((c_len, w), lambda b, hp: (row_blk0 + b, qcol // w + hp)),
                  pl.BlockSpec((c_len, w), lambda b, hp: (row_blk0 + b, kcol // w + hp)),
                  pl.BlockSpec((c_len, 128), lambda b, hp: (row_blk0 + b, vcol // 128 + hp)),
                  pl.BlockSpec(memory_space=pl.ANY)],
        out_specs=pl.BlockSpec((c_len, 128), lambda b, hp: (row_blk0 + b, hp)),
        out_shape=jax.ShapeDtypeStruct(o_full.shape, BF16),
        input_output_aliases={3: 0},
        compiler_params=_cparams(("parallel", "parallel")),
        name=name,
    )(q, k, v, o_full)


def _na_fill_bias(tab_ref, bias_ref):
    lane = lax.broadcasted_iota(jnp.int32, (GRID_W, 2 * GRID_W), 1)
    neg = jnp.full((GRID_W, 2 * GRID_W), NEG, F32)
    for pat in range(3):
        for j in range(NA_G):
            kr_lo = (0, j, NA_G)[pat]
            for h in range(2):
                for kp in range(NA_WIN // 2):
                    halves = []
                    for kr in (2 * kp, 2 * kp + 1):
                        d_row = kr - NA_G * pat - j + NA_ROWS - 1
                        halves.append(tab_ref[h, d_row] if kr_lo <= kr < kr_lo + NA_ROWS else neg)
                    bias_ref[pat, h, GRID_W * j:GRID_W * (j + 1), 2 * GRID_W * kp:2 * GRID_W * (kp + 1)] = (
                        jnp.where(lane < GRID_W, halves[0], halves[1]))


def _na_attn_kernel(q_ref, k_ref, v_ref, kc_ref, vc_ref, tab_ref, *rest, rows, aliased):
    o_ref, bias_ref = rest[1:] if aliased else rest
    nq = NA_G * GRID_W
    nk = NA_WIN * GRID_W
    vc = vc_ref[...]
    _na_fill_bias(tab_ref, bias_ref)

    def body(g, carry):
        r0 = g * NA_G
        ws = jnp.clip(r0 - NA_ROWS // 2, 0, rows - NA_WIN)
        pat = lax.shift_right_logical(r0 - ws, 2)
        qoff = pl.multiple_of(r0 * GRID_W, nq)
        koff = pl.multiple_of(ws * GRID_W, GRID_W)
        q = q_ref[pl.ds(qoff, nq), :]
        kw = k_ref[pl.ds(koff, nk), :]
        vw = v_ref[pl.ds(koff, nk), :]
        scores = []
        for h in range(2):
            sl = slice(64 * h, 64 * (h + 1))
            scores.append((_dot_nt(q[:, sl], kw[:, sl]) + bias_ref[pat, h], _dot_nt(q[:, sl], kc_ref[:, sl])))
        outs = []
        for s, sc in scores:
            m = jnp.maximum(jnp.max(s, axis=-1, keepdims=True), jnp.max(sc, axis=-1, keepdims=True))
            p = jnp.exp(s - m)
            pc = jnp.exp(sc - m)
            l = jnp.sum(p, axis=-1, keepdims=True) + jnp.sum(pc, axis=-1, keepdims=True)
            outs.append((_dot(p.astype(BF16), vw) + _dot(pc.astype(BF16), vc)) / l)
        o_ref[pl.ds(qoff, nq), :] = _pick_heads(outs[0], outs[1]).astype(BF16)
        return carry

    lax.fori_loop(0, rows // NA_G, body, 0)


def _na_attn(proj, tab, nb, s_len, c_len, o_init):
    rows = s_len // GRID_W
    assert NA_G == 4 and rows % NA_G == 0 and rows >= NA_WIN and NA_WIN % 2 == 0
    ctx_blk = nb * s_len // c_len
    alias_spec, alias_arg = _alias(o_init)
    return pl.pallas_call(
        functools.partial(_na_attn_kernel, rows=rows, aliased=o_init is not None),
        grid=(nb, 4),
        in_specs=[pl.BlockSpec((s_len, 128), lambda b, hp: (b, OFF_QA // 128 + hp)),
                  pl.BlockSpec((s_len, 128), lambda b, hp: (b, OFF_KA // 128 + hp)),
                  pl.BlockSpec((s_len, 128), lambda b, hp: (b, OFF_VA // 128 + hp)),
                  pl.BlockSpec((c_len, 128), lambda b, hp: (ctx_blk + b, OFF_KA // 128 + hp)),
                  pl.BlockSpec((c_len, 128), lambda b, hp: (ctx_blk + b, OFF_VA // 128 + hp)),
                  pl.BlockSpec((2, 2 * NA_ROWS - 1, GRID_W, 2 * GRID_W), lambda b, hp: (hp, 0, 0, 0))] + alias_spec,
        out_specs=pl.BlockSpec((s_len, 128), lambda b, hp: (b, hp)),
        out_shape=jax.ShapeDtypeStruct((nb * s_len if o_init is None else o_init.shape[0], 512), BF16),
        input_output_aliases={} if o_init is None else {6: 0},
        scratch_shapes=[pltpu.VMEM((3, 2, NA_G * GRID_W, NA_WIN * GRID_W), F32)],
        compiler_params=_cparams(("parallel", "parallel")),
        name="na_attn",
    )(proj, proj, proj, proj, proj, tab, *alias_arg)


def _na_bias_table(rpb):
    col = jnp.arange(GRID_W)
    cstart = jnp.clip(col - NA_COLS // 2, 0, GRID_W - NA_COLS)
    kc = jnp.arange(GRID_W)
    valid_c = (kc[None, :] >= cstart[:, None]) & (kc[None, :] < cstart[:, None] + NA_COLS)
    dcol = jnp.clip(kc[None, :] - col[:, None] + (NA_COLS - 1), 0, 2 * NA_COLS - 2)
    tab = jnp.where(valid_c[None, None], rpb[:, :, dcol], NEG).astype(F32)
    return jnp.concatenate([tab, tab], axis=-1)


def _local_kernel(up_p, up_c, up_n, cg_p, cg_c, cg_n, xc_p, xc_c, xc_n, bg_ref,
                  pw_ref, ps_ref, cw_ref, ob_ref, od_ref, *, lat_tiles, tiles_lat_seq, s_len, c_len):
    i = pl.program_id(0)
    is_lat = i < lat_tiles
    j = jnp.where(is_lat, i % tiles_lat_seq, 0)
    n_seq = jnp.where(is_lat, s_len, c_len)
    first = j == 0
    last = (j + 1) * TL == n_seq

    def ext(p_ref, c_ref, n_ref):
        p = jnp.where(first, 0.0, p_ref[...].astype(F32))
        n = jnp.where(last, 0.0, n_ref[...].astype(F32))
        return jnp.concatenate([p, c_ref[...].astype(F32), n], axis=0)

    z = ext(cg_p, cg_c, cg_n) * ext(xc_p, xc_c, xc_n)
    cw = cw_ref[...]
    y = (cw[0:1] * z[HALO - 1:HALO - 1 + TL] + cw[1:2] * z[HALO:HALO + TL]
         + cw[2:3] * z[HALO + 1:HALO + 1 + TL])
    od_ref[...] = (bg_ref[...].astype(F32) * y).astype(BF16)

    u = ext(up_p, up_c, up_n)
    t = j * TL + lax.broadcasted_iota(jnp.int32, (TL, 1), 0)
    ps = ps_ref[...]
    for g, w in enumerate(POOL_WINDOWS):
        sl = slice(POOL_GROUP * g, POOL_GROUP * (g + 1))
        ug = u[:, sl]
        acc = ug[HALO - w // 2:HALO - w // 2 + TL]
        for d in range(-w // 2 + 1, w // 2):
            acc = acc + ug[HALO + d:HALO + d + TL]
        cnt = (jnp.minimum(t - w // 2 + w, n_seq) - jnp.maximum(t - w // 2, 0)).astype(F32)
        pooled = acc / cnt - ug[HALO:HALO + TL]
        mixed = _dot(pooled.astype(BF16), pw_ref[g])
        ob_ref[:, sl] = (mixed * ps[:, sl]).astype(BF16)


def _local(proj, pool_w, pool_scale, conv_w, nb, s_len, c_len):
    r = proj.shape[0]
    lat_tiles = nb * s_len // TL
    hpt = TL // HALO
    nhalo = r // HALO

    def cur(off):
        return pl.BlockSpec((TL, 512), lambda i: (i, off // 512))

    def prev(off):
        return pl.BlockSpec((HALO, 512), lambda i: (jnp.maximum(i * hpt - 1, 0), off // 512))

    def nxt(off):
        return pl.BlockSpec((HALO, 512), lambda i: (jnp.minimum((i + 1) * hpt, nhalo - 1), off // 512))

    specs = []
    for off in (OFF_UP, OFF_CG, OFF_XC):
        specs += [prev(off), cur(off), nxt(off)]
    specs += [cur(OFF_BG),
              pl.BlockSpec((4, POOL_GROUP, POOL_GROUP), lambda i: (0, 0, 0)),
              pl.BlockSpec((1, 512), lambda i: (0, 0)),
              pl.BlockSpec((CONV_K, 512), lambda i: (0, 0))]
    return pl.pallas_call(
        functools.partial(_local_kernel, lat_tiles=lat_tiles, tiles_lat_seq=s_len // TL,
                          s_len=s_len, c_len=c_len),
        grid=(r // TL,),
        in_specs=specs,
        out_specs=[pl.BlockSpec((TL, 512), lambda i: (i, 0)), pl.BlockSpec((TL, 512), lambda i: (i, 0))],
        out_shape=[jax.ShapeDtypeStruct((r, 512), BF16), jax.ShapeDtypeStruct((r, 512), BF16)],
        compiler_params=_cparams(("parallel",)),
        name="local_mix",
    )(*([proj] * 10), pool_w, pool_scale, conv_w)


def _merge_kernel(oa_ref, ob_ref, oc_ref, od_ref, gate_ref, wb_ref, wo_ref, x_ref, gm_ref, g_ref, o_ref):
    merged = None
    for k, o in enumerate((oa_ref, ob_ref, oc_ref, od_ref)):
        proj = _dot(o[...], wb_ref[k])
        gk = jax.nn.sigmoid(gate_ref[:, 1024 * k:1024 * (k + 1)].astype(F32))
        merged = gk * proj if merged is None else merged + gk * proj
    y = _dot(merged.astype(BF16), wo_ref[...])
    o_ref[...] = x_ref[...] + gm_ref[...] * _rms(y, g_ref[...])


def _merge(oa, ob, oc, od, proj, wb, wo, xall, gate_mod, g, n_rows, tiles_per_batch):
    d = xall.shape[1]
    tm = TM
    tpb = tiles_per_batch * (TM // tm)
    row = lambda w: pl.BlockSpec((tm, w), lambda i: (i, 0))
    return pl.pallas_call(
        _merge_kernel,
        grid=(n_rows // tm,),
        in_specs=[row(512), row(512), row(512), row(512), row(4096),
                  pl.BlockSpec((4, 512, d), lambda i: (0, 0, 0)),
                  pl.BlockSpec((d, d), lambda i: (0, 0)),
                  row(d),
                  pl.BlockSpec((None, 1, d), lambda i: (i // tpb, 0, 0)),
                  pl.BlockSpec((1, d), lambda i: (0, 0))],
        out_specs=row(d),
        out_shape=jax.ShapeDtypeStruct((n_rows, d), F32),
        compiler_params=_cparams(("parallel",)),
        name="merge",
    )(oa, ob, oc, od, proj, wb, wo, xall, gate_mod, g.reshape(1, d))


def _ffn_kernel(x_ref, g2_ref, sh_ref, sc_ref, gm_ref, g3_ref, w1_ref, w3_ref, w2_ref, o_ref, h_ref, acc_ref):
    f = pl.program_id(1)

    @pl.when(f == 0)
    def _():
        h = _rms(x_ref[...], g2_ref[...]) * (1.0 + sc_ref[...]) + sh_ref[...]
        h_ref[...] = h.astype(BF16)
        acc_ref[...] = jnp.zeros_like(acc_ref)

    h = h_ref[...]
    a = _dot(h, w1_ref[...])
    b = _dot(h, w3_ref[...])
    acc_ref[...] += _dot((a * jax.nn.sigmoid(a) * b).astype(BF16), w2_ref[...])

    @pl.when(f == pl.num_programs(1) - 1)
    def _():
        o_ref[...] = x_ref[...] + gm_ref[...] * _rms(acc_ref[...], g3_ref[...])


def _ffn(xall, g2, shift, scale, gate_mod, g3, w1, w3, w2, tiles_per_batch):
    r, d = xall.shape
    dff = w1.shape[1]
    tf = dff // 2
    mod = pl.BlockSpec((None, 1, d), lambda i, f: (i // tiles_per_batch, 0, 0))
    vec = pl.BlockSpec((1, d), lambda i, f: (0, 0))
    return pl.pallas_call(
        _ffn_kernel,
        grid=(r // TM, dff // tf),
        in_specs=[pl.BlockSpec((TM, d), lambda i, f: (i, 0)), vec, mod, mod, mod, vec,
                  pl.BlockSpec((d, tf), lambda i, f: (0, f)),
                  pl.BlockSpec((d, tf), lambda i, f: (0, f)),
                  pl.BlockSpec((tf, d), lambda i, f: (f, 0))],
        out_specs=pl.BlockSpec((TM, d), lambda i, f: (i, 0)),
        out_shape=jax.ShapeDtypeStruct((r, d), F32),
        scratch_shapes=[pltpu.VMEM((TM, d), BF16), pltpu.VMEM((TM, d), F32)],
        compiler_params=_cparams(("parallel", "arbitrary")),
        name="ffn_dense",
    )(xall, g2.reshape(1, d), shift, scale, gate_mod, g3.reshape(1, d), w1, w3, w2)


def _router_kernel(x_ref, g2_ref, sh_ref, sc_ref, rt_ref, t_ref, idx_ref, w_ref):
    t = _rms(x_ref[...], g2_ref[...]) * (1.0 + sc_ref[...]) + sh_ref[...]
    t_ref[...] = t
    logits = lax.dot_general(rt_ref[...], t, (((1,), (1,)), ((), ())), preferred_element_type=F32,
                             precision=lax.Precision.HIGHEST)
    e = lax.broadcasted_iota(jnp.int32, logits.shape, 0).astype(F32)
    m1 = jnp.max(logits, axis=0, keepdims=True)
    i1 = jnp.min(jnp.where(logits == m1, e, float(N_EXPERTS)), axis=0, keepdims=True)
    rest = jnp.where(e == i1, -jnp.inf, logits)
    m2 = jnp.max(rest, axis=0, keepdims=True)
    i2 = jnp.min(jnp.where(rest == m2, e, float(N_EXPERTS)), axis=0, keepdims=True)
    ex = jnp.exp(m2 - m1)
    w1 = 1.0 / (1.0 + ex)
    idx_ref[0:1, :] = i1.astype(jnp.int32)
    idx_ref[1:2, :] = i2.astype(jnp.int32)
    w_ref[0:1, :] = w1
    w_ref[1:2, :] = ex * w1


def _router(x, g2, shift, scale, router_t, tiles_per_batch):
    t_rows, d = x.shape
    mod = pl.BlockSpec((None, 1, d), lambda i: (i // tiles_per_batch, 0, 0))
    return pl.pallas_call(
        _router_kernel,
        grid=(t_rows // TM,),
        in_specs=[pl.BlockSpec((TM, d), lambda i: (i, 0)),
                  pl.BlockSpec((1, d), lambda i: (0, 0)), mod, mod,
                  pl.BlockSpec((N_EXPERTS, d), lambda i: (0, 0))],
        out_specs=[pl.BlockSpec((TM, d), lambda i: (i, 0)),
                   pl.BlockSpec((TOP_K, TM), lambda i: (0, i)),
                   pl.BlockSpec((TOP_K, TM), lambda i: (0, i))],
        out_shape=[jax.ShapeDtypeStruct((t_rows, d), F32),
                   jax.ShapeDtypeStruct((TOP_K, t_rows), jnp.int32),
                   jax.ShapeDtypeStruct((TOP_K, t_rows), F32)],
        compiler_params=_cparams(("parallel",)),
        name="router",
    )(x, g2.reshape(1, d), shift, scale, router_t)


def _row_gather_start(ids_ref, src_ref, buf_ref, sem, n):
    def issue(r, c):
        pltpu.make_async_copy(src_ref.at[pl.ds(ids_ref[0, r], 1)], buf_ref.at[pl.ds(r, 1)], sem).start()
        return c

    lax.fori_loop(0, n, issue, 0, unroll=8)


def _row_gather_wait(src_ref, buf_ref, sem, n):
    pltpu.make_async_copy(src_ref.at[pl.ds(0, n)], buf_ref, sem).wait()


def _moe_ffn_kernel(eid_ref, nused_ref, ids_ref, ids_next_ref, t_ref, w1_ref, w3_ref, w2_ref, o_ref,
                    xbuf_ref, xs_ref, acc_ref, sems):
    i = pl.program_id(0)
    f = pl.program_id(1)
    n_f = pl.num_programs(1)
    slot = lax.rem(i, 2)

    def row_copy(ids, r, s):
        return pltpu.make_async_copy(t_ref.at[pl.ds(ids[0, r], 1)], xbuf_ref.at[s, pl.ds(r, 1)], sems.at[s])

    def all_rows(s):
        return pltpu.make_async_copy(t_ref.at[pl.ds(0, TM)], xbuf_ref.at[s], sems.at[s])

    @pl.when((i == 0) & (f == 0))
    def _():
        def issue(r, c):
            row_copy(ids_ref, r, 0).start()
            return c

        lax.fori_loop(0, TM, issue, 0)

    @pl.when(i < nused_ref[0])
    def _():
        @pl.when(f == 0)
        def _():
            all_rows(slot).wait()
            xs_ref[...] = xbuf_ref[slot].astype(BF16)
            acc_ref[...] = jnp.zeros_like(acc_ref)

        for rr in range(TM // MOE_F_STEPS):
            row_copy(ids_next_ref, f * (TM // MOE_F_STEPS) + rr, 1 - slot).start()

        x = xs_ref[...]
        a = _dot(x, w1_ref[...])
        b = _dot(x, w3_ref[...])
        acc_ref[...] += _dot((a * jax.nn.sigmoid(a) * b).astype(BF16), w2_ref[...])

        @pl.when(f == n_f - 1)
        def _():
            o_ref[...] = acc_ref[...]

    @pl.when((i + 1 == nused_ref[0]) & (f == n_f - 1))
    def _():
        all_rows(1 - slot).wait()

    @pl.when((i >= nused_ref[0]) & (f == n_f - 1))
    def _():
        o_ref[...] = jnp.zeros_like(o_ref)


def _moe_ffn(t, slot_tok, tile_eid, n_used, w1, w3, w2):
    n_tiles = slot_tok.shape[0]
    d = t.shape[1]
    dff = w1.shape[2]
    tf = dff // MOE_F_STEPS
    wf = lambda i, f, nu: jnp.where(i < nu[0], f, MOE_F_STEPS - 1)
    ids = lambda nxt: pl.BlockSpec((None, 1, TM),
                                   lambda i, f, eid, nu: (jnp.minimum(i + nxt, n_tiles - 1), 0, 0),
                                   memory_space=pltpu.SMEM)
    grid_spec = pltpu.PrefetchScalarGridSpec(
        num_scalar_prefetch=2,
        grid=(n_tiles, MOE_F_STEPS),
        in_specs=[ids(0), ids(1),
                  pl.BlockSpec(memory_space=pl.ANY),
                  pl.BlockSpec((None, d, tf), lambda i, f, eid, nu: (eid[i], 0, wf(i, f, nu))),
                  pl.BlockSpec((None, d, tf), lambda i, f, eid, nu: (eid[i], 0, wf(i, f, nu))),
                  pl.BlockSpec((None, tf, d), lambda i, f, eid, nu: (eid[i], wf(i, f, nu), 0))],
        out_specs=pl.BlockSpec((TM, d), lambda i, f, eid, nu: (i, 0)),
        scratch_shapes=[pltpu.VMEM((2, TM, d), F32), pltpu.VMEM((TM, d), BF16), pltpu.VMEM((TM, d), F32),
                        pltpu.SemaphoreType.DMA((2,))])
    return pl.pallas_call(
        _moe_ffn_kernel,
        grid_spec=grid_spec,
        out_shape=jax.ShapeDtypeStruct((n_tiles * TM, d), F32),
        compiler_params=_cparams(("arbitrary", "arbitrary")),
        name="moe_ffn",
    )(tile_eid, n_used, slot_tok, slot_tok, t, w1, w3, w2)


def _combine_kernel(p0_ref, p1_ref, p0n_ref, p1n_ref, ys_ref, w_ref, x_ref, gm_ref, g3_ref, o_ref,
                    b0_ref, b1_ref, sems):
    i = pl.program_id(0)
    slot = lax.rem(i, 2)

    def start(s, ids0, ids1):
        _row_gather_start(ids0, ys_ref, b0_ref.at[s], sems.at[0, s], TM)
        _row_gather_start(ids1, ys_ref, b1_ref.at[s], sems.at[1, s], TM)

    def wait(s):
        _row_gather_wait(ys_ref, b0_ref.at[s], sems.at[0, s], TM)
        _row_gather_wait(ys_ref, b1_ref.at[s], sems.at[1, s], TM)

    @pl.when(i == 0)
    def _():
        start(0, p0_ref, p1_ref)

    start(1 - slot, p0n_ref, p1n_ref)
    wait(slot)
    w = w_ref[...]
    y = w[:, 0:1] * b0_ref[slot] + w[:, 1:2] * b1_ref[slot]
    o_ref[...] = x_ref[...] + gm_ref[...] * _rms(y, g3_ref[...])

    @pl.when(i == pl.num_programs(0) - 1)
    def _():
        wait(1 - slot)


def _combine(pos0, pos1, ys, wcol, x, gate_mod, g3, tiles_per_batch):
    t_rows, d = x.shape
    n_tiles = t_rows // TM
    ids = lambda nxt: pl.BlockSpec((None, 1, TM), lambda i: (jnp.minimum(i + nxt, n_tiles - 1), 0, 0),
                                   memory_space=pltpu.SMEM)
    return pl.pallas_call(
        _combine_kernel,
        grid=(n_tiles,),
        in_specs=[ids(0), ids(0), ids(1), ids(1),
                  pl.BlockSpec(memory_space=pl.ANY),
                  pl.BlockSpec((TM, TOP_K), lambda i: (i, 0)),
                  pl.BlockSpec((TM, d), lambda i: (i, 0)),
                  pl.BlockSpec((None, 1, d), lambda i: (i // tiles_per_batch, 0, 0)),
                  pl.BlockSpec((1, d), lambda i: (0, 0))],
        out_specs=pl.BlockSpec((TM, d), lambda i: (i, 0)),
        out_shape=jax.ShapeDtypeStruct((t_rows, d), F32),
        scratch_shapes=[pltpu.VMEM((2, TM, d), F32), pltpu.VMEM((2, TM, d), F32),
                        pltpu.SemaphoreType.DMA((2, 2))],
        compiler_params=_cparams(("arbitrary",)),
        name="moe_combine",
    )(pos0, pos1, pos0, pos1, ys, wcol, x, gate_mod, g3.reshape(1, d))


def _routing_tables(top_i, t_rows):
    n_assign = TOP_K * t_rows
    n_tiles = n_assign // TM + N_EXPERTS
    e_flat = top_i.reshape(n_assign)
    onehot = (e_flat[:, None] == jnp.arange(N_EXPERTS)[None, :]).astype(jnp.int32)
    csum = jnp.cumsum(onehot, axis=0)
    counts = csum[-1]
    rank = jnp.sum((csum - onehot) * onehot, axis=1)
    tiles_e = (counts + TM - 1) // TM
    tile_end = jnp.cumsum(tiles_e)
    tile_start = tile_end - tiles_e
    slot = (tile_start * TM)[e_flat] + rank
    tok = jnp.tile(jnp.arange(t_rows, dtype=jnp.int32), TOP_K)
    slot_tok = jnp.zeros((n_tiles * TM,), jnp.int32).at[slot].set(tok)
    n_used = tile_end[-1]
    tile_ids = jnp.arange(n_tiles)
    tile_eid = jnp.sum(tile_ids[:, None] >= tile_end[None, :], axis=1)
    last_eid = jnp.sum(n_used - 1 >= tile_end)
    tile_eid = jnp.where(tile_ids < n_used, tile_eid, last_eid).astype(jnp.int32)
    pos = slot.reshape(TOP_K, t_rows).astype(jnp.int32)
    return (slot_tok.reshape(n_tiles, 1, TM), tile_eid, n_used.reshape(1).astype(jnp.int32),
            pos[0].reshape(t_rows // TM, 1, TM), pos[1].reshape(t_rows // TM, 1, TM))


def _prep_w_in(w):
    d = w.shape[0]
    q_a, k_a, v_a, u_p, c_q, c_kv, k_r, b_g, c_g, x_c, gate = jnp.split(w, IN_SPLITS, axis=1)
    z = lambda n: jnp.zeros((d, n), w.dtype)
    kr_e, kr_o = k_r[:, 0::2], k_r[:, 1::2]
    cols = [gate, q_a * NA_SCALE, k_a, v_a, u_p, b_g, c_g, x_c, c_q, z(128), c_kv,
            z(64), kr_e, kr_o, -kr_o, kr_e, z(128)]
    return jnp.concatenate(cols, axis=1).astype(BF16)


def _prep_w_uq(w_uq):
    wq = w_uq.reshape(MLA_Q_LORA, MLA_HEADS, MLA_NOPE + MLA_ROPE)
    nope, r = wq[..., :MLA_NOPE], wq[..., MLA_NOPE:]
    re, ro = r[..., 0::2], r[..., 1::2]
    ext = jnp.concatenate([nope, re, ro, -ro, re], axis=-1).reshape(MLA_Q_LORA, MLA_HEADS * 128)
    return jnp.pad(ext, ((0, 512 - MLA_Q_LORA), (0, 0))).T.astype(BF16)


def _prep_w_ukv(w_ukv):
    wkv = w_ukv.reshape(MLA_KV_LORA, MLA_HEADS, MLA_NOPE + MLA_V)
    zeros = jnp.zeros((MLA_KV_LORA, MLA_HEADS, 64), w_ukv.dtype)
    kn = jnp.concatenate([wkv[..., :MLA_NOPE], zeros], axis=-1).reshape(MLA_KV_LORA, MLA_HEADS * 128)
    vt = jnp.concatenate([wkv[..., MLA_NOPE:], zeros[..., :VROWS - MLA_V]], axis=-1)
    return kn.astype(BF16), vt.reshape(MLA_KV_LORA, MLA_HEADS * VROWS).T.astype(BF16)


def _rope_tables(nb, s_len, n_ctx_rows):
    pos = jnp.arange(s_len)
    row = (pos // GRID_W).astype(F32)
    col = (pos % GRID_W).astype(F32)
    n_pairs = MLA_ROPE // 4
    inv_freq = ROPE_BASE ** (-jnp.arange(n_pairs, dtype=F32) / n_pairs)
    ang = jnp.concatenate([row[:, None] * inv_freq, col[:, None] * inv_freq], axis=-1)
    cos = jnp.concatenate([jnp.tile(jnp.cos(ang), (nb, 1)), jnp.ones((n_ctx_rows, 16), F32)], axis=0)
    sin = jnp.concatenate([jnp.tile(jnp.sin(ang), (nb, 1)), jnp.zeros((n_ctx_rows, 16), F32)], axis=0)
    r = cos.shape[0]
    cos2 = jnp.concatenate([cos, cos], axis=1)
    sin2 = jnp.concatenate([sin, sin], axis=1)
    pad = lambda t, lead: jnp.concatenate([lead, t, jnp.zeros((r, 32), F32)], axis=1)
    return (pad(cos2, jnp.ones((r, 64), F32)), pad(sin2, jnp.zeros((r, 64), F32)), cos2.T, sin2.T)


def kernel(x, c, ctx, c_ctx, w_ada, b_ada, g_norm, w_in, na_rpb, pool_w, pool_scale, mla_g_q, mla_w_uq,
           mla_g_kv, mla_w_ukv, conv_w, w_branch, w_out, ffn_w1, ffn_w3, ffn_w2, moe_router, moe_w1,
           moe_w3, moe_w2):
    nb, s_len, d = x.shape
    c_len = ctx.shape[1]
    depth = w_in.shape[0]
    t_rows = nb * s_len
    n_ctx = nb * c_len
    assert s_len % TM == 0 and n_ctx == TM and s_len % GRID_W == 0 and t_rows % c_len == 0
    assert c_len == KC and c_len == TL
    tpb = s_len // TM
    rows = s_len // GRID_W

    xall = jnp.concatenate([x.reshape(t_rows, d), ctx.reshape(n_ctx, d)], axis=0)
    c8 = jnp.zeros((8, d), F32).at[:nb].set(c).at[nb].set(c_ctx)
    cos_t, sin_t, cos_tt, sin_tt = _rope_tables(nb, s_len, n_ctx)

    for l in range(depth):
        last = l == depth - 1
        mod = _adaln(c8, w_ada[l], b_ada[l]).reshape(8, 6, 1, d)
        mods = [mod[:, k] for k in range(6)]

        proj = _inproj(xall, g_norm[l, 0], mods[0], mods[1], _prep_w_in(w_in[l]), tpb)

        wkn, wvt = _prep_w_ukv(mla_w_ukv[l])
        gq = jnp.pad(mla_g_q[l], (0, 512 - MLA_Q_LORA)).reshape(1, 512)
        qt_m, k_m, vt_m = _mla_prep(proj, gq, mla_g_kv[l].reshape(1, MLA_KV_LORA), _prep_w_uq(mla_w_uq[l]),
                                    wkn, wvt, cos_t, sin_t, cos_tt, sin_tt)
        n_rows = t_rows if last else t_rows + n_ctx
        o_init = None if last else jnp.zeros((n_rows, 512), BF16)
        o_c = _mla_attn(qt_m, k_m, vt_m, nb, s_len, c_len, o_init)
        o_a = _na_attn(proj, _na_bias_table(na_rpb[l]), nb, s_len, c_len, o_init)
        o_b, o_d = _local(proj, pool_w[l].astype(BF16), pool_scale[l].reshape(1, 512), conv_w[l],
                          nb, s_len, c_len)

        wb = w_branch[l].astype(BF16)
        wo = w_out[l].astype(BF16)
        if not last:
            ctx_blk = t_rows // c_len
            o_c = _mla_ctx_attn(qt_m, k_m, vt_m, o_c, nb, t_rows, c_len)
            o_a = _ctx_attn(proj, proj, proj, o_a, nb, c_len, ctx_blk, 64, OFF_QA, OFF_KA, OFF_VA, "na_ctx_attn")
        xall = _merge(o_a, o_b, o_c, o_d, proj, wb, wo, xall, mods[2], g_norm[l, 1], n_rows, tpb)

        if l % 2 == 0:
            j = l // 2
            xall = _ffn(xall, g_norm[l, 2], mods[3], mods[4], mods[5], g_norm[l, 3],
                        ffn_w1[j].astype(BF16), ffn_w3[j].astype(BF16), ffn_w2[j].astype(BF16), tpb)
        else:
            j = l // 2
            assert last, "context rows are not routed through the experts"
            t_f32, top_i, top_w = _router(xall, g_norm[l, 2], mods[3], mods[4], moe_router[j].T, tpb)
            slot_tok, tile_eid, n_used, pos0, pos1 = _routing_tables(top_i, t_rows)
            ys = _moe_ffn(t_f32, slot_tok, tile_eid, n_used, moe_w1[j].astype(BF16), moe_w3[j].astype(BF16),
                          moe_w2[j].astype(BF16))
            xall = _combine(pos0, pos1, ys, top_w.T, xall, mods[5], g_norm[l, 3], tpb)

    return xall[:t_rows].reshape(nb, s_len, d)
```

```python
import functools

import numpy as np
import jax
import jax.numpy as jnp
from jax import lax
from jax.experimental import pallas as pl
from jax.experimental.pallas import tpu as pltpu

F32 = jnp.float32
BF16 = jnp.bfloat16

GRID_W = 64
N_BRANCH = 4
BRANCH_W = 512
EPS = 1e-6
NA_HEADS = 8
NA_HEAD_DIM = 64
NA_ROWS = 8
NA_COLS = 16
NA_G = 4
NA_WIN = NA_G + NA_ROWS
POOL_WINDOWS = (2, 4, 8, 16)
POOL_GROUP = 128
MLA_HEADS = 8
MLA_Q_LORA = 384
MLA_KV_LORA = 256
MLA_NOPE = 64
MLA_ROPE = 32
MLA_V = 64
ROPE_BASE = 10000.0
CONV_K = 3
N_EXPERTS = 8
TOP_K = 2

IN_SIZES = (512, 512, 512, 512, MLA_Q_LORA, MLA_KV_LORA, MLA_ROPE, 512, 512, 512, 4096)
IN_SPLITS = tuple(int(s) for s in np.cumsum(IN_SIZES)[:-1])

OFF_GATE = 0
OFF_QA = 4096
OFF_KA = 4608
OFF_VA = 5120
OFF_UP = 5632
OFF_BG = 6144
OFF_CG = 6656
OFF_XC = 7168
OFF_CQ = 7680
OFF_CKV = 8192
OFF_KR = 8448
N_IN = 8704

NA_SCALE = NA_HEAD_DIM ** -0.5
MLA_SCALE = (MLA_NOPE + MLA_ROPE) ** -0.5
LOG2E = 1.4426950408889634
NEG = -1e30

VMEM_LIMIT = 52 * 1024 * 1024
TM = 512
TL = 256
HALO = 16
KC = 256
VROWS = 80
MOE_F_STEPS = 2


def _cparams(sem):
    return pltpu.CompilerParams(dimension_semantics=sem, vmem_limit_bytes=VMEM_LIMIT)


def _rms(xf, g):
    ms = jnp.mean(xf * xf, axis=-1, keepdims=True)
    return xf * lax.rsqrt(ms + EPS) * g


def _dot(a, b):
    return jnp.dot(a, b, preferred_element_type=F32)


def _dot_nt(a, b):
    return lax.dot_general(a, b, (((1,), (1,)), ((), ())), preferred_element_type=F32)


def _adaln_kernel(c_ref, w_ref, b_ref, o_ref):
    c = c_ref[...]
    sc = c * jax.nn.sigmoid(c)
    o_ref[...] = jnp.dot(sc, w_ref[...], preferred_element_type=F32,
                         precision=lax.Precision.HIGHEST) + b_ref[...]


def _adaln(c8, w, b):
    d = c8.shape[1]
    n = w.shape[1]
    tn = 1536
    return pl.pallas_call(
        _adaln_kernel,
        grid=(n // tn,),
        in_specs=[pl.BlockSpec((8, d), lambda j: (0, 0)),
                  pl.BlockSpec((d, tn), lambda j: (0, j)),
                  pl.BlockSpec((1, tn), lambda j: (0, j))],
        out_specs=pl.BlockSpec((8, tn), lambda j: (0, j)),
        out_shape=jax.ShapeDtypeStruct((8, n), F32),
        compiler_params=_cparams(("arbitrary",)),
        name="adaln",
    )(c8, w, b.reshape(1, n))


def _inproj_kernel(x_ref, g_ref, sh_ref, sc_ref, w_ref, o_ref, *, n_split):
    h = (_rms(x_ref[...], g_ref[...]) * (1.0 + sc_ref[...]) + sh_ref[...]).astype(BF16)
    tn = w_ref.shape[1] // n_split
    for j in range(n_split):
        o_ref[:, tn * j:tn * (j + 1)] = _dot(h, w_ref[:, tn * j:tn * (j + 1)]).astype(BF16)


def _inproj(xall, g, shift, scale, w_p, tiles_per_batch):
    r, d = xall.shape
    n = w_p.shape[1]
    mod_spec = pl.BlockSpec((None, 1, d), lambda i: (i // tiles_per_batch, 0, 0))
    return pl.pallas_call(
        functools.partial(_inproj_kernel, n_split=4),
        grid=(r // TM,),
        in_specs=[pl.BlockSpec((TM, d), lambda i: (i, 0)),
                  pl.BlockSpec((1, d), lambda i: (0, 0)),
                  mod_spec, mod_spec,
                  pl.BlockSpec((d, n), lambda i: (0, 0), pipeline_mode=pl.Buffered(1))],
        out_specs=pl.BlockSpec((TM, n), lambda i: (i, 0)),
        out_shape=jax.ShapeDtypeStruct((r, n), BF16),
        compiler_params=_cparams(("parallel",)),
        name="inproj",
    )(xall, g.reshape(1, d), shift, scale, w_p)


def _mla_prep_kernel(cq_ref, ckv_ref, kr_ref, gq_ref, gkv_ref, wqt_ref, wkn_ref, wvt_ref, ones_ref,
                     cos_ref, sin_ref, cost_ref, sint_ref, qt_out, k_out, vt_out):
    cq = cq_ref[...].astype(F32)
    ms = jnp.sum(cq * cq, axis=-1, keepdims=True) * (1.0 / MLA_Q_LORA)
    cqn = (cq * lax.rsqrt(ms + EPS) * gq_ref[...]).astype(BF16)
    qt = _dot_nt(wqt_ref[...], cqn) * (MLA_SCALE * LOG2E)
    cost = cost_ref[...]
    sint = sint_ref[...]
    for h in range(MLA_HEADS):
        b = 128 * h
        qt_out[b:b + 64, :] = qt[b:b + 64].astype(BF16)
        qt_out[b + 64:b + 96, :] = (qt[b + 64:b + 96] * cost + qt[b + 96:b + 128] * sint).astype(BF16)
        qt_out[b + 96:b + 128, :] = jnp.zeros((32, TM), BF16)

    ckvn = _rms(ckv_ref[...].astype(F32), gkv_ref[...]).astype(BF16)
    kn = _dot(ckvn, wkn_ref[...])
    kseg = kr_ref[...].astype(F32)
    krt = kseg * cos_ref[...] + pltpu.roll(kseg, 96, 1) * sin_ref[...]
    for h in range(MLA_HEADS):
        sl = slice(128 * h, 128 * (h + 1))
        k_out[:, sl] = (kn[:, sl] + krt).astype(BF16)

    vt = (_dot_nt(wvt_ref[...], ckvn) + ones_ref[...]).astype(BF16)
    for j in range(TM // KC):
        vt_out[j] = vt[:, KC * j:KC * (j + 1)]


def _mla_prep(proj, gq, gkv, wqt, wkn, wvt, cos_t, sin_t, cos_tt, sin_tt):
    r = proj.shape[0]
    nv = MLA_HEADS * VROWS
    ones_col = ((jnp.arange(nv) % VROWS) >= MLA_V).astype(F32).reshape(nv, 1)
    full = lambda shape: pl.BlockSpec(shape, lambda i: (0, 0))
    return pl.pallas_call(
        _mla_prep_kernel,
        grid=(r // TM,),
        in_specs=[pl.BlockSpec((TM, 512), lambda i: (i, OFF_CQ // 512)),
                  pl.BlockSpec((TM, 256), lambda i: (i, OFF_CKV // 256)),
                  pl.BlockSpec((TM, 128), lambda i: (i, OFF_KR // 128)),
                  full((1, 512)), full((1, 256)),
                  full((1024, 512)), full((256, 1024)), full((nv, 256)), full((nv, 1)),
                  pl.BlockSpec((TM, 128), lambda i: (i, 0)),
                  pl.BlockSpec((TM, 128), lambda i: (i, 0)),
                  pl.BlockSpec((32, TM), lambda i: (0, i)),
                  pl.BlockSpec((32, TM), lambda i: (0, i))],
        out_specs=[pl.BlockSpec((1024, TM), lambda i: (0, i)),
                   pl.BlockSpec((TM, 1024), lambda i: (i, 0)),
                   pl.BlockSpec((TM // KC, nv, KC), lambda i: (i, 0, 0))],
        out_shape=[jax.ShapeDtypeStruct((1024, r), BF16),
                   jax.ShapeDtypeStruct((r, 1024), BF16),
                   jax.ShapeDtypeStruct((r // KC, nv, KC), BF16)],
        compiler_params=_cparams(("parallel",)),
        name="mla_prep",
    )(proj, proj, proj, gq, gkv, wqt, wkn, wvt, ones_col, cos_t, sin_t, cos_tt, sin_tt)


def _alias(o_init):
    return ([], []) if o_init is None else ([pl.BlockSpec(memory_space=pl.ANY)], [o_init])


def _pick_heads(o0, o1):
    lane = lax.broadcasted_iota(jnp.int32, o0.shape, 1)
    return jnp.where(lane < 64, o0, o1)


def _mla_update(s, vt, m_old, acc_ref):
    m_new = jnp.maximum(m_old, jnp.max(s, axis=0, keepdims=True))
    alpha = jnp.exp2(m_old - m_new)
    p = jnp.exp2(s - m_new).astype(BF16)
    acc_ref[...] = alpha * acc_ref[...] + _dot(vt, p)
    return m_new


def _mla_finish(acc0, acc1):
    def rows_to_lanes(acc):
        return jnp.concatenate([acc, jnp.zeros((128 - VROWS, acc.shape[1]), F32)], axis=0).T
    a0 = rows_to_lanes(acc0)
    a1 = pltpu.roll(rows_to_lanes(acc1), 64, 1)
    return _pick_heads(a0 / a0[:, 64:65], a1 / a1[:, 0:1]).astype(BF16)


def _mla_attn_kernel(qt_ref, k_ref, vt_ref, kc_ref, vtc_ref, *rest, cpi, aliased):
    o_ref, *acc_refs = rest[1:] if aliased else rest
    nt = qt_ref.shape[1] // 256
    units = [(t, h) for h in range(2) for t in range(nt)]
    qts = [qt_ref[128 * h:128 * (h + 1), 256 * t:256 * (t + 1)] for t, h in units]
    n_chunks = k_ref.shape[0] // KC
    for acc_ref in acc_refs:
        acc_ref[...] = jnp.zeros_like(acc_ref)

    def scores(kget):
        return tuple(_dot(kget(h), qts[u]) for u, (t, h) in enumerate(units))

    def latent_keys(j):
        off = pl.multiple_of(j * KC, KC)
        return lambda h: k_ref[pl.ds(off, KC), 128 * h:128 * (h + 1)]

    def step(s_cur, vget, ms, kget_next):
        s_next, ms_new = [], []
        for u, (t, h) in enumerate(units):
            if kget_next is not None:
                s_next.append(_dot(kget_next(h), qts[u]))
            ms_new.append(_mla_update(s_cur[u], vget(h), ms[u], acc_refs[u]))
        return tuple(s_next), tuple(ms_new)

    ms = tuple(jnp.full((1, 256), -jnp.inf, F32) for _ in units)
    s_ctx = scores(lambda h: kc_ref[:, 128 * h:128 * (h + 1)])
    s_cur, ms = step(s_ctx, lambda h: vtc_ref[0, VROWS * h:VROWS * (h + 1), :], ms, latent_keys(0))

    def values(j):
        return lambda h: vt_ref[j, VROWS * h:VROWS * (h + 1), :]

    def body(i, carry):
        ms, s_cur = carry
        for c in range(cpi):
            j = i * cpi + c
            s_cur, ms = step(s_cur, values(j), ms, latent_keys(j + 1))
        return ms, s_cur

    n_iter = n_chunks // cpi - 1
    ms, s_cur = lax.fori_loop(0, n_iter, body, (ms, s_cur))
    for j in range(n_iter * cpi, n_chunks):
        s_cur, ms = step(s_cur, values(j), ms, latent_keys(j + 1) if j + 1 < n_chunks else None)
    for t in range(nt):
        o_ref[256 * t:256 * (t + 1), :] = _mla_finish(acc_refs[units.index((t, 0))][...],
                                                      acc_refs[units.index((t, 1))][...])


def _mla_attn(qt, k, vt, nb, s_len, c_len, o_init):
    tq = min(512, s_len)
    nq = s_len // tq
    t_rows = nb * s_len
    cpi = s_len // KC
    alias_spec, alias_arg = _alias(o_init)
    return pl.pallas_call(
        functools.partial(_mla_attn_kernel, cpi=cpi, aliased=o_init is not None),
        grid=(nb, 4, nq),
        in_specs=[pl.BlockSpec((256, tq), lambda b, hp, i: (hp, b * nq + i)),
                  pl.BlockSpec((s_len, 256), lambda b, hp, i: (b, hp)),
                  pl.BlockSpec((s_len // KC, 2 * VROWS, KC), lambda b, hp, i: (b, hp, 0)),
                  pl.BlockSpec((c_len, 256), lambda b, hp, i: (t_rows // c_len + b, hp)),
                  pl.BlockSpec((1, 2 * VROWS, KC), lambda b, hp, i: (t_rows // KC + b, hp, 0))] + alias_spec,
        out_specs=pl.BlockSpec((tq, 128), lambda b, hp, i: (b * nq + i, hp)),
        out_shape=jax.ShapeDtypeStruct((t_rows if o_init is None else o_init.shape[0], 512), BF16),
        input_output_aliases={} if o_init is None else {5: 0},
        scratch_shapes=[pltpu.VMEM((VROWS, 256), F32)] * (2 * (tq // 256)),
        compiler_params=_cparams(("parallel", "parallel", "arbitrary")),
        name="mla_attn",
    )(qt, k, vt, k, vt, *alias_arg)


def _mla_ctx_attn_kernel(qt_ref, kc_ref, vtc_ref, o_full_ref, o_ref):
    del o_full_ref
    outs = []
    for h in range(2):
        s = _dot(kc_ref[:, 128 * h:128 * (h + 1)], qt_ref[128 * h:128 * (h + 1), :])
        p = jnp.exp2(s - jnp.max(s, axis=0, keepdims=True)).astype(BF16)
        outs.append(_dot(vtc_ref[0, VROWS * h:VROWS * (h + 1), :], p))
    o_ref[...] = _mla_finish(outs[0], outs[1])


def _mla_ctx_attn(qt, k, vt, o_full, nb, t_rows, c_len):
    blk0 = t_rows // c_len
    return pl.pallas_call(
        _mla_ctx_attn_kernel,
        grid=(nb, 4),
        in_specs=[pl.BlockSpec((256, c_len), lambda b, hp: (hp, blk0 + b)),
                  pl.BlockSpec((c_len, 256), lambda b, hp: (blk0 + b, hp)),
                  pl.BlockSpec((1, 2 * VROWS, KC), lambda b, hp: (blk0 + b, hp, 0)),
                  pl.BlockSpec(memory_space=pl.ANY)],
        out_specs=pl.BlockSpec((c_len, 128), lambda b, hp: (blk0 + b, hp)),
        out_shape=jax.ShapeDtypeStruct(o_full.shape, BF16),
        input_output_aliases={3: 0},
        compiler_params=_cparams(("parallel", "parallel")),
        name="mla_ctx_attn",
    )(qt, k, vt, o_full)


def _ctx_attn_kernel(q_ref, k_ref, v_ref, o_full_ref, o_ref, *, dqk):
    del o_full_ref
    v = v_ref[...]
    outs = []
    for h in range(2):
        sl = slice(dqk * h, dqk * (h + 1))
        s = _dot_nt(q_ref[:, sl], k_ref[:, sl])
        m = jnp.max(s, axis=-1, keepdims=True)
        p = jnp.exp(s - m)
        l = jnp.sum(p, axis=-1, keepdims=True)
        outs.append(_dot(p.astype(BF16), v) / l)
    o_ref[...] = _pick_heads(outs[0], outs[1]).astype(BF16)


def _ctx_attn(q, k, v, o_full, nb, c_len, row_blk0, dqk, qcol, kcol, vcol, name):
    w = 2 * dqk
    return pl.pallas_call(
        functools.partial(_ctx_attn_kernel, dqk=dqk),
        grid=(nb, 4),
        in_specs=[pl.BlockSpec((c_len, w), lambda b, hp: (row_blk0 + b, qcol // w + hp)),
                  pl.BlockSpec((c_len, w), lambda b, hp: (row_blk0 + b, kcol // w + hp)),
                  pl.BlockSpec((c_len, 128), lambda b, hp: (row_blk0 + b, vcol // 128 + hp)),
                  pl.BlockSpec(memory_space=pl.ANY)],
        out_specs=pl.BlockSpec((c_len, 128), lambda b, hp: (row_blk0 + b, hp)),
        out_shape=jax.ShapeDtypeStruct(o_full.shape, BF16),
        input_output_aliases={3: 0},
        compiler_params=_cparams(("parallel", "parallel")),
        name=name,
    )(q, k, v, o_full)


def _na_fill_bias(tab_ref, bias_ref):
    lane = lax.broadcasted_iota(jnp.int32, (GRID_W, 2 * GRID_W), 1)
    neg = jnp.full((GRID_W, 2 * GRID_W), NEG, F32)
    for pat in range(3):
        for j in range(NA_G):
            kr_lo = (0, j, NA_G)[pat]
            for h in range(2):
                for kp in range(NA_WIN // 2):
                    halves = []
                    for kr in (2 * kp, 2 * kp + 1):
                        d_row = kr - NA_G * pat - j + NA_ROWS - 1
                        halves.append(tab_ref[h, d_row] if kr_lo <= kr < kr_lo + NA_ROWS else neg)
                    bias_ref[pat, h, GRID_W * j:GRID_W * (j + 1), 2 * GRID_W * kp:2 * GRID_W * (kp + 1)] = (
                        jnp.where(lane < GRID_W, halves[0], halves[1]))


def _na_attn_kernel(q_ref, k_ref, v_ref, kc_ref, vc_ref, tab_ref, *rest, rows, aliased):
    o_ref, bias_ref = rest[1:] if aliased else rest
    nq = NA_G * GRID_W
    nk = NA_WIN * GRID_W
    vc = vc_ref[...]
    _na_fill_bias(tab_ref, bias_ref)

    def body(g, carry):
        r0 = g * NA_G
        ws = jnp.clip(r0 - NA_ROWS // 2, 0, rows - NA_WIN)
        pat = lax.shift_right_logical(r0 - ws, 2)
        qoff = pl.multiple_of(r0 * GRID_W, nq)
        koff = pl.multiple_of(ws * GRID_W, GRID_W)
        q = q_ref[pl.ds(qoff, nq), :]
        kw = k_ref[pl.ds(koff, nk), :]
        vw = v_ref[pl.ds(koff, nk), :]
        scores = []
        for h in range(2):
            sl = slice(64 * h, 64 * (h + 1))
            scores.append((_dot_nt(q[:, sl], kw[:, sl]) + bias_ref[pat, h], _dot_nt(q[:, sl], kc_ref[:, sl])))
        outs = []
        for s, sc in scores:
            m = jnp.maximum(jnp.max(s, axis=-1, keepdims=True), jnp.max(sc, axis=-1, keepdims=True))
            p = jnp.exp(s - m)
            pc = jnp.exp(sc - m)
            l = jnp.sum(p, axis=-1, keepdims=True) + jnp.sum(pc, axis=-1, keepdims=True)
            outs.append((_dot(p.astype(BF16), vw) + _dot(pc.astype(BF16), vc)) / l)
        o_ref[pl.ds(qoff, nq), :] = _pick_heads(outs[0], outs[1]).astype(BF16)
        return carry

    lax.fori_loop(0, rows // NA_G, body, 0)


def _na_attn(proj, tab, nb, s_len, c_len, o_init):
    rows = s_len // GRID_W
    assert NA_G == 4 and rows % NA_G == 0 and rows >= NA_WIN and NA_WIN % 2 == 0
    ctx_blk = nb * s_len // c_len
    alias_spec, alias_arg = _alias(o_init)
    return pl.pallas_call(
        functools.partial(_na_attn_kernel, rows=rows, aliased=o_init is not None),
        grid=(nb, 4),
        in_specs=[pl.BlockSpec((s_len, 128), lambda b, hp: (b, OFF_QA // 128 + hp)),
                  pl.BlockSpec((s_len, 128), lambda b, hp: (b, OFF_KA // 128 + hp)),
                  pl.BlockSpec((s_len, 128), lambda b, hp: (b, OFF_VA // 128 + hp)),
                  pl.BlockSpec((c_len, 128), lambda b, hp: (ctx_blk + b, OFF_KA // 128 + hp)),
                  pl.BlockSpec((c_len, 128), lambda b, hp: (ctx_blk + b, OFF_VA // 128 + hp)),
                  pl.BlockSpec((2, 2 * NA_ROWS - 1, GRID_W, 2 * GRID_W), lambda b, hp: (hp, 0, 0, 0))] + alias_spec,
        out_specs=pl.BlockSpec((s_len, 128), lambda b, hp: (b, hp)),
        out_shape=jax.ShapeDtypeStruct((nb * s_len if o_init is None else o_init.shape[0], 512), BF16),
        input_output_aliases={} if o_init is None else {6: 0},
        scratch_shapes=[pltpu.VMEM((3, 2, NA_G * GRID_W, NA_WIN * GRID_W), F32)],
        compiler_params=_cparams(("parallel", "parallel")),
        name="na_attn",
    )(proj, proj, proj, proj, proj, tab, *alias_arg)


def _na_bias_table(rpb):
    col = jnp.arange(GRID_W)
    cstart = jnp.clip(col - NA_COLS // 2, 0, GRID_W - NA_COLS)
    kc = jnp.arange(GRID_W)
    valid_c = (kc[None, :] >= cstart[:, None]) & (kc[None, :] < cstart[:, None] + NA_COLS)
    dcol = jnp.clip(kc[None, :] - col[:, None] + (NA_COLS - 1), 0, 2 * NA_COLS - 2)
    tab = jnp.where(valid_c[None, None], rpb[:, :, dcol], NEG).astype(F32)
    return jnp.concatenate([tab, tab], axis=-1)


def _local_kernel(up_p, up_c, up_n, cg_p, cg_c, cg_n, xc_p, xc_c, xc_n, bg_ref,
                  pw_ref, ps_ref, cw_ref, ob_ref, od_ref, *, lat_tiles, tiles_lat_seq, s_len, c_len):
    i = pl.program_id(0)
    is_lat = i < lat_tiles
    j = jnp.where(is_lat, i % tiles_lat_seq, 0)
    n_seq = jnp.where(is_lat, s_len, c_len)
    first = j == 0
    last = (j + 1) * TL == n_seq

    def ext(p_ref, c_ref, n_ref):
        p = jnp.where(first, 0.0, p_ref[...].astype(F32))
        n = jnp.where(last, 0.0, n_ref[...].astype(F32))
        return jnp.concatenate([p, c_ref[...].astype(F32), n], axis=0)

    z = ext(cg_p, cg_c, cg_n) * ext(xc_p, xc_c, xc_n)
    cw = cw_ref[...]
    y = (cw[0:1] * z[HALO - 1:HALO - 1 + TL] + cw[1:2] * z[HALO:HALO + TL]
         + cw[2:3] * z[HALO + 1:HALO + 1 + TL])
    od_ref[...] = (bg_ref[...].astype(F32) * y).astype(BF16)

    u = ext(up_p, up_c, up_n)
    t = j * TL + lax.broadcasted_iota(jnp.int32, (TL, 1), 0)
    ps = ps_ref[...]
    for g, w in enumerate(POOL_WINDOWS):
        sl = slice(POOL_GROUP * g, POOL_GROUP * (g + 1))
        ug = u[:, sl]
        acc = ug[HALO - w // 2:HALO - w // 2 + TL]
        for d in range(-w // 2 + 1, w // 2):
            acc = acc + ug[HALO + d:HALO + d + TL]
        cnt = (jnp.minimum(t - w // 2 + w, n_seq) - jnp.maximum(t - w // 2, 0)).astype(F32)
        pooled = acc / cnt - ug[HALO:HALO + TL]
        mixed = _dot(pooled.astype(BF16), pw_ref[g])
        ob_ref[:, sl] = (mixed * ps[:, sl]).astype(BF16)


def _local(proj, pool_w, pool_scale, conv_w, nb, s_len, c_len):
    r = proj.shape[0]
    lat_tiles = nb * s_len // TL
    hpt = TL // HALO
    nhalo = r // HALO

    def cur(off):
        return pl.BlockSpec((TL, 512), lambda i: (i, off // 512))

    def prev(off):
        return pl.BlockSpec((HALO, 512), lambda i: (jnp.maximum(i * hpt - 1, 0), off // 512))

    def nxt(off):
        return pl.BlockSpec((HALO, 512), lambda i: (jnp.minimum((i + 1) * hpt, nhalo - 1), off // 512))

    specs = []
    for off in (OFF_UP, OFF_CG, OFF_XC):
        specs += [prev(off), cur(off), nxt(off)]
    specs += [cur(OFF_BG),
              pl.BlockSpec((4, POOL_GROUP, POOL_GROUP), lambda i: (0, 0, 0)),
              pl.BlockSpec((1, 512), lambda i: (0, 0)),
              pl.BlockSpec((CONV_K, 512), lambda i: (0, 0))]
    return pl.pallas_call(
        functools.partial(_local_kernel, lat_tiles=lat_tiles, tiles_lat_seq=s_len // TL,
                          s_len=s_len, c_len=c_len),
        grid=(r // TL,),
        in_specs=specs,
        out_specs=[pl.BlockSpec((TL, 512), lambda i: (i, 0)), pl.BlockSpec((TL, 512), lambda i: (i, 0))],
        out_shape=[jax.ShapeDtypeStruct((r, 512), BF16), jax.ShapeDtypeStruct((r, 512), BF16)],
        compiler_params=_cparams(("parallel",)),
        name="local_mix",
    )(*([proj] * 10), pool_w, pool_scale, conv_w)


def _merge_kernel(oa_ref, ob_ref, oc_ref, od_ref, gate_ref, wb_ref, wo_ref, x_ref, gm_ref, g_ref, o_ref):
    merged = None
    for k, o in enumerate((oa_ref, ob_ref, oc_ref, od_ref)):
        proj = _dot(o[...], wb_ref[k])
        gk = jax.nn.sigmoid(gate_ref[:, 1024 * k:1024 * (k + 1)].astype(F32))
        merged = gk * proj if merged is None else merged + gk * proj
    y = _dot(merged.astype(BF16), wo_ref[...])
    o_ref[...] = x_ref[...] + gm_ref[...] * _rms(y, g_ref[...])


def _merge(oa, ob, oc, od, proj, wb, wo, xall, gate_mod, g, n_rows, tiles_per_batch):
    d = xall.shape[1]
    tm = TM
    tpb = tiles_per_batch * (TM // tm)
    row = lambda w: pl.BlockSpec((tm, w), lambda i: (i, 0))
    return pl.pallas_call(
        _merge_kernel,
        grid=(n_rows // tm,),
        in_specs=[row(512), row(512), row(512), row(512), row(4096),
                  pl.BlockSpec((4, 512, d), lambda i: (0, 0, 0)),
                  pl.BlockSpec((d, d), lambda i: (0, 0)),
                  row(d),
                  pl.BlockSpec((None, 1, d), lambda i: (i // tpb, 0, 0)),
                  pl.BlockSpec((1, d), lambda i: (0, 0))],
        out_specs=row(d),
        out_shape=jax.ShapeDtypeStruct((n_rows, d), F32),
        compiler_params=_cparams(("parallel",)),
        name="merge",
    )(oa, ob, oc, od, proj, wb, wo, xall, gate_mod, g.reshape(1, d))


def _ffn_kernel(x_ref, g2_ref, sh_ref, sc_ref, gm_ref, g3_ref, w1_ref, w3_ref, w2_ref, o_ref, *, n_split):
    x = x_ref[...]
    h = (_rms(x, g2_ref[...]) * (1.0 + sc_ref[...]) + sh_ref[...]).astype(BF16)
    tf = w1_ref.shape[1] // n_split
    acc = None
    for j in range(n_split):
        sl = slice(tf * j, tf * (j + 1))
        a = _dot(h, w1_ref[:, sl])
        b = _dot(h, w3_ref[:, sl])
        y = _dot((a * jax.nn.sigmoid(a) * b).astype(BF16), w2_ref[sl, :])
        acc = y if acc is None else acc + y
    o_ref[...] = x + gm_ref[...] * _rms(acc, g3_ref[...])


def _ffn(xall, g2, shift, scale, gate_mod, g3, w1, w3, w2, tiles_per_batch):
    r, d = xall.shape
    dff = w1.shape[1]
    mod = pl.BlockSpec((None, 1, d), lambda i: (i // tiles_per_batch, 0, 0))
    vec = pl.BlockSpec((1, d), lambda i: (0, 0))
    resident = lambda shape: pl.BlockSpec(shape, lambda i: (0, 0), pipeline_mode=pl.Buffered(1))
    return pl.pallas_call(
        functools.partial(_ffn_kernel, n_split=2),
        grid=(r // TM,),
        in_specs=[pl.BlockSpec((TM, d), lambda i: (i, 0)), vec, mod, mod, mod, vec,
                  resident((d, dff)), resident((d, dff)), resident((dff, d))],
        out_specs=pl.BlockSpec((TM, d), lambda i: (i, 0)),
        out_shape=jax.ShapeDtypeStruct((r, d), F32),
        compiler_params=_cparams(("parallel",)),
        name="ffn_dense",
    )(xall, g2.reshape(1, d), shift, scale, gate_mod, g3.reshape(1, d), w1, w3, w2)


def _router_kernel(x_ref, g2_ref, sh_ref, sc_ref, rt_ref, t_ref, idx_ref, w_ref):
    t = _rms(x_ref[...], g2_ref[...]) * (1.0 + sc_ref[...]) + sh_ref[...]
    t_ref[...] = t
    logits = lax.dot_general(rt_ref[...], t, (((1,), (1,)), ((), ())), preferred_element_type=F32,
                             precision=lax.Precision.HIGHEST)
    e = lax.broadcasted_iota(jnp.int32, logits.shape, 0).astype(F32)
    m1 = jnp.max(logits, axis=0, keepdims=True)
    i1 = jnp.min(jnp.where(logits == m1, e, float(N_EXPERTS)), axis=0, keepdims=True)
    rest = jnp.where(e == i1, -jnp.inf, logits)
    m2 = jnp.max(rest, axis=0, keepdims=True)
    i2 = jnp.min(jnp.where(rest == m2, e, float(N_EXPERTS)), axis=0, keepdims=True)
    ex = jnp.exp(m2 - m1)
    w1 = 1.0 / (1.0 + ex)
    idx_ref[0:1, :] = i1.astype(jnp.int32)
    idx_ref[1:2, :] = i2.astype(jnp.int32)
    w_ref[0:1, :] = w1
    w_ref[1:2, :] = ex * w1


def _router(x, g2, shift, scale, router_t, tiles_per_batch):
    t_rows, d = x.shape
    mod = pl.BlockSpec((None, 1, d), lambda i: (i // tiles_per_batch, 0, 0))
    return pl.pallas_call(
        _router_kernel,
        grid=(t_rows // TM,),
        in_specs=[pl.BlockSpec((TM, d), lambda i: (i, 0)),
                  pl.BlockSpec((1, d), lambda i: (0, 0)), mod, mod,
                  pl.BlockSpec((N_EXPERTS, d), lambda i: (0, 0))],
        out_specs=[pl.BlockSpec((TM, d), lambda i: (i, 0)),
                   pl.BlockSpec((TOP_K, TM), lambda i: (0, i)),
                   pl.BlockSpec((TOP_K, TM), lambda i: (0, i))],
        out_shape=[jax.ShapeDtypeStruct((t_rows, d), F32),
                   jax.ShapeDtypeStruct((TOP_K, t_rows), jnp.int32),
                   jax.ShapeDtypeStruct((TOP_K, t_rows), F32)],
        compiler_params=_cparams(("parallel",)),
        name="router",
    )(x, g2.reshape(1, d), shift, scale, router_t)


def _row_gather_start(ids_ref, src_ref, buf_ref, sem, n):
    def issue(r, c):
        pltpu.make_async_copy(src_ref.at[pl.ds(ids_ref[0, r], 1)], buf_ref.at[pl.ds(r, 1)], sem).start()
        return c

    lax.fori_loop(0, n, issue, 0, unroll=8)


def _row_gather_wait(src_ref, buf_ref, sem, n):
    pltpu.make_async_copy(src_ref.at[pl.ds(0, n)], buf_ref, sem).wait()


def _moe_ffn_kernel(eid_ref, nused_ref, ids_ref, ids_next_ref, t_ref, w1_ref, w3_ref, w2_ref, o_ref,
                    xbuf_ref, xs_ref, acc_ref, sems):
    i = pl.program_id(0)
    f = pl.program_id(1)
    n_f = pl.num_programs(1)
    slot = lax.rem(i, 2)

    def row_copy(ids, r, s):
        return pltpu.make_async_copy(t_ref.at[pl.ds(ids[0, r], 1)], xbuf_ref.at[s, pl.ds(r, 1)], sems.at[s])

    def all_rows(s):
        return pltpu.make_async_copy(t_ref.at[pl.ds(0, TM)], xbuf_ref.at[s], sems.at[s])

    @pl.when((i == 0) & (f == 0))
    def _():
        def issue(r, c):
            row_copy(ids_ref, r, 0).start()
            return c

        lax.fori_loop(0, TM, issue, 0)

    @pl.when(i < nused_ref[0])
    def _():
        @pl.when(f == 0)
        def _():
            all_rows(slot).wait()
            xs_ref[...] = xbuf_ref[slot].astype(BF16)

        for step in range(MOE_F_STEPS):
            @pl.when(f == step)
            def _():
                for r in range(step * (TM // MOE_F_STEPS), (step + 1) * (TM // MOE_F_STEPS)):
                    row_copy(ids_next_ref, r, 1 - slot).start()

        x = xs_ref[...]
        a = _dot(x, w1_ref[...])
        b = _dot(x, w3_ref[...])
        y = _dot((a * jax.nn.sigmoid(a) * b).astype(BF16), w2_ref[...])

        @pl.when(f == 0)
        def _():
            acc_ref[...] = y

        @pl.when(f == n_f - 1)
        def _():
            o_ref[...] = acc_ref[...] + y

    @pl.when((i + 1 == nused_ref[0]) & (f == n_f - 1))
    def _():
        all_rows(1 - slot).wait()

    @pl.when((i >= nused_ref[0]) & (f == n_f - 1))
    def _():
        o_ref[...] = jnp.zeros_like(o_ref)


def _moe_ffn(t, slot_tok, tile_eid, n_used, w1, w3, w2):
    n_tiles = slot_tok.shape[0]
    d = t.shape[1]
    dff = w1.shape[2]
    tf = dff // MOE_F_STEPS
    assert MOE_F_STEPS == 2, "the kernel stores the first step's partial sum and adds it in the second"
    wf = lambda i, f, nu: jnp.where(i < nu[0], f, MOE_F_STEPS - 1)
    ids = lambda nxt: pl.BlockSpec((None, 1, TM),
                                   lambda i, f, eid, nu: (jnp.minimum(i + nxt, n_tiles - 1), 0, 0),
                                   memory_space=pltpu.SMEM)
    grid_spec = pltpu.PrefetchScalarGridSpec(
        num_scalar_prefetch=2,
        grid=(n_tiles, MOE_F_STEPS),
        in_specs=[ids(0), ids(1),
                  pl.BlockSpec(memory_space=pl.ANY),
                  pl.BlockSpec((None, d, tf), lambda i, f, eid, nu: (eid[i], 0, wf(i, f, nu))),
                  pl.BlockSpec((None, d, tf), lambda i, f, eid, nu: (eid[i], 0, wf(i, f, nu))),
                  pl.BlockSpec((None, tf, d), lambda i, f, eid, nu: (eid[i], wf(i, f, nu), 0))],
        out_specs=pl.BlockSpec((TM, d), lambda i, f, eid, nu: (i, 0)),
        scratch_shapes=[pltpu.VMEM((2, TM, d), F32), pltpu.VMEM((TM, d), BF16), pltpu.VMEM((TM, d), F32),
                        pltpu.SemaphoreType.DMA((2,))])
    return pl.pallas_call(
        _moe_ffn_kernel,
        grid_spec=grid_spec,
        out_shape=jax.ShapeDtypeStruct((n_tiles * TM, d), F32),
        compiler_params=_cparams(("arbitrary", "arbitrary")),
        name="moe_ffn",
    )(tile_eid, n_used, slot_tok, slot_tok, t, w1, w3, w2)


def _combine_kernel(p0_ref, p1_ref, p0n_ref, p1n_ref, ys_ref, w_ref, x_ref, gm_ref, g3_ref, o_ref,
                    b0_ref, b1_ref, sems):
    i = pl.program_id(0)
    slot = lax.rem(i, 2)

    def start(s, ids0, ids1):
        _row_gather_start(ids0, ys_ref, b0_ref.at[s], sems.at[0, s], TM)
        _row_gather_start(ids1, ys_ref, b1_ref.at[s], sems.at[1, s], TM)

    def wait(s):
        _row_gather_wait(ys_ref, b0_ref.at[s], sems.at[0, s], TM)
        _row_gather_wait(ys_ref, b1_ref.at[s], sems.at[1, s], TM)

    @pl.when(i == 0)
    def _():
        start(0, p0_ref, p1_ref)

    start(1 - slot, p0n_ref, p1n_ref)
    wait(slot)
    w = w_ref[...]
    y = w[:, 0:1] * b0_ref[slot] + w[:, 1:2] * b1_ref[slot]
    o_ref[...] = x_ref[...] + gm_ref[...] * _rms(y, g3_ref[...])

    @pl.when(i == pl.num_programs(0) - 1)
    def _():
        wait(1 - slot)


def _combine(pos0, pos1, ys, wcol, x, gate_mod, g3, tiles_per_batch):
    t_rows, d = x.shape
    n_tiles = t_rows // TM
    ids = lambda nxt: pl.BlockSpec((None, 1, TM), lambda i: (jnp.minimum(i + nxt, n_tiles - 1), 0, 0),
                                   memory_space=pltpu.SMEM)
    return pl.pallas_call(
        _combine_kernel,
        grid=(n_tiles,),
        in_specs=[ids(0), ids(0), ids(1), ids(1),
                  pl.BlockSpec(memory_space=pl.ANY),
                  pl.BlockSpec((TM, TOP_K), lambda i: (i, 0)),
                  pl.BlockSpec((TM, d), lambda i: (i, 0)),
                  pl.BlockSpec((None, 1, d), lambda i: (i // tiles_per_batch, 0, 0)),
                  pl.BlockSpec((1, d), lambda i: (0, 0))],
        out_specs=pl.BlockSpec((TM, d), lambda i: (i, 0)),
        out_shape=jax.ShapeDtypeStruct((t_rows, d), F32),
        scratch_shapes=[pltpu.VMEM((2, TM, d), F32), pltpu.VMEM((2, TM, d), F32),
                        pltpu.SemaphoreType.DMA((2, 2))],
        compiler_params=_cparams(("arbitrary",)),
        name="moe_combine",
    )(pos0, pos1, pos0, pos1, ys, wcol, x, gate_mod, g3.reshape(1, d))


def _routing_tables(top_i, t_rows):
    n_assign = TOP_K * t_rows
    n_tiles = n_assign // TM + N_EXPERTS
    e_flat = top_i.reshape(n_assign)
    onehot = (e_flat[:, None] == jnp.arange(N_EXPERTS)[None, :]).astype(jnp.int32)
    csum = jnp.cumsum(onehot, axis=0)
    counts = csum[-1]
    rank = jnp.sum((csum - onehot) * onehot, axis=1)
    tiles_e = (counts + TM - 1) // TM
    tile_end = jnp.cumsum(tiles_e)
    tile_start = tile_end - tiles_e
    slot = (tile_start * TM)[e_flat] + rank
    tok = jnp.tile(jnp.arange(t_rows, dtype=jnp.int32), TOP_K)
    slot_tok = jnp.zeros((n_tiles * TM,), jnp.int32).at[slot].set(tok)
    n_used = tile_end[-1]
    tile_ids = jnp.arange(n_tiles)
    tile_eid = jnp.sum(tile_ids[:, None] >= tile_end[None, :], axis=1)
    last_eid = jnp.sum(n_used - 1 >= tile_end)
    tile_eid = jnp.where(tile_ids < n_used, tile_eid, last_eid).astype(jnp.int32)
    pos = slot.reshape(TOP_K, t_rows).astype(jnp.int32)
    return (slot_tok.reshape(n_tiles, 1, TM), tile_eid, n_used.reshape(1).astype(jnp.int32),
            pos[0].reshape(t_rows // TM, 1, TM), pos[1].reshape(t_rows // TM, 1, TM))


def _prep_w_in(w):
    d = w.shape[0]
    q_a, k_a, v_a, u_p, c_q, c_kv, k_r, b_g, c_g, x_c, gate = jnp.split(w, IN_SPLITS, axis=1)
    z = lambda n: jnp.zeros((d, n), w.dtype)
    kr_e, kr_o = k_r[:, 0::2], k_r[:, 1::2]
    cols = [gate, q_a * NA_SCALE, k_a, v_a, u_p, b_g, c_g, x_c, c_q, z(128), c_kv,
            z(64), kr_e, kr_o, -kr_o, kr_e, z(128)]
    return jnp.concatenate(cols, axis=1).astype(BF16)


def _prep_w_uq(w_uq):
    wq = w_uq.reshape(MLA_Q_LORA, MLA_HEADS, MLA_NOPE + MLA_ROPE)
    nope, r = wq[..., :MLA_NOPE], wq[..., MLA_NOPE:]
    re, ro = r[..., 0::2], r[..., 1::2]
    ext = jnp.concatenate([nope, re, ro, -ro, re], axis=-1).reshape(MLA_Q_LORA, MLA_HEADS * 128)
    return jnp.pad(ext, ((0, 512 - MLA_Q_LORA), (0, 0))).T.astype(BF16)


def _prep_w_ukv(w_ukv):
    wkv = w_ukv.reshape(MLA_KV_LORA, MLA_HEADS, MLA_NOPE + MLA_V)
    zeros = jnp.zeros((MLA_KV_LORA, MLA_HEADS, 64), w_ukv.dtype)
    kn = jnp.concatenate([wkv[..., :MLA_NOPE], zeros], axis=-1).reshape(MLA_KV_LORA, MLA_HEADS * 128)
    vt = jnp.concatenate([wkv[..., MLA_NOPE:], zeros[..., :VROWS - MLA_V]], axis=-1)
    return kn.astype(BF16), vt.reshape(MLA_KV_LORA, MLA_HEADS * VROWS).T.astype(BF16)


def _rope_tables(nb, s_len, n_ctx_rows):
    pos = jnp.arange(s_len)
    row = (pos // GRID_W).astype(F32)
    col = (pos % GRID_W).astype(F32)
    n_pairs = MLA_ROPE // 4
    inv_freq = ROPE_BASE ** (-jnp.arange(n_pairs, dtype=F32) / n_pairs)
    ang = jnp.concatenate([row[:, None] * inv_freq, col[:, None] * inv_freq], axis=-1)
    cos = jnp.concatenate([jnp.tile(jnp.cos(ang), (nb, 1)), jnp.ones((n_ctx_rows, 16), F32)], axis=0)
    sin = jnp.concatenate([jnp.tile(jnp.sin(ang), (nb, 1)), jnp.zeros((n_ctx_rows, 16), F32)], axis=0)
    r = cos.shape[0]
    cos2 = jnp.concatenate([cos, cos], axis=1)
    sin2 = jnp.concatenate([sin, sin], axis=1)
    pad = lambda t, lead: jnp.concatenate([lead, t, jnp.zeros((r, 32), F32)], axis=1)
    return (pad(cos2, jnp.ones((r, 64), F32)), pad(sin2, jnp.zeros((r, 64), F32)), cos2.T, sin2.T)


def kernel(x, c, ctx, c_ctx, w_ada, b_ada, g_norm, w_in, na_rpb, pool_w, pool_scale, mla_g_q, mla_w_uq,
           mla_g_kv, mla_w_ukv, conv_w, w_branch, w_out, ffn_w1, ffn_w3, ffn_w2, moe_router, moe_w1,
           moe_w3, moe_w2):
    nb, s_len, d = x.shape
    c_len = ctx.shape[1]
    depth = w_in.shape[0]
    t_rows = nb * s_len
    n_ctx = nb * c_len
    assert s_len % TM == 0 and n_ctx == TM and s_len % GRID_W == 0 and t_rows % c_len == 0
    assert c_len == KC and c_len == TL
    tpb = s_len // TM
    rows = s_len // GRID_W

    xall = jnp.concatenate([x.reshape(t_rows, d), ctx.reshape(n_ctx, d)], axis=0)
    c8 = jnp.zeros((8, d), F32).at[:nb].set(c).at[nb].set(c_ctx)
    cos_t, sin_t, cos_tt, sin_tt = _rope_tables(nb, s_len, n_ctx)

    for l in range(depth):
        last = l == depth - 1
        mod = _adaln(c8, w_ada[l], b_ada[l]).reshape(8, 6, 1, d)
        mods = [mod[:, k] for k in range(6)]

        proj = _inproj(xall, g_norm[l, 0], mods[0], mods[1], _prep_w_in(w_in[l]), tpb)

        wkn, wvt = _prep_w_ukv(mla_w_ukv[l])
        gq = jnp.pad(mla_g_q[l], (0, 512 - MLA_Q_LORA)).reshape(1, 512)
        qt_m, k_m, vt_m = _mla_prep(proj, gq, mla_g_kv[l].reshape(1, MLA_KV_LORA), _prep_w_uq(mla_w_uq[l]),
                                    wkn, wvt, cos_t, sin_t, cos_tt, sin_tt)
        n_rows = t_rows if last else t_rows + n_ctx
        o_init = None if last else jnp.zeros((n_rows, 512), BF16)
        o_c = _mla_attn(qt_m, k_m, vt_m, nb, s_len, c_len, o_init)
        o_a = _na_attn(proj, _na_bias_table(na_rpb[l]), nb, s_len, c_len, o_init)
        o_b, o_d = _local(proj, pool_w[l].astype(BF16), pool_scale[l].reshape(1, 512), conv_w[l],
                          nb, s_len, c_len)

        wb = w_branch[l].astype(BF16)
        wo = w_out[l].astype(BF16)
        if not last:
            ctx_blk = t_rows // c_len
            o_c = _mla_ctx_attn(qt_m, k_m, vt_m, o_c, nb, t_rows, c_len)
            o_a = _ctx_attn(proj, proj, proj, o_a, nb, c_len, ctx_blk, 64, OFF_QA, OFF_KA, OFF_VA, "na_ctx_attn")
        xall = _merge(o_a, o_b, o_c, o_d, proj, wb, wo, xall, mods[2], g_norm[l, 1], n_rows, tpb)

        if l % 2 == 0:
            j = l // 2
            xall = _ffn(xall, g_norm[l, 2], mods[3], mods[4], mods[5], g_norm[l, 3],
                        ffn_w1[j].astype(BF16), ffn_w3[j].astype(BF16), ffn_w2[j].astype(BF16), tpb)
        else:
            j = l // 2
            assert last, "context rows are not routed through the experts"
            t_f32, top_i, top_w = _router(xall, g_norm[l, 2], mods[3], mods[4], moe_router[j].T, tpb)
            slot_tok, tile_eid, n_used, pos0, pos1 = _routing_tables(top_i, t_rows)
            ys = _moe_ffn(t_f32, slot_tok, tile_eid, n_used, moe_w1[j].astype(BF16), moe_w3[j].astype(BF16),
                          moe_w2[j].astype(BF16))
            xall = _combine(pos0, pos1, ys, top_w.T, xall, mods[5], g_norm[l, 3], tpb)

    return xall[:t_rows].reshape(nb, s_len, d)
```

```python
import functools

import numpy as np
import jax
import jax.numpy as jnp
from jax import lax
from jax.experimental import pallas as pl
from jax.experimental.pallas import tpu as pltpu

F32 = jnp.float32
BF16 = jnp.bfloat16

GRID_W = 64
N_BRANCH = 4
BRANCH_W = 512
EPS = 1e-6
NA_HEADS = 8
NA_HEAD_DIM = 64
NA_ROWS = 8
NA_COLS = 16
NA_G = 4
NA_WIN = NA_G + NA_ROWS
POOL_WINDOWS = (2, 4, 8, 16)
POOL_GROUP = 128
MLA_HEADS = 8
MLA_Q_LORA = 384
MLA_KV_LORA = 256
MLA_NOPE = 64
MLA_ROPE = 32
MLA_V = 64
ROPE_BASE = 10000.0
CONV_K = 3
N_EXPERTS = 8
TOP_K = 2

IN_SIZES = (512, 512, 512, 512, MLA_Q_LORA, MLA_KV_LORA, MLA_ROPE, 512, 512, 512, 4096)
IN_SPLITS = tuple(int(s) for s in np.cumsum(IN_SIZES)[:-1])

OFF_GATE = 0
OFF_QA = 4096
OFF_KA = 4608
OFF_VA = 5120
OFF_UP = 5632
OFF_BG = 6144
OFF_CG = 6656
OFF_XC = 7168
OFF_CQ = 7680
OFF_CKV = 8192
OFF_KR = 8448
N_IN = 8704

NA_SCALE = NA_HEAD_DIM ** -0.5
MLA_SCALE = (MLA_NOPE + MLA_ROPE) ** -0.5
LOG2E = 1.4426950408889634
NEG = -1e30

VMEM_LIMIT = 52 * 1024 * 1024
TM = 512
TL = 256
HALO = 16
KC = 256
VROWS = 80
MOE_F_STEPS = 2


def _cparams(sem):
    return pltpu.CompilerParams(dimension_semantics=sem, vmem_limit_bytes=VMEM_LIMIT)


def _rms(xf, g):
    ms = jnp.mean(xf * xf, axis=-1, keepdims=True)
    return xf * lax.rsqrt(ms + EPS) * g


def _dot(a, b):
    return jnp.dot(a, b, preferred_element_type=F32)


def _dot_nt(a, b):
    return lax.dot_general(a, b, (((1,), (1,)), ((), ())), preferred_element_type=F32)


def _adaln_kernel(c_ref, w_ref, b_ref, o_ref):
    c = c_ref[...]
    sc = c * jax.nn.sigmoid(c)
    o_ref[...] = jnp.dot(sc, w_ref[...], preferred_element_type=F32,
                         precision=lax.Precision.HIGHEST) + b_ref[...]


def _adaln(c8, w, b):
    d = c8.shape[1]
    n = w.shape[1]
    tn = 1536
    return pl.pallas_call(
        _adaln_kernel,
        grid=(n // tn,),
        in_specs=[pl.BlockSpec((8, d), lambda j: (0, 0)),
                  pl.BlockSpec((d, tn), lambda j: (0, j)),
                  pl.BlockSpec((1, tn), lambda j: (0, j))],
        out_specs=pl.BlockSpec((8, tn), lambda j: (0, j)),
        out_shape=jax.ShapeDtypeStruct((8, n), F32),
        compiler_params=_cparams(("arbitrary",)),
        name="adaln",
    )(c8, w, b.reshape(1, n))


def _inproj_kernel(x_ref, g_ref, sh_ref, sc_ref, w_ref, o_ref, *, n_split):
    h = (_rms(x_ref[...], g_ref[...]) * (1.0 + sc_ref[...]) + sh_ref[...]).astype(BF16)
    tn = w_ref.shape[1] // n_split
    for j in range(n_split):
        o_ref[:, tn * j:tn * (j + 1)] = _dot(h, w_ref[:, tn * j:tn * (j + 1)]).astype(BF16)


def _inproj(xall, g, shift, scale, w_p, tiles_per_batch):
    r, d = xall.shape
    n = w_p.shape[1]
    mod_spec = pl.BlockSpec((None, 1, d), lambda i: (i // tiles_per_batch, 0, 0))
    return pl.pallas_call(
        functools.partial(_inproj_kernel, n_split=4),
        grid=(r // TM,),
        in_specs=[pl.BlockSpec((TM, d), lambda i: (i, 0)),
                  pl.BlockSpec((1, d), lambda i: (0, 0)),
                  mod_spec, mod_spec,
                  pl.BlockSpec((d, n), lambda i: (0, 0), pipeline_mode=pl.Buffered(1))],
        out_specs=pl.BlockSpec((TM, n), lambda i: (i, 0)),
        out_shape=jax.ShapeDtypeStruct((r, n), BF16),
        compiler_params=_cparams(("parallel",)),
        name="inproj",
    )(xall, g.reshape(1, d), shift, scale, w_p)


def _mla_prep_kernel(cq_ref, ckv_ref, kr_ref, gq_ref, gkv_ref, wqt_ref, wkn_ref, wvt_ref, ones_ref,
                     cos_ref, sin_ref, cost_ref, sint_ref, qt_out, k_out, vt_out):
    cq = cq_ref[...].astype(F32)
    ms = jnp.sum(cq * cq, axis=-1, keepdims=True) * (1.0 / MLA_Q_LORA)
    cqn = (cq * lax.rsqrt(ms + EPS) * gq_ref[...]).astype(BF16)
    qt = _dot_nt(wqt_ref[...], cqn) * (MLA_SCALE * LOG2E)
    cost = cost_ref[...]
    sint = sint_ref[...]
    for h in range(MLA_HEADS):
        b = 128 * h
        qt_out[b:b + 64, :] = qt[b:b + 64].astype(BF16)
        qt_out[b + 64:b + 96, :] = (qt[b + 64:b + 96] * cost + qt[b + 96:b + 128] * sint).astype(BF16)
        qt_out[b + 96:b + 128, :] = jnp.zeros((32, TM), BF16)

    ckvn = _rms(ckv_ref[...].astype(F32), gkv_ref[...]).astype(BF16)
    kn = _dot(ckvn, wkn_ref[...])
    kseg = kr_ref[...].astype(F32)
    krt = kseg * cos_ref[...] + pltpu.roll(kseg, 96, 1) * sin_ref[...]
    for h in range(MLA_HEADS):
        sl = slice(128 * h, 128 * (h + 1))
        k_out[:, sl] = (kn[:, sl] + krt).astype(BF16)

    vt = (_dot_nt(wvt_ref[...], ckvn) + ones_ref[...]).astype(BF16)
    for j in range(TM // KC):
        vt_out[j] = vt[:, KC * j:KC * (j + 1)]


def _mla_prep(proj, gq, gkv, wqt, wkn, wvt, cos_t, sin_t, cos_tt, sin_tt):
    r = proj.shape[0]
    nv = MLA_HEADS * VROWS
    ones_col = ((jnp.arange(nv) % VROWS) >= MLA_V).astype(F32).reshape(nv, 1)
    full = lambda shape: pl.BlockSpec(shape, lambda i: (0, 0))
    return pl.pallas_call(
        _mla_prep_kernel,
        grid=(r // TM,),
        in_specs=[pl.BlockSpec((TM, 512), lambda i: (i, OFF_CQ // 512)),
                  pl.BlockSpec((TM, 256), lambda i: (i, OFF_CKV // 256)),
                  pl.BlockSpec((TM, 128), lambda i: (i, OFF_KR // 128)),
                  full((1, 512)), full((1, 256)),
                  full((1024, 512)), full((256, 1024)), full((nv, 256)), full((nv, 1)),
                  pl.BlockSpec((TM, 128), lambda i: (i, 0)),
                  pl.BlockSpec((TM, 128), lambda i: (i, 0)),
                  pl.BlockSpec((32, TM), lambda i: (0, i)),
                  pl.BlockSpec((32, TM), lambda i: (0, i))],
        out_specs=[pl.BlockSpec((1024, TM), lambda i: (0, i)),
                   pl.BlockSpec((TM, 1024), lambda i: (i, 0)),
                   pl.BlockSpec((TM // KC, nv, KC), lambda i: (i, 0, 0))],
        out_shape=[jax.ShapeDtypeStruct((1024, r), BF16),
                   jax.ShapeDtypeStruct((r, 1024), BF16),
                   jax.ShapeDtypeStruct((r // KC, nv, KC), BF16)],
        compiler_params=_cparams(("parallel",)),
        name="mla_prep",
    )(proj, proj, proj, gq, gkv, wqt, wkn, wvt, ones_col, cos_t, sin_t, cos_tt, sin_tt)


def _alias(o_init):
    return ([], []) if o_init is None else ([pl.BlockSpec(memory_space=pl.ANY)], [o_init])


def _pick_heads(o0, o1):
    lane = lax.broadcasted_iota(jnp.int32, o0.shape, 1)
    return jnp.where(lane < 64, o0, o1)


def _mla_update(s, vt, m_old, acc_ref):
    m_new = jnp.maximum(m_old, jnp.max(s, axis=0, keepdims=True))
    alpha = jnp.exp2(m_old - m_new)
    p = jnp.exp2(s - m_new).astype(BF16)
    acc_ref[...] = alpha * acc_ref[...] + _dot(vt, p)
    return m_new


def _mla_finish(acc0, acc1):
    def rows_to_lanes(acc):
        return jnp.concatenate([acc, jnp.zeros((128 - VROWS, acc.shape[1]), F32)], axis=0).T
    a0 = rows_to_lanes(acc0)
    a1 = pltpu.roll(rows_to_lanes(acc1), 64, 1)
    return _pick_heads(a0 / a0[:, 64:65], a1 / a1[:, 0:1]).astype(BF16)


def _mla_attn_kernel(qt_ref, k_ref, vt_ref, kc_ref, vtc_ref, *rest, cpi, aliased):
    o_ref, *acc_refs = rest[1:] if aliased else rest
    nt = qt_ref.shape[1] // 256
    units = [(t, h) for h in range(2) for t in range(nt)]
    qts = [qt_ref[128 * h:128 * (h + 1), 256 * t:256 * (t + 1)] for t, h in units]
    n_chunks = k_ref.shape[0] // KC
    for acc_ref in acc_refs:
        acc_ref[...] = jnp.zeros_like(acc_ref)

    def scores(kget):
        return tuple(_dot(kget(h), qts[u]) for u, (t, h) in enumerate(units))

    def latent_keys(j):
        off = pl.multiple_of(j * KC, KC)
        return lambda h: k_ref[pl.ds(off, KC), 128 * h:128 * (h + 1)]

    def step(s_cur, vget, ms, kget_next):
        s_next, ms_new = [], []
        for u, (t, h) in enumerate(units):
            if kget_next is not None:
                s_next.append(_dot(kget_next(h), qts[u]))
            ms_new.append(_mla_update(s_cur[u], vget(h), ms[u], acc_refs[u]))
        return tuple(s_next), tuple(ms_new)

    ms = tuple(jnp.full((1, 256), -jnp.inf, F32) for _ in units)
    s_ctx = scores(lambda h: kc_ref[:, 128 * h:128 * (h + 1)])
    s_cur, ms = step(s_ctx, lambda h: vtc_ref[0, VROWS * h:VROWS * (h + 1), :], ms, latent_keys(0))

    def values(j):
        return lambda h: vt_ref[j, VROWS * h:VROWS * (h + 1), :]

    def body(i, carry):
        ms, s_cur = carry
        for c in range(cpi):
            j = i * cpi + c
            s_cur, ms = step(s_cur, values(j), ms, latent_keys(j + 1))
        return ms, s_cur

    n_iter = n_chunks // cpi - 1
    ms, s_cur = lax.fori_loop(0, n_iter, body, (ms, s_cur))
    for j in range(n_iter * cpi, n_chunks):
        s_cur, ms = step(s_cur, values(j), ms, latent_keys(j + 1) if j + 1 < n_chunks else None)
    for t in range(nt):
        o_ref[256 * t:256 * (t + 1), :] = _mla_finish(acc_refs[units.index((t, 0))][...],
                                                      acc_refs[units.index((t, 1))][...])


def _mla_attn(qt, k, vt, nb, s_len, c_len, o_init):
    tq = min(1024, s_len)
    nq = s_len // tq
    t_rows = nb * s_len
    cpi = s_len // KC
    alias_spec, alias_arg = _alias(o_init)
    return pl.pallas_call(
        functools.partial(_mla_attn_kernel, cpi=cpi, aliased=o_init is not None),
        grid=(nb, 4, nq),
        in_specs=[pl.BlockSpec((256, tq), lambda b, hp, i: (hp, b * nq + i)),
                  pl.BlockSpec((s_len, 256), lambda b, hp, i: (b, hp)),
                  pl.BlockSpec((s_len // KC, 2 * VROWS, KC), lambda b, hp, i: (b, hp, 0)),
                  pl.BlockSpec((c_len, 256), lambda b, hp, i: (t_rows // c_len + b, hp)),
                  pl.BlockSpec((1, 2 * VROWS, KC), lambda b, hp, i: (t_rows // KC + b, hp, 0))] + alias_spec,
        out_specs=pl.BlockSpec((tq, 128), lambda b, hp, i: (b * nq + i, hp)),
        out_shape=jax.ShapeDtypeStruct((t_rows if o_init is None else o_init.shape[0], 512), BF16),
        input_output_aliases={} if o_init is None else {5: 0},
        scratch_shapes=[pltpu.VMEM((VROWS, 256), F32)] * (2 * (tq // 256)),
        compiler_params=_cparams(("parallel", "parallel", "arbitrary")),
        name="mla_attn",
    )(qt, k, vt, k, vt, *alias_arg)


def _mla_ctx_attn_kernel(qt_ref, kc_ref, vtc_ref, o_full_ref, o_ref):
    del o_full_ref
    outs = []
    for h in range(2):
        s = _dot(kc_ref[:, 128 * h:128 * (h + 1)], qt_ref[128 * h:128 * (h + 1), :])
        p = jnp.exp2(s - jnp.max(s, axis=0, keepdims=True)).astype(BF16)
        outs.append(_dot(vtc_ref[0, VROWS * h:VROWS * (h + 1), :], p))
    o_ref[...] = _mla_finish(outs[0], outs[1])


def _mla_ctx_attn(qt, k, vt, o_full, nb, t_rows, c_len):
    blk0 = t_rows // c_len
    return pl.pallas_call(
        _mla_ctx_attn_kernel,
        grid=(nb, 4),
        in_specs=[pl.BlockSpec((256, c_len), lambda b, hp: (hp, blk0 + b)),
                  pl.BlockSpec((c_len, 256), lambda b, hp: (blk0 + b, hp)),
                  pl.BlockSpec((1, 2 * VROWS, KC), lambda b, hp: (blk0 + b, hp, 0)),
                  pl.BlockSpec(memory_space=pl.ANY)],
        out_specs=pl.BlockSpec((c_len, 128), lambda b, hp: (blk0 + b, hp)),
        out_shape=jax.ShapeDtypeStruct(o_full.shape, BF16),
        input_output_aliases={3: 0},
        compiler_params=_cparams(("parallel", "parallel")),
        name="mla_ctx_attn",
    )(qt, k, vt, o_full)


def _ctx_attn_kernel(q_ref, k_ref, v_ref, o_full_ref, o_ref, *, dqk):
    del o_full_ref
    v = v_ref[...]
    outs = []
    for h in range(2):
        sl = slice(dqk * h, dqk * (h + 1))
        s = _dot_nt(q_ref[:, sl], k_ref[:, sl])
        m = jnp.max(s, axis=-1, keepdims=True)
        p = jnp.exp(s - m)
        l = jnp.sum(p, axis=-1, keepdims=True)
        outs.append(_dot(p.astype(BF16), v) / l)
    o_ref[...] = _pick_heads(outs[0], outs[1]).astype(BF16)


def _ctx_attn(q, k, v, o_full, nb, c_len, row_blk0, dqk, qcol, kcol, vcol, name):
    w = 2 * dqk
    return pl.pallas_call(
        functools.partial(_ctx_attn_kernel, dqk=dqk),
        grid=(nb, 4),
        in_specs=[pl.BlockSpec((c_len, w), lambda b, hp: (row_blk0 + b, qcol // w + hp)),
                  pl.BlockSpec((c_len, w), lambda b, hp: (row_blk0 + b, kcol // w + hp)),
                  pl.BlockSpec((c_len, 128), lambda b, hp: (row_blk0 + b, vcol // 128 + hp)),
                  pl.BlockSpec(memory_space=pl.ANY)],
        out_specs=pl.BlockSpec((c_len, 128), lambda b, hp: (row_blk0 + b, hp)),
        out_shape=jax.ShapeDtypeStruct(o_full.shape, BF16),
        input_output_aliases={3: 0},
        compiler_params=_cparams(("parallel", "parallel")),
        name=name,
    )(q, k, v, o_full)


def _na_fill_bias(tab_ref, bias_ref):
    lane = lax.broadcasted_iota(jnp.int32, (GRID_W, 2 * GRID_W), 1)
    neg = jnp.full((GRID_W, 2 * GRID_W), NEG, F32)
    for pat in range(3):
        for j in range(NA_G):
            kr_lo = (0, j, NA_G)[pat]
            for h in range(2):
                for kp in range(NA_WIN // 2):
                    halves = []
                    for kr in (2 * kp, 2 * kp + 1):
                        d_row = kr - NA_G * pat - j + NA_ROWS - 1
                        halves.append(tab_ref[h, d_row] if kr_lo <= kr < kr_lo + NA_ROWS else neg)
                    bias_ref[pat, h, GRID_W * j:GRID_W * (j + 1), 2 * GRID_W * kp:2 * GRID_W * (kp + 1)] = (
                        jnp.where(lane < GRID_W, halves[0], halves[1]))


def _na_attn_kernel(q_ref, k_ref, v_ref, kc_ref, vc_ref, tab_ref, *rest, rows, aliased):
    o_ref, bias_ref = rest[1:] if aliased else rest
    nq = NA_G * GRID_W
    nk = NA_WIN * GRID_W
    vc = vc_ref[...]
    _na_fill_bias(tab_ref, bias_ref)

    def body(g, carry):
        r0 = g * NA_G
        ws = jnp.clip(r0 - NA_ROWS // 2, 0, rows - NA_WIN)
        pat = lax.shift_right_logical(r0 - ws, 2)
        qoff = pl.multiple_of(r0 * GRID_W, nq)
        koff = pl.multiple_of(ws * GRID_W, GRID_W)
        q = q_ref[pl.ds(qoff, nq), :]
        kw = k_ref[pl.ds(koff, nk), :]
        vw = v_ref[pl.ds(koff, nk), :]
        scores = []
        for h in range(2):
            sl = slice(64 * h, 64 * (h + 1))
            scores.append((_dot_nt(q[:, sl], kw[:, sl]) + bias_ref[pat, h], _dot_nt(q[:, sl], kc_ref[:, sl])))
        outs = []
        for s, sc in scores:
            m = jnp.maximum(jnp.max(s, axis=-1, keepdims=True), jnp.max(sc, axis=-1, keepdims=True))
            p = jnp.exp(s - m)
            pc = jnp.exp(sc - m)
            l = jnp.sum(p, axis=-1, keepdims=True) + jnp.sum(pc, axis=-1, keepdims=True)
            outs.append((_dot(p.astype(BF16), vw) + _dot(pc.astype(BF16), vc)) / l)
        o_ref[pl.ds(qoff, nq), :] = _pick_heads(outs[0], outs[1]).astype(BF16)
        return carry

    lax.fori_loop(0, rows // NA_G, body, 0)


def _na_attn(proj, tab, nb, s_len, c_len, o_init):
    rows = s_len // GRID_W
    assert NA_G == 4 and rows % NA_G == 0 and rows >= NA_WIN and NA_WIN % 2 == 0
    ctx_blk = nb * s_len // c_len
    alias_spec, alias_arg = _alias(o_init)
    return pl.pallas_call(
        functools.partial(_na_attn_kernel, rows=rows, aliased=o_init is not None),
        grid=(nb, 4),
        in_specs=[pl.BlockSpec((s_len, 128), lambda b, hp: (b, OFF_QA // 128 + hp)),
                  pl.BlockSpec((s_len, 128), lambda b, hp: (b, OFF_KA // 128 + hp)),
                  pl.BlockSpec((s_len, 128), lambda b, hp: (b, OFF_VA // 128 + hp)),
                  pl.BlockSpec((c_len, 128), lambda b, hp: (ctx_blk + b, OFF_KA // 128 + hp)),
                  pl.BlockSpec((c_len, 128), lambda b, hp: (ctx_blk + b, OFF_VA // 128 + hp)),
                  pl.BlockSpec((2, 2 * NA_ROWS - 1, GRID_W, 2 * GRID_W), lambda b, hp: (hp, 0, 0, 0))] + alias_spec,
        out_specs=pl.BlockSpec((s_len, 128), lambda b, hp: (b, hp)),
        out_shape=jax.ShapeDtypeStruct((nb * s_len if o_init is None else o_init.shape[0], 512), BF16),
        input_output_aliases={} if o_init is None else {6: 0},
        scratch_shapes=[pltpu.VMEM((3, 2, NA_G * GRID_W, NA_WIN * GRID_W), F32)],
        compiler_params=_cparams(("parallel", "parallel")),
        name="na_attn",
    )(proj, proj, proj, proj, proj, tab, *alias_arg)


def _na_bias_table(rpb):
    col = jnp.arange(GRID_W)
    cstart = jnp.clip(col - NA_COLS // 2, 0, GRID_W - NA_COLS)
    kc = jnp.arange(GRID_W)
    valid_c = (kc[None, :] >= cstart[:, None]) & (kc[None, :] < cstart[:, None] + NA_COLS)
    dcol = jnp.clip(kc[None, :] - col[:, None] + (NA_COLS - 1), 0, 2 * NA_COLS - 2)
    tab = jnp.where(valid_c[None, None], rpb[:, :, dcol], NEG).astype(F32)
    return jnp.concatenate([tab, tab], axis=-1)


def _local_kernel(up_p, up_c, up_n, cg_p, cg_c, cg_n, xc_p, xc_c, xc_n, bg_ref,
                  pw_ref, ps_ref, cw_ref, ob_ref, od_ref, *, lat_tiles, tiles_lat_seq, s_len, c_len):
    i = pl.program_id(0)
    is_lat = i < lat_tiles
    j = jnp.where(is_lat, i % tiles_lat_seq, 0)
    n_seq = jnp.where(is_lat, s_len, c_len)
    first = j == 0
    last = (j + 1) * TL == n_seq

    def ext(p_ref, c_ref, n_ref):
        p = jnp.where(first, 0.0, p_ref[...].astype(F32))
        n = jnp.where(last, 0.0, n_ref[...].astype(F32))
        return jnp.concatenate([p, c_ref[...].astype(F32), n], axis=0)

    z = ext(cg_p, cg_c, cg_n) * ext(xc_p, xc_c, xc_n)
    cw = cw_ref[...]
    y = (cw[0:1] * z[HALO - 1:HALO - 1 + TL] + cw[1:2] * z[HALO:HALO + TL]
         + cw[2:3] * z[HALO + 1:HALO + 1 + TL])
    od_ref[...] = (bg_ref[...].astype(F32) * y).astype(BF16)

    u = ext(up_p, up_c, up_n)
    t = j * TL + lax.broadcasted_iota(jnp.int32, (TL, 1), 0)
    ps = ps_ref[...]
    for g, w in enumerate(POOL_WINDOWS):
        sl = slice(POOL_GROUP * g, POOL_GROUP * (g + 1))
        ug = u[:, sl]
        acc = ug[HALO - w // 2:HALO - w // 2 + TL]
        for d in range(-w // 2 + 1, w // 2):
            acc = acc + ug[HALO + d:HALO + d + TL]
        cnt = (jnp.minimum(t - w // 2 + w, n_seq) - jnp.maximum(t - w // 2, 0)).astype(F32)
        pooled = acc / cnt - ug[HALO:HALO + TL]
        mixed = _dot(pooled.astype(BF16), pw_ref[g])
        ob_ref[:, sl] = (mixed * ps[:, sl]).astype(BF16)


def _local(proj, pool_w, pool_scale, conv_w, nb, s_len, c_len):
    r = proj.shape[0]
    lat_tiles = nb * s_len // TL
    hpt = TL // HALO
    nhalo = r // HALO

    def cur(off):
        return pl.BlockSpec((TL, 512), lambda i: (i, off // 512))

    def prev(off):
        return pl.BlockSpec((HALO, 512), lambda i: (jnp.maximum(i * hpt - 1, 0), off // 512))

    def nxt(off):
        return pl.BlockSpec((HALO, 512), lambda i: (jnp.minimum((i + 1) * hpt, nhalo - 1), off // 512))

    specs = []
    for off in (OFF_UP, OFF_CG, OFF_XC):
        specs += [prev(off), cur(off), nxt(off)]
    specs += [cur(OFF_BG),
              pl.BlockSpec((4, POOL_GROUP, POOL_GROUP), lambda i: (0, 0, 0)),
              pl.BlockSpec((1, 512), lambda i: (0, 0)),
              pl.BlockSpec((CONV_K, 512), lambda i: (0, 0))]
    return pl.pallas_call(
        functools.partial(_local_kernel, lat_tiles=lat_tiles, tiles_lat_seq=s_len // TL,
                          s_len=s_len, c_len=c_len),
        grid=(r // TL,),
        in_specs=specs,
        out_specs=[pl.BlockSpec((TL, 512), lambda i: (i, 0)), pl.BlockSpec((TL, 512), lambda i: (i, 0))],
        out_shape=[jax.ShapeDtypeStruct((r, 512), BF16), jax.ShapeDtypeStruct((r, 512), BF16)],
        compiler_params=_cparams(("parallel",)),
        name="local_mix",
    )(*([proj] * 10), pool_w, pool_scale, conv_w)


def _merge_kernel(oa_ref, ob_ref, oc_ref, od_ref, gate_ref, wb_ref, wo_ref, x_ref, gm_ref, g_ref, o_ref):
    merged = None
    for k, o in enumerate((oa_ref, ob_ref, oc_ref, od_ref)):
        proj = _dot(o[...], wb_ref[k])
        gk = jax.nn.sigmoid(gate_ref[:, 1024 * k:1024 * (k + 1)].astype(F32))
        merged = gk * proj if merged is None else merged + gk * proj
    y = _dot(merged.astype(BF16), wo_ref[...])
    o_ref[...] = x_ref[...] + gm_ref[...] * _rms(y, g_ref[...])


def _merge(oa, ob, oc, od, proj, wb, wo, xall, gate_mod, g, n_rows, tiles_per_batch):
    d = xall.shape[1]
    tm = TM
    tpb = tiles_per_batch * (TM // tm)
    row = lambda w: pl.BlockSpec((tm, w), lambda i: (i, 0))
    return pl.pallas_call(
        _merge_kernel,
        grid=(n_rows // tm,),
        in_specs=[row(512), row(512), row(512), row(512), row(4096),
                  pl.BlockSpec((4, 512, d), lambda i: (0, 0, 0)),
                  pl.BlockSpec((d, d), lambda i: (0, 0)),
                  row(d),
                  pl.BlockSpec((None, 1, d), lambda i: (i // tpb, 0, 0)),
                  pl.BlockSpec((1, d), lambda i: (0, 0))],
        out_specs=row(d),
        out_shape=jax.ShapeDtypeStruct((n_rows, d), F32),
        compiler_params=_cparams(("parallel",)),
        name="merge",
    )(oa, ob, oc, od, proj, wb, wo, xall, gate_mod, g.reshape(1, d))


def _ffn_kernel(x_ref, g2_ref, sh_ref, sc_ref, gm_ref, g3_ref, w1_ref, w3_ref, w2_ref, o_ref, *, n_split):
    x = x_ref[...]
    h = (_rms(x, g2_ref[...]) * (1.0 + sc_ref[...]) + sh_ref[...]).astype(BF16)
    tf = w1_ref.shape[1] // n_split
    acc = None
    for j in range(n_split):
        sl = slice(tf * j, tf * (j + 1))
        a = _dot(h, w1_ref[:, sl])
        b = _dot(h, w3_ref[:, sl])
        y = _dot((a * jax.nn.sigmoid(a) * b).astype(BF16), w2_ref[sl, :])
        acc = y if acc is None else acc + y
    o_ref[...] = x + gm_ref[...] * _rms(acc, g3_ref[...])


def _ffn(xall, g2, shift, scale, gate_mod, g3, w1, w3, w2, tiles_per_batch):
    r, d = xall.shape
    dff = w1.shape[1]
    mod = pl.BlockSpec((None, 1, d), lambda i: (i // tiles_per_batch, 0, 0))
    vec = pl.BlockSpec((1, d), lambda i: (0, 0))
    resident = lambda shape: pl.BlockSpec(shape, lambda i: (0, 0), pipeline_mode=pl.Buffered(1))
    return pl.pallas_call(
        functools.partial(_ffn_kernel, n_split=2),
        grid=(r // TM,),
        in_specs=[pl.BlockSpec((TM, d), lambda i: (i, 0)), vec, mod, mod, mod, vec,
                  resident((d, dff)), resident((d, dff)), resident((dff, d))],
        out_specs=pl.BlockSpec((TM, d), lambda i: (i, 0)),
        out_shape=jax.ShapeDtypeStruct((r, d), F32),
        compiler_params=_cparams(("parallel",)),
        name="ffn_dense",
    )(xall, g2.reshape(1, d), shift, scale, gate_mod, g3.reshape(1, d), w1, w3, w2)


def _router_kernel(x_ref, g2_ref, sh_ref, sc_ref, rt_ref, t_ref, idx_ref, w_ref):
    t = _rms(x_ref[...], g2_ref[...]) * (1.0 + sc_ref[...]) + sh_ref[...]
    t_ref[...] = t
    logits = lax.dot_general(rt_ref[...], t, (((1,), (1,)), ((), ())), preferred_element_type=F32,
                             precision=lax.Precision.HIGHEST)
    e = lax.broadcasted_iota(jnp.int32, logits.shape, 0).astype(F32)
    m1 = jnp.max(logits, axis=0, keepdims=True)
    i1 = jnp.min(jnp.where(logits == m1, e, float(N_EXPERTS)), axis=0, keepdims=True)
    rest = jnp.where(e == i1, -jnp.inf, logits)
    m2 = jnp.max(rest, axis=0, keepdims=True)
    i2 = jnp.min(jnp.where(rest == m2, e, float(N_EXPERTS)), axis=0, keepdims=True)
    ex = jnp.exp(m2 - m1)
    w1 = 1.0 / (1.0 + ex)
    idx_ref[0:1, :] = i1.astype(jnp.int32)
    idx_ref[1:2, :] = i2.astype(jnp.int32)
    w_ref[0:1, :] = w1
    w_ref[1:2, :] = ex * w1


def _router(x, g2, shift, scale, router_t, tiles_per_batch):
    t_rows, d = x.shape
    mod = pl.BlockSpec((None, 1, d), lambda i: (i // tiles_per_batch, 0, 0))
    return pl.pallas_call(
        _router_kernel,
        grid=(t_rows // TM,),
        in_specs=[pl.BlockSpec((TM, d), lambda i: (i, 0)),
                  pl.BlockSpec((1, d), lambda i: (0, 0)), mod, mod,
                  pl.BlockSpec((N_EXPERTS, d), lambda i: (0, 0))],
        out_specs=[pl.BlockSpec((TM, d), lambda i: (i, 0)),
                   pl.BlockSpec((TOP_K, TM), lambda i: (0, i)),
                   pl.BlockSpec((TOP_K, TM), lambda i: (0, i))],
        out_shape=[jax.ShapeDtypeStruct((t_rows, d), F32),
                   jax.ShapeDtypeStruct((TOP_K, t_rows), jnp.int32),
                   jax.ShapeDtypeStruct((TOP_K, t_rows), F32)],
        compiler_params=_cparams(("parallel",)),
        name="router",
    )(x, g2.reshape(1, d), shift, scale, router_t)


def _row_gather_start(ids_ref, src_ref, buf_ref, sem, n):
    def issue(r, c):
        pltpu.make_async_copy(src_ref.at[pl.ds(ids_ref[0, r], 1)], buf_ref.at[pl.ds(r, 1)], sem).start()
        return c

    lax.fori_loop(0, n, issue, 0, unroll=8)


def _row_gather_wait(src_ref, buf_ref, sem, n):
    pltpu.make_async_copy(src_ref.at[pl.ds(0, n)], buf_ref, sem).wait()


def _moe_ffn_kernel(eid_ref, nused_ref, ids_ref, ids_next_ref, t_ref, w1_ref, w3_ref, w2_ref, o_ref,
                    xbuf_ref, xs_ref, acc_ref, sems):
    i = pl.program_id(0)
    f = pl.program_id(1)
    n_f = pl.num_programs(1)
    slot = lax.rem(i, 2)

    def row_copy(ids, r, s):
        return pltpu.make_async_copy(t_ref.at[pl.ds(ids[0, r], 1)], xbuf_ref.at[s, pl.ds(r, 1)], sems.at[s])

    def all_rows(s):
        return pltpu.make_async_copy(t_ref.at[pl.ds(0, TM)], xbuf_ref.at[s], sems.at[s])

    @pl.when((i == 0) & (f == 0))
    def _():
        def issue(r, c):
            row_copy(ids_ref, r, 0).start()
            return c

        lax.fori_loop(0, TM, issue, 0)

    @pl.when(i < nused_ref[0])
    def _():
        @pl.when(f == 0)
        def _():
            all_rows(slot).wait()
            xs_ref[...] = xbuf_ref[slot].astype(BF16)

        for step in range(MOE_F_STEPS):
            @pl.when(f == step)
            def _():
                for r in range(step * (TM // MOE_F_STEPS), (step + 1) * (TM // MOE_F_STEPS)):
                    row_copy(ids_next_ref, r, 1 - slot).start()

        x = xs_ref[...]
        a = _dot(x, w1_ref[...])
        b = _dot(x, w3_ref[...])
        y = _dot((a * jax.nn.sigmoid(a) * b).astype(BF16), w2_ref[...])

        @pl.when(f == 0)
        def _():
            acc_ref[...] = y

        @pl.when(f == n_f - 1)
        def _():
            o_ref[...] = acc_ref[...] + y

    @pl.when((i + 1 == nused_ref[0]) & (f == n_f - 1))
    def _():
        all_rows(1 - slot).wait()

    @pl.when((i >= nused_ref[0]) & (f == n_f - 1))
    def _():
        o_ref[...] = jnp.zeros_like(o_ref)


def _moe_ffn(t, slot_tok, tile_eid, n_used, w1, w3, w2):
    n_tiles = slot_tok.shape[0]
    d = t.shape[1]
    dff = w1.shape[2]
    tf = dff // MOE_F_STEPS
    assert MOE_F_STEPS == 2, "the kernel stores the first step's partial sum and adds it in the second"
    wf = lambda i, f, nu: jnp.where(i < nu[0], f, MOE_F_STEPS - 1)
    ids = lambda nxt: pl.BlockSpec((None, 1, TM),
                                   lambda i, f, eid, nu: (jnp.minimum(i + nxt, n_tiles - 1), 0, 0),
                                   memory_space=pltpu.SMEM)
    grid_spec = pltpu.PrefetchScalarGridSpec(
        num_scalar_prefetch=2,
        grid=(n_tiles, MOE_F_STEPS),
        in_specs=[ids(0), ids(1),
                  pl.BlockSpec(memory_space=pl.ANY),
                  pl.BlockSpec((None, d, tf), lambda i, f, eid, nu: (eid[i], 0, wf(i, f, nu))),
                  pl.BlockSpec((None, d, tf), lambda i, f, eid, nu: (eid[i], 0, wf(i, f, nu))),
                  pl.BlockSpec((None, tf, d), lambda i, f, eid, nu: (eid[i], wf(i, f, nu), 0))],
        out_specs=pl.BlockSpec((TM, d), lambda i, f, eid, nu: (i, 0)),
        scratch_shapes=[pltpu.VMEM((2, TM, d), F32), pltpu.VMEM((TM, d), BF16), pltpu.VMEM((TM, d), F32),
                        pltpu.SemaphoreType.DMA((2,))])
    return pl.pallas_call(
        _moe_ffn_kernel,
        grid_spec=grid_spec,
        out_shape=jax.ShapeDtypeStruct((n_tiles * TM, d), F32),
        compiler_params=_cparams(("arbitrary", "arbitrary")),
        name="moe_ffn",
    )(tile_eid, n_used, slot_tok, slot_tok, t, w1, w3, w2)


def _combine_kernel(p0_ref, p1_ref, p0n_ref, p1n_ref, ys_ref, w_ref, x_ref, gm_ref, g3_ref, o_ref,
                    b0_ref, b1_ref, sems):
    i = pl.program_id(0)
    slot = lax.rem(i, 2)

    def start(s, ids0, ids1):
        _row_gather_start(ids0, ys_ref, b0_ref.at[s], sems.at[0, s], TM)
        _row_gather_start(ids1, ys_ref, b1_ref.at[s], sems.at[1, s], TM)

    def wait(s):
        _row_gather_wait(ys_ref, b0_ref.at[s], sems.at[0, s], TM)
        _row_gather_wait(ys_ref, b1_ref.at[s], sems.at[1, s], TM)

    @pl.when(i == 0)
    def _():
        start(0, p0_ref, p1_ref)

    start(1 - slot, p0n_ref, p1n_ref)
    wait(slot)
    w = w_ref[...]
    y = w[:, 0:1] * b0_ref[slot] + w[:, 1:2] * b1_ref[slot]
    o_ref[...] = x_ref[...] + gm_ref[...] * _rms(y, g3_ref[...])

    @pl.when(i == pl.num_programs(0) - 1)
    def _():
        wait(1 - slot)


def _combine(pos0, pos1, ys, wcol, x, gate_mod, g3, tiles_per_batch):
    t_rows, d = x.shape
    n_tiles = t_rows // TM
    ids = lambda nxt: pl.BlockSpec((None, 1, TM), lambda i: (jnp.minimum(i + nxt, n_tiles - 1), 0, 0),
                                   memory_space=pltpu.SMEM)
    return pl.pallas_call(
        _combine_kernel,
        grid=(n_tiles,),
        in_specs=[ids(0), ids(0), ids(1), ids(1),
                  pl.BlockSpec(memory_space=pl.ANY),
                  pl.BlockSpec((TM, TOP_K), lambda i: (i, 0)),
                  pl.BlockSpec((TM, d), lambda i: (i, 0)),
                  pl.BlockSpec((None, 1, d), lambda i: (i // tiles_per_batch, 0, 0)),
                  pl.BlockSpec((1, d), lambda i: (0, 0))],
        out_specs=pl.BlockSpec((TM, d), lambda i: (i, 0)),
        out_shape=jax.ShapeDtypeStruct((t_rows, d), F32),
        scratch_shapes=[pltpu.VMEM((2, TM, d), F32), pltpu.VMEM((2, TM, d), F32),
                        pltpu.SemaphoreType.DMA((2, 2))],
        compiler_params=_cparams(("arbitrary",)),
        name="moe_combine",
    )(pos0, pos1, pos0, pos1, ys, wcol, x, gate_mod, g3.reshape(1, d))


def _routing_tables(top_i, t_rows):
    n_assign = TOP_K * t_rows
    n_tiles = n_assign // TM + N_EXPERTS
    e_flat = top_i.reshape(n_assign)
    onehot = (e_flat[:, None] == jnp.arange(N_EXPERTS)[None, :]).astype(jnp.int32)
    csum = jnp.cumsum(onehot, axis=0)
    counts = csum[-1]
    rank = jnp.sum((csum - onehot) * onehot, axis=1)
    tiles_e = (counts + TM - 1) // TM
    tile_end = jnp.cumsum(tiles_e)
    tile_start = tile_end - tiles_e
    slot = (tile_start * TM)[e_flat] + rank
    tok = jnp.tile(jnp.arange(t_rows, dtype=jnp.int32), TOP_K)
    slot_tok = jnp.zeros((n_tiles * TM,), jnp.int32).at[slot].set(tok)
    n_used = tile_end[-1]
    tile_ids = jnp.arange(n_tiles)
    tile_eid = jnp.sum(tile_ids[:, None] >= tile_end[None, :], axis=1)
    last_eid = jnp.sum(n_used - 1 >= tile_end)
    tile_eid = jnp.where(tile_ids < n_used, tile_eid, last_eid).astype(jnp.int32)
    pos = slot.reshape(TOP_K, t_rows).astype(jnp.int32)
    return (slot_tok.reshape(n_tiles, 1, TM), tile_eid, n_used.reshape(1).astype(jnp.int32),
            pos[0].reshape(t_rows // TM, 1, TM), pos[1].reshape(t_rows // TM, 1, TM))


def _prep_w_in(w):
    d = w.shape[0]
    w = w.astype(BF16)
    q_a, k_a, v_a, u_p, c_q, c_kv, k_r, b_g, c_g, x_c, gate = jnp.split(w, IN_SPLITS, axis=1)
    z = lambda n: jnp.zeros((d, n), w.dtype)
    kr_e, kr_o = k_r[:, 0::2], k_r[:, 1::2]
    cols = [gate, q_a * NA_SCALE, k_a, v_a, u_p, b_g, c_g, x_c, c_q, z(128), c_kv,
            z(64), kr_e, kr_o, -kr_o, kr_e, z(128)]
    return jnp.concatenate(cols, axis=1).astype(BF16)


def _prep_w_uq(w_uq):
    wq = w_uq.reshape(MLA_Q_LORA, MLA_HEADS, MLA_NOPE + MLA_ROPE)
    nope, r = wq[..., :MLA_NOPE], wq[..., MLA_NOPE:]
    re, ro = r[..., 0::2], r[..., 1::2]
    ext = jnp.concatenate([nope, re, ro, -ro, re], axis=-1).reshape(MLA_Q_LORA, MLA_HEADS * 128)
    return jnp.pad(ext, ((0, 512 - MLA_Q_LORA), (0, 0))).T.astype(BF16)


def _prep_w_ukv(w_ukv):
    wkv = w_ukv.reshape(MLA_KV_LORA, MLA_HEADS, MLA_NOPE + MLA_V)
    zeros = jnp.zeros((MLA_KV_LORA, MLA_HEADS, 64), w_ukv.dtype)
    kn = jnp.concatenate([wkv[..., :MLA_NOPE], zeros], axis=-1).reshape(MLA_KV_LORA, MLA_HEADS * 128)
    vt = jnp.concatenate([wkv[..., MLA_NOPE:], zeros[..., :VROWS - MLA_V]], axis=-1)
    return kn.astype(BF16), vt.reshape(MLA_KV_LORA, MLA_HEADS * VROWS).T.astype(BF16)


def _rope_tables(nb, s_len, n_ctx_rows):
    pos = jnp.arange(s_len)
    row = (pos // GRID_W).astype(F32)
    col = (pos % GRID_W).astype(F32)
    n_pairs = MLA_ROPE // 4
    inv_freq = ROPE_BASE ** (-jnp.arange(n_pairs, dtype=F32) / n_pairs)
    ang = jnp.concatenate([row[:, None] * inv_freq, col[:, None] * inv_freq], axis=-1)
    cos = jnp.concatenate([jnp.tile(jnp.cos(ang), (nb, 1)), jnp.ones((n_ctx_rows, 16), F32)], axis=0)
    sin = jnp.concatenate([jnp.tile(jnp.sin(ang), (nb, 1)), jnp.zeros((n_ctx_rows, 16), F32)], axis=0)
    r = cos.shape[0]
    cos2 = jnp.concatenate([cos, cos], axis=1)
    sin2 = jnp.concatenate([sin, sin], axis=1)
    pad = lambda t, lead: jnp.concatenate([lead, t, jnp.zeros((r, 32), F32)], axis=1)
    return (pad(cos2, jnp.ones((r, 64), F32)), pad(sin2, jnp.zeros((r, 64), F32)), cos2.T, sin2.T)


def kernel(x, c, ctx, c_ctx, w_ada, b_ada, g_norm, w_in, na_rpb, pool_w, pool_scale, mla_g_q, mla_w_uq,
           mla_g_kv, mla_w_ukv, conv_w, w_branch, w_out, ffn_w1, ffn_w3, ffn_w2, moe_router, moe_w1,
           moe_w3, moe_w2):
    nb, s_len, d = x.shape
    c_len = ctx.shape[1]
    depth = w_in.shape[0]
    t_rows = nb * s_len
    n_ctx = nb * c_len
    assert s_len % TM == 0 and n_ctx == TM and s_len % GRID_W == 0 and t_rows % c_len == 0
    assert c_len == KC and c_len == TL
    tpb = s_len // TM
    rows = s_len // GRID_W

    xall = jnp.concatenate([x.reshape(t_rows, d), ctx.reshape(n_ctx, d)], axis=0)
    c8 = jnp.zeros((8, d), F32).at[:nb].set(c).at[nb].set(c_ctx)
    cos_t, sin_t, cos_tt, sin_tt = _rope_tables(nb, s_len, n_ctx)

    for l in range(depth):
        last = l == depth - 1
        mod = _adaln(c8, w_ada[l], b_ada[l]).reshape(8, 6, 1, d)
        mods = [mod[:, k] for k in range(6)]

        proj = _inproj(xall, g_norm[l, 0], mods[0], mods[1], _prep_w_in(w_in[l]), tpb)

        wkn, wvt = _prep_w_ukv(mla_w_ukv[l])
        gq = jnp.pad(mla_g_q[l], (0, 512 - MLA_Q_LORA)).reshape(1, 512)
        qt_m, k_m, vt_m = _mla_prep(proj, gq, mla_g_kv[l].reshape(1, MLA_KV_LORA), _prep_w_uq(mla_w_uq[l]),
                                    wkn, wvt, cos_t, sin_t, cos_tt, sin_tt)
        n_rows = t_rows if last else t_rows + n_ctx
        o_init = None if last else jnp.zeros((n_rows, 512), BF16)
        o_c = _mla_attn(qt_m, k_m, vt_m, nb, s_len, c_len, o_init)
        o_a = _na_attn(proj, _na_bias_table(na_rpb[l]), nb, s_len, c_len, o_init)
        o_b, o_d = _local(proj, pool_w[l].astype(BF16), pool_scale[l].reshape(1, 512), conv_w[l],
                          nb, s_len, c_len)

        wb = w_branch[l].astype(BF16)
        wo = w_out[l].astype(BF16)
        if not last:
            ctx_blk = t_rows // c_len
            o_c = _mla_ctx_attn(qt_m, k_m, vt_m, o_c, nb, t_rows, c_len)
            o_a = _ctx_attn(proj, proj, proj, o_a, nb, c_len, ctx_blk, 64, OFF_QA, OFF_KA, OFF_VA, "na_ctx_attn")
        xall = _merge(o_a, o_b, o_c, o_d, proj, wb, wo, xall, mods[2], g_norm[l, 1], n_rows, tpb)

        if l % 2 == 0:
            j = l // 2
            xall = _ffn(xall, g_norm[l, 2], mods[3], mods[4], mods[5], g_norm[l, 3],
                        ffn_w1[j].astype(BF16), ffn_w3[j].astype(BF16), ffn_w2[j].astype(BF16), tpb)
        else:
            j = l // 2
            assert last, "context rows are not routed through the experts"
            t_f32, top_i, top_w = _router(xall, g_norm[l, 2], mods[3], mods[4], moe_router[j].T, tpb)
            slot_tok, tile_eid, n_used, pos0, pos1 = _routing_tables(top_i, t_rows)
            ys = _moe_ffn(t_f32, slot_tok, tile_eid, n_used, moe_w1[j].astype(BF16), moe_w3[j].astype(BF16),
                          moe_w2[j].astype(BF16))
            xall = _combine(pos0, pos1, ys, top_w.T, xall, mods[5], g_norm[l, 3], tpb)

    return xall[:t_rows].reshape(nb, s_len, d)
```

```python
import functools

import numpy as np
import jax
import jax.numpy as jnp
from jax import lax
from jax.experimental import pallas as pl
from jax.experimental.pallas import tpu as pltpu

F32 = jnp.float32
BF16 = jnp.bfloat16

GRID_W = 64
N_BRANCH = 4
BRANCH_W = 512
EPS = 1e-6
NA_HEADS = 8
NA_HEAD_DIM = 64
NA_ROWS = 8
NA_COLS = 16
NA_G = 4
NA_WIN = NA_G + NA_ROWS
NA_UNROLL = 2
POOL_WINDOWS = (2, 4, 8, 16)
POOL_GROUP = 128
MLA_HEADS = 8
MLA_Q_LORA = 384
MLA_KV_LORA = 256
MLA_NOPE = 64
MLA_ROPE = 32
MLA_V = 64
ROPE_BASE = 10000.0
CONV_K = 3
N_EXPERTS = 8
TOP_K = 2

IN_SIZES = (512, 512, 512, 512, MLA_Q_LORA, MLA_KV_LORA, MLA_ROPE, 512, 512, 512, 4096)
IN_SPLITS = tuple(int(s) for s in np.cumsum(IN_SIZES)[:-1])

OFF_GATE = 0
OFF_QA = 4096
OFF_KA = 4608
OFF_VA = 5120
OFF_UP = 5632
OFF_BG = 6144
OFF_CG = 6656
OFF_XC = 7168
OFF_CQ = 7680
OFF_CKV = 8192
OFF_KR = 8448
N_IN = 8704

NA_SCALE = NA_HEAD_DIM ** -0.5
MLA_SCALE = (MLA_NOPE + MLA_ROPE) ** -0.5
LOG2E = 1.4426950408889634
NEG = -1e30

VMEM_LIMIT = 52 * 1024 * 1024
TM = 512
TL = 256
HALO = 16
KC = 256
VROWS = 80
MOE_F_STEPS = 2


def _cparams(sem):
    return pltpu.CompilerParams(dimension_semantics=sem, vmem_limit_bytes=VMEM_LIMIT)


def _rms(xf, g):
    ms = jnp.mean(xf * xf, axis=-1, keepdims=True)
    return xf * lax.rsqrt(ms + EPS) * g


def _dot(a, b):
    return jnp.dot(a, b, preferred_element_type=F32)


def _dot_nt(a, b):
    return lax.dot_general(a, b, (((1,), (1,)), ((), ())), preferred_element_type=F32)


def _adaln_kernel(c_ref, w_ref, b_ref, o_ref):
    c = c_ref[...]
    sc = c * jax.nn.sigmoid(c)
    o_ref[...] = jnp.dot(sc, w_ref[...], preferred_element_type=F32,
                         precision=lax.Precision.HIGHEST) + b_ref[...]


def _adaln(c8, w, b):
    d = c8.shape[1]
    n = w.shape[1]
    tn = 1536
    return pl.pallas_call(
        _adaln_kernel,
        grid=(n // tn,),
        in_specs=[pl.BlockSpec((8, d), lambda j: (0, 0)),
                  pl.BlockSpec((d, tn), lambda j: (0, j)),
                  pl.BlockSpec((1, tn), lambda j: (0, j))],
        out_specs=pl.BlockSpec((8, tn), lambda j: (0, j)),
        out_shape=jax.ShapeDtypeStruct((8, n), F32),
        compiler_params=_cparams(("arbitrary",)),
        name="adaln",
    )(c8, w, b.reshape(1, n))


def _inproj_kernel(x_ref, g_ref, sh_ref, sc_ref, w_ref, o_ref, *, n_split):
    h = (_rms(x_ref[...], g_ref[...]) * (1.0 + sc_ref[...]) + sh_ref[...]).astype(BF16)
    tn = w_ref.shape[1] // n_split
    for j in range(n_split):
        o_ref[:, tn * j:tn * (j + 1)] = _dot(h, w_ref[:, tn * j:tn * (j + 1)]).astype(BF16)


def _inproj(xall, g, shift, scale, w_p, tiles_per_batch):
    r, d = xall.shape
    n = w_p.shape[1]
    mod_spec = pl.BlockSpec((None, 1, d), lambda i: (i // tiles_per_batch, 0, 0))
    return pl.pallas_call(
        functools.partial(_inproj_kernel, n_split=4),
        grid=(r // TM,),
        in_specs=[pl.BlockSpec((TM, d), lambda i: (i, 0)),
                  pl.BlockSpec((1, d), lambda i: (0, 0)),
                  mod_spec, mod_spec,
                  pl.BlockSpec((d, n), lambda i: (0, 0), pipeline_mode=pl.Buffered(1))],
        out_specs=pl.BlockSpec((TM, n), lambda i: (i, 0)),
        out_shape=jax.ShapeDtypeStruct((r, n), BF16),
        compiler_params=_cparams(("parallel",)),
        name="inproj",
    )(xall, g.reshape(1, d), shift, scale, w_p)


def _mla_prep_kernel(cq_ref, ckv_ref, kr_ref, gq_ref, gkv_ref, wqt_ref, wkn_ref, wvt_ref, ones_ref,
                     cos_ref, sin_ref, cost_ref, sint_ref, qt_out, k_out, vt_out):
    cq = cq_ref[...].astype(F32)
    ms = jnp.sum(cq * cq, axis=-1, keepdims=True) * (1.0 / MLA_Q_LORA)
    cqn = (cq * lax.rsqrt(ms + EPS) * gq_ref[...]).astype(BF16)
    qt = _dot_nt(wqt_ref[...], cqn) * (MLA_SCALE * LOG2E)
    cost = cost_ref[...]
    sint = sint_ref[...]
    for h in range(MLA_HEADS):
        b = 128 * h
        qt_out[b:b + 64, :] = qt[b:b + 64].astype(BF16)
        qt_out[b + 64:b + 96, :] = (qt[b + 64:b + 96] * cost + qt[b + 96:b + 128] * sint).astype(BF16)
        qt_out[b + 96:b + 128, :] = jnp.zeros((32, TM), BF16)

    ckvn = _rms(ckv_ref[...].astype(F32), gkv_ref[...]).astype(BF16)
    kn = _dot(ckvn, wkn_ref[...])
    kseg = kr_ref[...].astype(F32)
    krt = kseg * cos_ref[...] + pltpu.roll(kseg, 96, 1) * sin_ref[...]
    for h in range(MLA_HEADS):
        sl = slice(128 * h, 128 * (h + 1))
        k_out[:, sl] = (kn[:, sl] + krt).astype(BF16)

    vt = (_dot_nt(wvt_ref[...], ckvn) + ones_ref[...]).astype(BF16)
    for j in range(TM // KC):
        vt_out[j] = vt[:, KC * j:KC * (j + 1)]


def _mla_prep(proj, gq, gkv, wqt, wkn, wvt, cos_t, sin_t, cos_tt, sin_tt):
    r = proj.shape[0]
    nv = MLA_HEADS * VROWS
    ones_col = ((jnp.arange(nv) % VROWS) >= MLA_V).astype(F32).reshape(nv, 1)
    full = lambda shape: pl.BlockSpec(shape, lambda i: (0, 0))
    return pl.pallas_call(
        _mla_prep_kernel,
        grid=(r // TM,),
        in_specs=[pl.BlockSpec((TM, 512), lambda i: (i, OFF_CQ // 512)),
                  pl.BlockSpec((TM, 256), lambda i: (i, OFF_CKV // 256)),
                  pl.BlockSpec((TM, 128), lambda i: (i, OFF_KR // 128)),
                  full((1, 512)), full((1, 256)),
                  full((1024, 512)), full((256, 1024)), full((nv, 256)), full((nv, 1)),
                  pl.BlockSpec((TM, 128), lambda i: (i, 0)),
                  pl.BlockSpec((TM, 128), lambda i: (i, 0)),
                  pl.BlockSpec((32, TM), lambda i: (0, i)),
                  pl.BlockSpec((32, TM), lambda i: (0, i))],
        out_specs=[pl.BlockSpec((1024, TM), lambda i: (0, i)),
                   pl.BlockSpec((TM, 1024), lambda i: (i, 0)),
                   pl.BlockSpec((TM // KC, nv, KC), lambda i: (i, 0, 0))],
        out_shape=[jax.ShapeDtypeStruct((1024, r), BF16),
                   jax.ShapeDtypeStruct((r, 1024), BF16),
                   jax.ShapeDtypeStruct((r // KC, nv, KC), BF16)],
        compiler_params=_cparams(("parallel",)),
        name="mla_prep",
    )(proj, proj, proj, gq, gkv, wqt, wkn, wvt, ones_col, cos_t, sin_t, cos_tt, sin_tt)


def _alias(o_init):
    return ([], []) if o_init is None else ([pl.BlockSpec(memory_space=pl.ANY)], [o_init])


def _pick_heads(o0, o1):
    lane = lax.broadcasted_iota(jnp.int32, o0.shape, 1)
    return jnp.where(lane < 64, o0, o1)


def _mla_update(s, vt, m_old, acc_ref):
    m_new = jnp.maximum(m_old, jnp.max(s, axis=0, keepdims=True))
    alpha = jnp.exp2(m_old - m_new)
    p = jnp.exp2(s - m_new).astype(BF16)
    acc_ref[...] = alpha * acc_ref[...] + _dot(vt, p)
    return m_new


def _mla_finish(acc0, acc1):
    def rows_to_lanes(acc):
        return jnp.concatenate([acc, jnp.zeros((128 - VROWS, acc.shape[1]), F32)], axis=0).T
    a0 = rows_to_lanes(acc0)
    a1 = pltpu.roll(rows_to_lanes(acc1), 64, 1)
    return _pick_heads(a0 / a0[:, 64:65], a1 / a1[:, 0:1]).astype(BF16)


def _mla_attn_kernel(qt_ref, k_ref, vt_ref, kc_ref, vtc_ref, *rest, cpi, aliased):
    o_ref, *acc_refs = rest[1:] if aliased else rest
    nt = qt_ref.shape[1] // 256
    units = [(t, h) for h in range(2) for t in range(nt)]
    qts = [qt_ref[128 * h:128 * (h + 1), 256 * t:256 * (t + 1)] for t, h in units]
    n_chunks = k_ref.shape[0] // KC
    for acc_ref in acc_refs:
        acc_ref[...] = jnp.zeros_like(acc_ref)

    def scores(kget):
        return tuple(_dot(kget(h), qts[u]) for u, (t, h) in enumerate(units))

    def latent_keys(j):
        off = pl.multiple_of(j * KC, KC)
        return lambda h: k_ref[pl.ds(off, KC), 128 * h:128 * (h + 1)]

    def step(s_cur, vget, ms, kget_next):
        s_next, ms_new = [], []
        for u, (t, h) in enumerate(units):
            if kget_next is not None:
                s_next.append(_dot(kget_next(h), qts[u]))
            ms_new.append(_mla_update(s_cur[u], vget(h), ms[u], acc_refs[u]))
        return tuple(s_next), tuple(ms_new)

    ms = tuple(jnp.full((1, 256), -jnp.inf, F32) for _ in units)
    s_ctx = scores(lambda h: kc_ref[:, 128 * h:128 * (h + 1)])
    s_cur, ms = step(s_ctx, lambda h: vtc_ref[0, VROWS * h:VROWS * (h + 1), :], ms, latent_keys(0))

    def values(j):
        return lambda h: vt_ref[j, VROWS * h:VROWS * (h + 1), :]

    def body(i, carry):
        ms, s_cur = carry
        for c in range(cpi):
            j = i * cpi + c
            s_cur, ms = step(s_cur, values(j), ms, latent_keys(j + 1))
        return ms, s_cur

    n_iter = n_chunks // cpi - 1
    ms, s_cur = lax.fori_loop(0, n_iter, body, (ms, s_cur))
    for j in range(n_iter * cpi, n_chunks):
        s_cur, ms = step(s_cur, values(j), ms, latent_keys(j + 1) if j + 1 < n_chunks else None)
    for t in range(nt):
        o_ref[256 * t:256 * (t + 1), :] = _mla_finish(acc_refs[units.index((t, 0))][...],
                                                      acc_refs[units.index((t, 1))][...])


def _mla_attn(qt, k, vt, nb, s_len, c_len, o_init):
    tq = min(1024, s_len)
    nq = s_len // tq
    t_rows = nb * s_len
    cpi = s_len // KC
    alias_spec, alias_arg = _alias(o_init)
    return pl.pallas_call(
        functools.partial(_mla_attn_kernel, cpi=cpi, aliased=o_init is not None),
        grid=(nb, 4, nq),
        in_specs=[pl.BlockSpec((256, tq), lambda b, hp, i: (hp, b * nq + i)),
                  pl.BlockSpec((s_len, 256), lambda b, hp, i: (b, hp)),
                  pl.BlockSpec((s_len // KC, 2 * VROWS, KC), lambda b, hp, i: (b, hp, 0)),
                  pl.BlockSpec((c_len, 256), lambda b, hp, i: (t_rows // c_len + b, hp)),
                  pl.BlockSpec((1, 2 * VROWS, KC), lambda b, hp, i: (t_rows // KC + b, hp, 0))] + alias_spec,
        out_specs=pl.BlockSpec((tq, 128), lambda b, hp, i: (b * nq + i, hp)),
        out_shape=jax.ShapeDtypeStruct((t_rows if o_init is None else o_init.shape[0], 512), BF16),
        input_output_aliases={} if o_init is None else {5: 0},
        scratch_shapes=[pltpu.VMEM((VROWS, 256), F32)] * (2 * (tq // 256)),
        compiler_params=_cparams(("parallel", "parallel", "arbitrary")),
        name="mla_attn",
    )(qt, k, vt, k, vt, *alias_arg)


def _mla_ctx_attn_kernel(qt_ref, kc_ref, vtc_ref, o_full_ref, o_ref):
    del o_full_ref
    outs = []
    for h in range(2):
        s = _dot(kc_ref[:, 128 * h:128 * (h + 1)], qt_ref[128 * h:128 * (h + 1), :])
        p = jnp.exp2(s - jnp.max(s, axis=0, keepdims=True)).astype(BF16)
        outs.append(_dot(vtc_ref[0, VROWS * h:VROWS * (h + 1), :], p))
    o_ref[...] = _mla_finish(outs[0], outs[1])


def _mla_ctx_attn(qt, k, vt, o_full, nb, t_rows, c_len):
    blk0 = t_rows // c_len
    return pl.pallas_call(
        _mla_ctx_attn_kernel,
        grid=(nb, 4),
        in_specs=[pl.BlockSpec((256, c_len), lambda b, hp: (hp, blk0 + b)),
                  pl.BlockSpec((c_len, 256), lambda b, hp: (blk0 + b, hp)),
                  pl.BlockSpec((1, 2 * VROWS, KC), lambda b, hp: (blk0 + b, hp, 0)),
                  pl.BlockSpec(memory_space=pl.ANY)],
        out_specs=pl.BlockSpec((c_len, 128), lambda b, hp: (blk0 + b, hp)),
        out_shape=jax.ShapeDtypeStruct(o_full.shape, BF16),
        input_output_aliases={3: 0},
        compiler_params=_cparams(("parallel", "parallel")),
        name="mla_ctx_attn",
    )(qt, k, vt, o_full)


def _ctx_attn_kernel(q_ref, k_ref, v_ref, o_full_ref, o_ref, *, dqk):
    del o_full_ref
    v = v_ref[...]
    outs = []
    for h in range(2):
        sl = slice(dqk * h, dqk * (h + 1))
        s = _dot_nt(q_ref[:, sl], k_ref[:, sl])
        m = jnp.max(s, axis=-1, keepdims=True)
        p = jnp.exp(s - m)
        l = jnp.sum(p, axis=-1, keepdims=True)
        outs.append(_dot(p.astype(BF16), v) / l)
    o_ref[...] = _pick_heads(outs[0], outs[1]).astype(BF16)


def _ctx_attn(q, k, v, o_full, nb, c_len, row_blk0, dqk, qcol, kcol, vcol, name):
    w = 2 * dqk
    return pl.pallas_call(
        functools.partial(_ctx_attn_kernel, dqk=dqk),
        grid=(nb, 4),
        in_specs=[pl.BlockSpec((c_len, w), lambda b, hp: (row_blk0 + b, qcol // w + hp)),
                  pl.BlockSpec((c_len, w), lambda b, hp: (row_blk0 + b, kcol // w + hp)),
                  pl.BlockSpec((c_len, 128), lambda b, hp: (row_blk0 + b, vcol // 128 + hp)),
                  pl.BlockSpec(memory_space=pl.ANY)],
        out_specs=pl.BlockSpec((c_len, 128), lambda b, hp: (row_blk0 + b, hp)),
        out_shape=jax.ShapeDtypeStruct(o_full.shape, BF16),
        input_output_aliases={3: 0},
        compiler_params=_cparams(("parallel", "parallel")),
        name=name,
    )(q, k, v, o_full)


def _na_fill_bias(tab_ref, bias_ref):
    lane = lax.broadcasted_iota(jnp.int32, (GRID_W, 2 * GRID_W), 1)
    neg = jnp.full((GRID_W, 2 * GRID_W), NEG, F32)
    for pat in range(3):
        for j in range(NA_G):
            kr_lo = (0, j, NA_G)[pat]
            for h in range(2):
                for kp in range(NA_WIN // 2):
                    halves = []
                    for kr in (2 * kp, 2 * kp + 1):
                        d_row = kr - NA_G * pat - j + NA_ROWS - 1
                        halves.append(tab_ref[h, d_row] if kr_lo <= kr < kr_lo + NA_ROWS else neg)
                    bias_ref[pat, h, GRID_W * j:GRID_W * (j + 1), 2 * GRID_W * kp:2 * GRID_W * (kp + 1)] = (
                        jnp.where(lane < GRID_W, halves[0], halves[1]))


def _na_attn_kernel(q_ref, k_ref, v_ref, kc_ref, vc_ref, tab_ref, *rest, rows, aliased):
    o_ref, bias_ref = rest[1:] if aliased else rest
    nq = NA_G * GRID_W
    nk = NA_WIN * GRID_W
    vc = vc_ref[...]
    _na_fill_bias(tab_ref, bias_ref)

    def group_scores(g):
        r0 = g * NA_G
        ws = jnp.clip(r0 - NA_ROWS // 2, 0, rows - NA_WIN)
        pat = lax.shift_right_logical(r0 - ws, 2)
        qoff = pl.multiple_of(r0 * GRID_W, nq)
        koff = pl.multiple_of(ws * GRID_W, GRID_W)
        q = q_ref[pl.ds(qoff, nq), :]
        kw = k_ref[pl.ds(koff, nk), :]
        scores = []
        for h in range(2):
            sl = slice(64 * h, 64 * (h + 1))
            scores.append((_dot_nt(q[:, sl], kw[:, sl]) + bias_ref[pat, h], _dot_nt(q[:, sl], kc_ref[:, sl])))
        return qoff, koff, scores

    def group_output(qoff, koff, scores):
        vw = v_ref[pl.ds(koff, nk), :]
        outs = []
        for s, sc in scores:
            m = jnp.maximum(jnp.max(s, axis=-1, keepdims=True), jnp.max(sc, axis=-1, keepdims=True))
            p = jnp.exp(s - m)
            pc = jnp.exp(sc - m)
            l = jnp.sum(p, axis=-1, keepdims=True) + jnp.sum(pc, axis=-1, keepdims=True)
            outs.append((_dot(p.astype(BF16), vw) + _dot(pc.astype(BF16), vc)) / l)
        o_ref[pl.ds(qoff, nq), :] = _pick_heads(outs[0], outs[1]).astype(BF16)

    def body(i, carry):
        pending = [group_scores(i * NA_UNROLL + u) for u in range(NA_UNROLL)]
        for qoff, koff, scores in pending:
            group_output(qoff, koff, scores)
        return carry

    lax.fori_loop(0, rows // (NA_G * NA_UNROLL), body, 0)


def _na_attn(proj, tab, nb, s_len, c_len, o_init):
    rows = s_len // GRID_W
    assert NA_G == 4 and rows % (NA_G * NA_UNROLL) == 0 and rows >= NA_WIN and NA_WIN % 2 == 0
    ctx_blk = nb * s_len // c_len
    alias_spec, alias_arg = _alias(o_init)
    return pl.pallas_call(
        functools.partial(_na_attn_kernel, rows=rows, aliased=o_init is not None),
        grid=(nb, 4),
        in_specs=[pl.BlockSpec((s_len, 128), lambda b, hp: (b, OFF_QA // 128 + hp)),
                  pl.BlockSpec((s_len, 128), lambda b, hp: (b, OFF_KA // 128 + hp)),
                  pl.BlockSpec((s_len, 128), lambda b, hp: (b, OFF_VA // 128 + hp)),
                  pl.BlockSpec((c_len, 128), lambda b, hp: (ctx_blk + b, OFF_KA // 128 + hp)),
                  pl.BlockSpec((c_len, 128), lambda b, hp: (ctx_blk + b, OFF_VA // 128 + hp)),
                  pl.BlockSpec((2, 2 * NA_ROWS - 1, GRID_W, 2 * GRID_W), lambda b, hp: (hp, 0, 0, 0))] + alias_spec,
        out_specs=pl.BlockSpec((s_len, 128), lambda b, hp: (b, hp)),
        out_shape=jax.ShapeDtypeStruct((nb * s_len if o_init is None else o_init.shape[0], 512), BF16),
        input_output_aliases={} if o_init is None else {6: 0},
        scratch_shapes=[pltpu.VMEM((3, 2, NA_G * GRID_W, NA_WIN * GRID_W), F32)],
        compiler_params=_cparams(("parallel", "parallel")),
        name="na_attn",
    )(proj, proj, proj, proj, proj, tab, *alias_arg)


def _na_bias_table(rpb):
    col = jnp.arange(GRID_W)
    cstart = jnp.clip(col - NA_COLS // 2, 0, GRID_W - NA_COLS)
    kc = jnp.arange(GRID_W)
    valid_c = (kc[None, :] >= cstart[:, None]) & (kc[None, :] < cstart[:, None] + NA_COLS)
    dcol = jnp.clip(kc[None, :] - col[:, None] + (NA_COLS - 1), 0, 2 * NA_COLS - 2)
    tab = jnp.where(valid_c[None, None], rpb[:, :, dcol], NEG).astype(F32)
    return jnp.concatenate([tab, tab], axis=-1)


def _local_kernel(up_p, up_c, up_n, cg_p, cg_c, cg_n, xc_p, xc_c, xc_n, bg_ref,
                  pw_ref, ps_ref, cw_ref, ob_ref, od_ref, *, lat_tiles, tiles_lat_seq, s_len, c_len):
    i = pl.program_id(0)
    is_lat = i < lat_tiles
    j = jnp.where(is_lat, i % tiles_lat_seq, 0)
    n_seq = jnp.where(is_lat, s_len, c_len)
    first = j == 0
    last = (j + 1) * TL == n_seq

    def ext(p_ref, c_ref, n_ref):
        p = jnp.where(first, 0.0, p_ref[...].astype(F32))
        n = jnp.where(last, 0.0, n_ref[...].astype(F32))
        return jnp.concatenate([p, c_ref[...].astype(F32), n], axis=0)

    z = ext(cg_p, cg_c, cg_n) * ext(xc_p, xc_c, xc_n)
    cw = cw_ref[...]
    y = (cw[0:1] * z[HALO - 1:HALO - 1 + TL] + cw[1:2] * z[HALO:HALO + TL]
         + cw[2:3] * z[HALO + 1:HALO + 1 + TL])
    od_ref[...] = (bg_ref[...].astype(F32) * y).astype(BF16)

    u = ext(up_p, up_c, up_n)
    t = j * TL + lax.broadcasted_iota(jnp.int32, (TL, 1), 0)
    ps = ps_ref[...]
    for g, w in enumerate(POOL_WINDOWS):
        sl = slice(POOL_GROUP * g, POOL_GROUP * (g + 1))
        ug = u[:, sl]
        acc = ug[HALO - w // 2:HALO - w // 2 + TL]
        for d in range(-w // 2 + 1, w // 2):
            acc = acc + ug[HALO + d:HALO + d + TL]
        cnt = (jnp.minimum(t - w // 2 + w, n_seq) - jnp.maximum(t - w // 2, 0)).astype(F32)
        pooled = acc / cnt - ug[HALO:HALO + TL]
        mixed = _dot(pooled.astype(BF16), pw_ref[g])
        ob_ref[:, sl] = (mixed * ps[:, sl]).astype(BF16)


def _local(proj, pool_w, pool_scale, conv_w, nb, s_len, c_len):
    r = proj.shape[0]
    lat_tiles = nb * s_len // TL
    hpt = TL // HALO
    nhalo = r // HALO

    def cur(off):
        return pl.BlockSpec((TL, 512), lambda i: (i, off // 512))

    def prev(off):
        return pl.BlockSpec((HALO, 512), lambda i: (jnp.maximum(i * hpt - 1, 0), off // 512))

    def nxt(off):
        return pl.BlockSpec((HALO, 512), lambda i: (jnp.minimum((i + 1) * hpt, nhalo - 1), off // 512))

    specs = []
    for off in (OFF_UP, OFF_CG, OFF_XC):
        specs += [prev(off), cur(off), nxt(off)]
    specs += [cur(OFF_BG),
              pl.BlockSpec((4, POOL_GROUP, POOL_GROUP), lambda i: (0, 0, 0)),
              pl.BlockSpec((1, 512), lambda i: (0, 0)),
              pl.BlockSpec((CONV_K, 512), lambda i: (0, 0))]
    return pl.pallas_call(
        functools.partial(_local_kernel, lat_tiles=lat_tiles, tiles_lat_seq=s_len // TL,
                          s_len=s_len, c_len=c_len),
        grid=(r // TL,),
        in_specs=specs,
        out_specs=[pl.BlockSpec((TL, 512), lambda i: (i, 0)), pl.BlockSpec((TL, 512), lambda i: (i, 0))],
        out_shape=[jax.ShapeDtypeStruct((r, 512), BF16), jax.ShapeDtypeStruct((r, 512), BF16)],
        compiler_params=_cparams(("parallel",)),
        name="local_mix",
    )(*([proj] * 10), pool_w, pool_scale, conv_w)


def _merge_kernel(oa_ref, ob_ref, oc_ref, od_ref, gate_ref, wb_ref, wo_ref, x_ref, gm_ref, g_ref, o_ref):
    merged = None
    for k, o in enumerate((oa_ref, ob_ref, oc_ref, od_ref)):
        proj = _dot(o[...], wb_ref[k])
        gk = jax.nn.sigmoid(gate_ref[:, 1024 * k:1024 * (k + 1)].astype(F32))
        merged = gk * proj if merged is None else merged + gk * proj
    y = _dot(merged.astype(BF16), wo_ref[...])
    o_ref[...] = x_ref[...] + gm_ref[...] * _rms(y, g_ref[...])


def _merge(oa, ob, oc, od, proj, wb, wo, xall, gate_mod, g, n_rows, tiles_per_batch):
    d = xall.shape[1]
    tm = TM
    tpb = tiles_per_batch * (TM // tm)
    row = lambda w: pl.BlockSpec((tm, w), lambda i: (i, 0))
    return pl.pallas_call(
        _merge_kernel,
        grid=(n_rows // tm,),
        in_specs=[row(512), row(512), row(512), row(512), row(4096),
                  pl.BlockSpec((4, 512, d), lambda i: (0, 0, 0)),
                  pl.BlockSpec((d, d), lambda i: (0, 0)),
                  row(d),
                  pl.BlockSpec((None, 1, d), lambda i: (i // tpb, 0, 0)),
                  pl.BlockSpec((1, d), lambda i: (0, 0))],
        out_specs=row(d),
        out_shape=jax.ShapeDtypeStruct((n_rows, d), F32),
        compiler_params=_cparams(("parallel",)),
        name="merge",
    )(oa, ob, oc, od, proj, wb, wo, xall, gate_mod, g.reshape(1, d))


def _ffn_kernel(x_ref, g2_ref, sh_ref, sc_ref, gm_ref, g3_ref, w1_ref, w3_ref, w2_ref, o_ref, *, n_split):
    x = x_ref[...]
    h = (_rms(x, g2_ref[...]) * (1.0 + sc_ref[...]) + sh_ref[...]).astype(BF16)
    tf = w1_ref.shape[1] // n_split
    acc = None
    for j in range(n_split):
        sl = slice(tf * j, tf * (j + 1))
        a = _dot(h, w1_ref[:, sl])
        b = _dot(h, w3_ref[:, sl])
        y = _dot((a * jax.nn.sigmoid(a) * b).astype(BF16), w2_ref[sl, :])
        acc = y if acc is None else acc + y
    o_ref[...] = x + gm_ref[...] * _rms(acc, g3_ref[...])


def _ffn(xall, g2, shift, scale, gate_mod, g3, w1, w3, w2, tiles_per_batch):
    r, d = xall.shape
    dff = w1.shape[1]
    mod = pl.BlockSpec((None, 1, d), lambda i: (i // tiles_per_batch, 0, 0))
    vec = pl.BlockSpec((1, d), lambda i: (0, 0))
    resident = lambda shape: pl.BlockSpec(shape, lambda i: (0, 0), pipeline_mode=pl.Buffered(1))
    return pl.pallas_call(
        functools.partial(_ffn_kernel, n_split=2),
        grid=(r // TM,),
        in_specs=[pl.BlockSpec((TM, d), lambda i: (i, 0)), vec, mod, mod, mod, vec,
                  resident((d, dff)), resident((d, dff)), resident((dff, d))],
        out_specs=pl.BlockSpec((TM, d), lambda i: (i, 0)),
        out_shape=jax.ShapeDtypeStruct((r, d), F32),
        compiler_params=_cparams(("parallel",)),
        name="ffn_dense",
    )(xall, g2.reshape(1, d), shift, scale, gate_mod, g3.reshape(1, d), w1, w3, w2)


def _router_kernel(x_ref, g2_ref, sh_ref, sc_ref, rt_ref, t_ref, idx_ref, w_ref):
    t = _rms(x_ref[...], g2_ref[...]) * (1.0 + sc_ref[...]) + sh_ref[...]
    t_ref[...] = t
    logits = lax.dot_general(rt_ref[...], t, (((1,), (1,)), ((), ())), preferred_element_type=F32,
                             precision=lax.Precision.HIGHEST)
    e = lax.broadcasted_iota(jnp.int32, logits.shape, 0).astype(F32)
    m1 = jnp.max(logits, axis=0, keepdims=True)
    i1 = jnp.min(jnp.where(logits == m1, e, float(N_EXPERTS)), axis=0, keepdims=True)
    rest = jnp.where(e == i1, -jnp.inf, logits)
    m2 = jnp.max(rest, axis=0, keepdims=True)
    i2 = jnp.min(jnp.where(rest == m2, e, float(N_EXPERTS)), axis=0, keepdims=True)
    ex = jnp.exp(m2 - m1)
    w1 = 1.0 / (1.0 + ex)
    idx_ref[0:1, :] = i1.astype(jnp.int32)
    idx_ref[1:2, :] = i2.astype(jnp.int32)
    w_ref[0:1, :] = w1
    w_ref[1:2, :] = ex * w1


def _router(x, g2, shift, scale, router_t, tiles_per_batch):
    t_rows, d = x.shape
    mod = pl.BlockSpec((None, 1, d), lambda i: (i // tiles_per_batch, 0, 0))
    return pl.pallas_call(
        _router_kernel,
        grid=(t_rows // TM,),
        in_specs=[pl.BlockSpec((TM, d), lambda i: (i, 0)),
                  pl.BlockSpec((1, d), lambda i: (0, 0)), mod, mod,
                  pl.BlockSpec((N_EXPERTS, d), lambda i: (0, 0))],
        out_specs=[pl.BlockSpec((TM, d), lambda i: (i, 0)),
                   pl.BlockSpec((TOP_K, TM), lambda i: (0, i)),
                   pl.BlockSpec((TOP_K, TM), lambda i: (0, i))],
        out_shape=[jax.ShapeDtypeStruct((t_rows, d), F32),
                   jax.ShapeDtypeStruct((TOP_K, t_rows), jnp.int32),
                   jax.ShapeDtypeStruct((TOP_K, t_rows), F32)],
        compiler_params=_cparams(("parallel",)),
        name="router",
    )(x, g2.reshape(1, d), shift, scale, router_t)


def _row_gather_start(ids_ref, src_ref, buf_ref, sem, n):
    def issue(r, c):
        pltpu.make_async_copy(src_ref.at[pl.ds(ids_ref[0, r], 1)], buf_ref.at[pl.ds(r, 1)], sem).start()
        return c

    lax.fori_loop(0, n, issue, 0, unroll=8)


def _row_gather_wait(src_ref, buf_ref, sem, n):
    pltpu.make_async_copy(src_ref.at[pl.ds(0, n)], buf_ref, sem).wait()


def _moe_ffn_kernel(eid_ref, nused_ref, ids_ref, ids_next_ref, t_ref, w1_ref, w3_ref, w2_ref, o_ref,
                    xbuf_ref, xs_ref, acc_ref, sems):
    i = pl.program_id(0)
    f = pl.program_id(1)
    n_f = pl.num_programs(1)
    slot = lax.rem(i, 2)

    def row_copy(ids, r, s):
        return pltpu.make_async_copy(t_ref.at[pl.ds(ids[0, r], 1)], xbuf_ref.at[s, pl.ds(r, 1)], sems.at[s])

    def all_rows(s):
        return pltpu.make_async_copy(t_ref.at[pl.ds(0, TM)], xbuf_ref.at[s], sems.at[s])

    @pl.when((i == 0) & (f == 0))
    def _():
        def issue(r, c):
            row_copy(ids_ref, r, 0).start()
            return c

        lax.fori_loop(0, TM, issue, 0)

    @pl.when(i < nused_ref[0])
    def _():
        @pl.when(f == 0)
        def _():
            all_rows(slot).wait()
            xs_ref[...] = xbuf_ref[slot].astype(BF16)

        for step in range(MOE_F_STEPS):
            @pl.when(f == step)
            def _():
                for r in range(step * (TM // MOE_F_STEPS), (step + 1) * (TM // MOE_F_STEPS)):
                    row_copy(ids_next_ref, r, 1 - slot).start()

        x = xs_ref[...]
        a = _dot(x, w1_ref[...])
        b = _dot(x, w3_ref[...])
        y = _dot((a * jax.nn.sigmoid(a) * b).astype(BF16), w2_ref[...])

        @pl.when(f == 0)
        def _():
            acc_ref[...] = y

        @pl.when(f == n_f - 1)
        def _():
            o_ref[...] = acc_ref[...] + y

    @pl.when((i + 1 == nused_ref[0]) & (f == n_f - 1))
    def _():
        all_rows(1 - slot).wait()

    @pl.when((i >= nused_ref[0]) & (f == n_f - 1))
    def _():
        o_ref[...] = jnp.zeros_like(o_ref)


def _moe_ffn(t, slot_tok, tile_eid, n_used, w1, w3, w2):
    n_tiles = slot_tok.shape[0]
    d = t.shape[1]
    dff = w1.shape[2]
    tf = dff // MOE_F_STEPS
    assert MOE_F_STEPS == 2, "the kernel stores the first step's partial sum and adds it in the second"
    wf = lambda i, f, nu: jnp.where(i < nu[0], f, MOE_F_STEPS - 1)
    ids = lambda nxt: pl.BlockSpec((None, 1, TM),
                                   lambda i, f, eid, nu: (jnp.minimum(i + nxt, n_tiles - 1), 0, 0),
                                   memory_space=pltpu.SMEM)
    grid_spec = pltpu.PrefetchScalarGridSpec(
        num_scalar_prefetch=2,
        grid=(n_tiles, MOE_F_STEPS),
        in_specs=[ids(0), ids(1),
                  pl.BlockSpec(memory_space=pl.ANY),
                  pl.BlockSpec((None, d, tf), lambda i, f, eid, nu: (eid[i], 0, wf(i, f, nu))),
                  pl.BlockSpec((None, d, tf), lambda i, f, eid, nu: (eid[i], 0, wf(i, f, nu))),
                  pl.BlockSpec((None, tf, d), lambda i, f, eid, nu: (eid[i], wf(i, f, nu), 0))],
        out_specs=pl.BlockSpec((TM, d), lambda i, f, eid, nu: (i, 0)),
        scratch_shapes=[pltpu.VMEM((2, TM, d), F32), pltpu.VMEM((TM, d), BF16), pltpu.VMEM((TM, d), F32),
                        pltpu.SemaphoreType.DMA((2,))])
    return pl.pallas_call(
        _moe_ffn_kernel,
        grid_spec=grid_spec,
        out_shape=jax.ShapeDtypeStruct((n_tiles * TM, d), F32),
        compiler_params=_cparams(("arbitrary", "arbitrary")),
        name="moe_ffn",
    )(tile_eid, n_used, slot_tok, slot_tok, t, w1, w3, w2)


def _combine_kernel(p0_ref, p1_ref, p0n_ref, p1n_ref, ys_ref, w_ref, x_ref, gm_ref, g3_ref, o_ref,
                    b0_ref, b1_ref, sems):
    i = pl.program_id(0)
    slot = lax.rem(i, 2)

    def start(s, ids0, ids1):
        _row_gather_start(ids0, ys_ref, b0_ref.at[s], sems.at[0, s], TM)
        _row_gather_start(ids1, ys_ref, b1_ref.at[s], sems.at[1, s], TM)

    def wait(s):
        _row_gather_wait(ys_ref, b0_ref.at[s], sems.at[0, s], TM)
        _row_gather_wait(ys_ref, b1_ref.at[s], sems.at[1, s], TM)

    @pl.when(i == 0)
    def _():
        start(0, p0_ref, p1_ref)

    start(1 - slot, p0n_ref, p1n_ref)
    wait(slot)
    w = w_ref[...]
    y = w[:, 0:1] * b0_ref[slot] + w[:, 1:2] * b1_ref[slot]
    o_ref[...] = x_ref[...] + gm_ref[...] * _rms(y, g3_ref[...])

    @pl.when(i == pl.num_programs(0) - 1)
    def _():
        wait(1 - slot)


def _combine(pos0, pos1, ys, wcol, x, gate_mod, g3, tiles_per_batch):
    t_rows, d = x.shape
    n_tiles = t_rows // TM
    ids = lambda nxt: pl.BlockSpec((None, 1, TM), lambda i: (jnp.minimum(i + nxt, n_tiles - 1), 0, 0),
                                   memory_space=pltpu.SMEM)
    return pl.pallas_call(
        _combine_kernel,
        grid=(n_tiles,),
        in_specs=[ids(0), ids(0), ids(1), ids(1),
                  pl.BlockSpec(memory_space=pl.ANY),
                  pl.BlockSpec((TM, TOP_K), lambda i: (i, 0)),
                  pl.BlockSpec((TM, d), lambda i: (i, 0)),
                  pl.BlockSpec((None, 1, d), lambda i: (i // tiles_per_batch, 0, 0)),
                  pl.BlockSpec((1, d), lambda i: (0, 0))],
        out_specs=pl.BlockSpec((TM, d), lambda i: (i, 0)),
        out_shape=jax.ShapeDtypeStruct((t_rows, d), F32),
        scratch_shapes=[pltpu.VMEM((2, TM, d), F32), pltpu.VMEM((2, TM, d), F32),
                        pltpu.SemaphoreType.DMA((2, 2))],
        compiler_params=_cparams(("arbitrary",)),
        name="moe_combine",
    )(pos0, pos1, pos0, pos1, ys, wcol, x, gate_mod, g3.reshape(1, d))


def _routing_tables(top_i, t_rows):
    n_assign = TOP_K * t_rows
    n_tiles = n_assign // TM + N_EXPERTS
    e_flat = top_i.reshape(n_assign)
    onehot = (e_flat[:, None] == jnp.arange(N_EXPERTS)[None, :]).astype(jnp.int32)
    csum = jnp.cumsum(onehot, axis=0)
    counts = csum[-1]
    rank = jnp.sum((csum - onehot) * onehot, axis=1)
    tiles_e = (counts + TM - 1) // TM
    tile_end = jnp.cumsum(tiles_e)
    tile_start = tile_end - tiles_e
    slot = (tile_start * TM)[e_flat] + rank
    tok = jnp.tile(jnp.arange(t_rows, dtype=jnp.int32), TOP_K)
    slot_tok = jnp.zeros((n_tiles * TM,), jnp.int32).at[slot].set(tok)
    n_used = tile_end[-1]
    tile_ids = jnp.arange(n_tiles)
    tile_eid = jnp.sum(tile_ids[:, None] >= tile_end[None, :], axis=1)
    last_eid = jnp.sum(n_used - 1 >= tile_end)
    tile_eid = jnp.where(tile_ids < n_used, tile_eid, last_eid).astype(jnp.int32)
    pos = slot.reshape(TOP_K, t_rows).astype(jnp.int32)
    return (slot_tok.reshape(n_tiles, 1, TM), tile_eid, n_used.reshape(1).astype(jnp.int32),
            pos[0].reshape(t_rows // TM, 1, TM), pos[1].reshape(t_rows // TM, 1, TM))


def _prep_w_in(w):
    d = w.shape[0]
    w = w.astype(BF16)
    q_a, k_a, v_a, u_p, c_q, c_kv, k_r, b_g, c_g, x_c, gate = jnp.split(w, IN_SPLITS, axis=1)
    z = lambda n: jnp.zeros((d, n), w.dtype)
    kr_e, kr_o = k_r[:, 0::2], k_r[:, 1::2]
    cols = [gate, q_a * NA_SCALE, k_a, v_a, u_p, b_g, c_g, x_c, c_q, z(128), c_kv,
            z(64), kr_e, kr_o, -kr_o, kr_e, z(128)]
    return jnp.concatenate(cols, axis=1).astype(BF16)


def _prep_w_uq(w_uq):
    wq = w_uq.reshape(MLA_Q_LORA, MLA_HEADS, MLA_NOPE + MLA_ROPE)
    nope, r = wq[..., :MLA_NOPE], wq[..., MLA_NOPE:]
    re, ro = r[..., 0::2], r[..., 1::2]
    ext = jnp.concatenate([nope, re, ro, -ro, re], axis=-1).reshape(MLA_Q_LORA, MLA_HEADS * 128)
    return jnp.pad(ext, ((0, 512 - MLA_Q_LORA), (0, 0))).T.astype(BF16)


def _prep_w_ukv(w_ukv):
    wkv = w_ukv.reshape(MLA_KV_LORA, MLA_HEADS, MLA_NOPE + MLA_V)
    zeros = jnp.zeros((MLA_KV_LORA, MLA_HEADS, 64), w_ukv.dtype)
    kn = jnp.concatenate([wkv[..., :MLA_NOPE], zeros], axis=-1).reshape(MLA_KV_LORA, MLA_HEADS * 128)
    vt = jnp.concatenate([wkv[..., MLA_NOPE:], zeros[..., :VROWS - MLA_V]], axis=-1)
    return kn.astype(BF16), vt.reshape(MLA_KV_LORA, MLA_HEADS * VROWS).T.astype(BF16)


def _rope_tables(nb, s_len, n_ctx_rows):
    pos = jnp.arange(s_len)
    row = (pos // GRID_W).astype(F32)
    col = (pos % GRID_W).astype(F32)
    n_pairs = MLA_ROPE // 4
    inv_freq = ROPE_BASE ** (-jnp.arange(n_pairs, dtype=F32) / n_pairs)
    ang = jnp.concatenate([row[:, None] * inv_freq, col[:, None] * inv_freq], axis=-1)
    cos = jnp.concatenate([jnp.tile(jnp.cos(ang), (nb, 1)), jnp.ones((n_ctx_rows, 16), F32)], axis=0)
    sin = jnp.concatenate([jnp.tile(jnp.sin(ang), (nb, 1)), jnp.zeros((n_ctx_rows, 16), F32)], axis=0)
    r = cos.shape[0]
    cos2 = jnp.concatenate([cos, cos], axis=1)
    sin2 = jnp.concatenate([sin, sin], axis=1)
    pad = lambda t, lead: jnp.concatenate([lead, t, jnp.zeros((r, 32), F32)], axis=1)
    return (pad(cos2, jnp.ones((r, 64), F32)), pad(sin2, jnp.zeros((r, 64), F32)), cos2.T, sin2.T)


def kernel(x, c, ctx, c_ctx, w_ada, b_ada, g_norm, w_in, na_rpb, pool_w, pool_scale, mla_g_q, mla_w_uq,
           mla_g_kv, mla_w_ukv, conv_w, w_branch, w_out, ffn_w1, ffn_w3, ffn_w2, moe_router, moe_w1,
           moe_w3, moe_w2):
    nb, s_len, d = x.shape
    c_len = ctx.shape[1]
    depth = w_in.shape[0]
    t_rows = nb * s_len
    n_ctx = nb * c_len
    assert s_len % TM == 0 and n_ctx == TM and s_len % GRID_W == 0 and t_rows % c_len == 0
    assert c_len == KC and c_len == TL
    tpb = s_len // TM
    rows = s_len // GRID_W

    xall = jnp.concatenate([x.reshape(t_rows, d), ctx.reshape(n_ctx, d)], axis=0)
    c8 = jnp.zeros((8, d), F32).at[:nb].set(c).at[nb].set(c_ctx)
    cos_t, sin_t, cos_tt, sin_tt = _rope_tables(nb, s_len, n_ctx)

    for l in range(depth):
        last = l == depth - 1
        mod = _adaln(c8, w_ada[l], b_ada[l]).reshape(8, 6, 1, d)
        mods = [mod[:, k] for k in range(6)]

        proj = _inproj(xall, g_norm[l, 0], mods[0], mods[1], _prep_w_in(w_in[l]), tpb)

        wkn, wvt = _prep_w_ukv(mla_w_ukv[l])
        gq = jnp.pad(mla_g_q[l], (0, 512 - MLA_Q_LORA)).reshape(1, 512)
        qt_m, k_m, vt_m = _mla_prep(proj, gq, mla_g_kv[l].reshape(1, MLA_KV_LORA), _prep_w_uq(mla_w_uq[l]),
                                    wkn, wvt, cos_t, sin_t, cos_tt, sin_tt)
        n_rows = t_rows if last else t_rows + n_ctx
        o_init = None if last else jnp.zeros((n_rows, 512), BF16)
        o_c = _mla_attn(qt_m, k_m, vt_m, nb, s_len, c_len, o_init)
        o_a = _na_attn(proj, _na_bias_table(na_rpb[l]), nb, s_len, c_len, o_init)
        o_b, o_d = _local(proj, pool_w[l].astype(BF16), pool_scale[l].reshape(1, 512), conv_w[l],
                          nb, s_len, c_len)

        wb = w_branch[l].astype(BF16)
        wo = w_out[l].astype(BF16)
        if not last:
            ctx_blk = t_rows // c_len
            o_c = _mla_ctx_attn(qt_m, k_m, vt_m, o_c, nb, t_rows, c_len)
            o_a = _ctx_attn(proj, proj, proj, o_a, nb, c_len, ctx_blk, 64, OFF_QA, OFF_KA, OFF_VA, "na_ctx_attn")
        xall = _merge(o_a, o_b, o_c, o_d, proj, wb, wo, xall, mods[2], g_norm[l, 1], n_rows, tpb)

        if l % 2 == 0:
            j = l // 2
            xall = _ffn(xall, g_norm[l, 2], mods[3], mods[4], mods[5], g_norm[l, 3],
                        ffn_w1[j].astype(BF16), ffn_w3[j].astype(BF16), ffn_w2[j].astype(BF16), tpb)
        else:
            j = l // 2
            assert last, "context rows are not routed through the experts"
            t_f32, top_i, top_w = _router(xall, g_norm[l, 2], mods[3], mods[4], moe_router[j].T, tpb)
            slot_tok, tile_eid, n_used, pos0, pos1 = _routing_tables(top_i, t_rows)
            ys = _moe_ffn(t_f32, slot_tok, tile_eid, n_used, moe_w1[j].astype(BF16), moe_w3[j].astype(BF16),
                          moe_w2[j].astype(BF16))
            xall = _combine(pos0, pos1, ys, top_w.T, xall, mods[5], g_norm[l, 3], tpb)

    return xall[:t_rows].reshape(nb, s_len, d)
```

```python
import functools

import numpy as np
import jax
import jax.numpy as jnp
from jax import lax
from jax.experimental import pallas as pl
from jax.experimental.pallas import tpu as pltpu

F32 = jnp.float32
BF16 = jnp.bfloat16

GRID_W = 64
N_BRANCH = 4
BRANCH_W = 512
EPS = 1e-6
NA_HEADS = 8
NA_HEAD_DIM = 64
NA_ROWS = 8
NA_COLS = 16
NA_G = 4
NA_WIN = NA_G + NA_ROWS
NA_UNROLL = 2
POOL_WINDOWS = (2, 4, 8, 16)
POOL_GROUP = 128
MLA_HEADS = 8
MLA_Q_LORA = 384
MLA_KV_LORA = 256
MLA_NOPE = 64
MLA_ROPE = 32
MLA_V = 64
ROPE_BASE = 10000.0
CONV_K = 3
N_EXPERTS = 8
TOP_K = 2

IN_SIZES = (512, 512, 512, 512, MLA_Q_LORA, MLA_KV_LORA, MLA_ROPE, 512, 512, 512, 4096)
IN_SPLITS = tuple(int(s) for s in np.cumsum(IN_SIZES)[:-1])

OFF_GATE = 0
OFF_QA = 4096
OFF_KA = 4608
OFF_VA = 5120
OFF_UP = 5632
OFF_BG = 6144
OFF_CG = 6656
OFF_XC = 7168
OFF_CQ = 7680
OFF_CKV = 8192
OFF_KR = 8448
N_IN = 8704

NA_SCALE = NA_HEAD_DIM ** -0.5
MLA_SCALE = (MLA_NOPE + MLA_ROPE) ** -0.5
LOG2E = 1.4426950408889634
NEG = -1e30

VMEM_LIMIT = 52 * 1024 * 1024
TM = 512
TL = 256
HALO = 16
KC = 256
VROWS = 80
MOE_F_STEPS = 2


def _cparams(sem):
    return pltpu.CompilerParams(dimension_semantics=sem, vmem_limit_bytes=VMEM_LIMIT)


def _rms(xf, g):
    ms = jnp.mean(xf * xf, axis=-1, keepdims=True)
    return xf * lax.rsqrt(ms + EPS) * g


def _dot(a, b):
    return jnp.dot(a, b, preferred_element_type=F32)


def _dot_nt(a, b):
    return lax.dot_general(a, b, (((1,), (1,)), ((), ())), preferred_element_type=F32)


def _adaln_kernel(c_ref, w_ref, b_ref, o_ref):
    c = c_ref[...]
    sc = c * jax.nn.sigmoid(c)
    o_ref[...] = jnp.dot(sc, w_ref[...], preferred_element_type=F32,
                         precision=lax.Precision.HIGHEST) + b_ref[...]


def _adaln(c8, w, b):
    d = c8.shape[1]
    n = w.shape[1]
    tn = 1536
    return pl.pallas_call(
        _adaln_kernel,
        grid=(n // tn,),
        in_specs=[pl.BlockSpec((8, d), lambda j: (0, 0)),
                  pl.BlockSpec((d, tn), lambda j: (0, j)),
                  pl.BlockSpec((1, tn), lambda j: (0, j))],
        out_specs=pl.BlockSpec((8, tn), lambda j: (0, j)),
        out_shape=jax.ShapeDtypeStruct((8, n), F32),
        compiler_params=_cparams(("arbitrary",)),
        name="adaln",
    )(c8, w, b.reshape(1, n))


def _inproj_kernel(x_ref, g_ref, sh_ref, sc_ref, w_ref, o_ref, *, n_split):
    h = (_rms(x_ref[...], g_ref[...]) * (1.0 + sc_ref[...]) + sh_ref[...]).astype(BF16)
    tn = w_ref.shape[1] // n_split
    for j in range(n_split):
        o_ref[:, tn * j:tn * (j + 1)] = _dot(h, w_ref[:, tn * j:tn * (j + 1)]).astype(BF16)


def _inproj(xall, g, shift, scale, w_p, tiles_per_batch):
    r, d = xall.shape
    n = w_p.shape[1]
    mod_spec = pl.BlockSpec((None, 1, d), lambda i: (i // tiles_per_batch, 0, 0))
    return pl.pallas_call(
        functools.partial(_inproj_kernel, n_split=4),
        grid=(r // TM,),
        in_specs=[pl.BlockSpec((TM, d), lambda i: (i, 0)),
                  pl.BlockSpec((1, d), lambda i: (0, 0)),
                  mod_spec, mod_spec,
                  pl.BlockSpec((d, n), lambda i: (0, 0), pipeline_mode=pl.Buffered(1))],
        out_specs=pl.BlockSpec((TM, n), lambda i: (i, 0)),
        out_shape=jax.ShapeDtypeStruct((r, n), BF16),
        compiler_params=_cparams(("parallel",)),
        name="inproj",
    )(xall, g.reshape(1, d), shift, scale, w_p)


def _mla_prep_kernel(cq_ref, ckv_ref, kr_ref, gq_ref, gkv_ref, wqt_ref, wkn_ref, wvt_ref, ones_ref,
                     cos_ref, sin_ref, cost_ref, sint_ref, qt_out, k_out, vt_out):
    cq = cq_ref[...].astype(F32)
    ms = jnp.sum(cq * cq, axis=-1, keepdims=True) * (1.0 / MLA_Q_LORA)
    cqn = (cq * lax.rsqrt(ms + EPS) * gq_ref[...]).astype(BF16)
    qt = _dot_nt(wqt_ref[...], cqn) * (MLA_SCALE * LOG2E)
    cost = cost_ref[...]
    sint = sint_ref[...]
    for h in range(MLA_HEADS):
        b = 128 * h
        qt_out[b:b + 64, :] = qt[b:b + 64].astype(BF16)
        qt_out[b + 64:b + 96, :] = (qt[b + 64:b + 96] * cost + qt[b + 96:b + 128] * sint).astype(BF16)
        qt_out[b + 96:b + 128, :] = jnp.zeros((32, TM), BF16)

    ckvn = _rms(ckv_ref[...].astype(F32), gkv_ref[...]).astype(BF16)
    kn = _dot(ckvn, wkn_ref[...])
    kseg = kr_ref[...].astype(F32)
    krt = kseg * cos_ref[...] + pltpu.roll(kseg, 96, 1) * sin_ref[...]
    for h in range(MLA_HEADS):
        sl = slice(128 * h, 128 * (h + 1))
        k_out[:, sl] = (kn[:, sl] + krt).astype(BF16)

    vt = (_dot_nt(wvt_ref[...], ckvn) + ones_ref[...]).astype(BF16)
    for j in range(TM // KC):
        vt_out[j] = vt[:, KC * j:KC * (j + 1)]


def _mla_prep(proj, gq, gkv, wqt, wkn, wvt, cos_t, sin_t, cos_tt, sin_tt):
    r = proj.shape[0]
    nv = MLA_HEADS * VROWS
    ones_col = ((jnp.arange(nv) % VROWS) >= MLA_V).astype(F32).reshape(nv, 1)
    full = lambda shape: pl.BlockSpec(shape, lambda i: (0, 0))
    return pl.pallas_call(
        _mla_prep_kernel,
        grid=(r // TM,),
        in_specs=[pl.BlockSpec((TM, 512), lambda i: (i, OFF_CQ // 512)),
                  pl.BlockSpec((TM, 256), lambda i: (i, OFF_CKV // 256)),
                  pl.BlockSpec((TM, 128), lambda i: (i, OFF_KR // 128)),
                  full((1, 512)), full((1, 256)),
                  full((1024, 512)), full((256, 1024)), full((nv, 256)), full((nv, 1)),
                  pl.BlockSpec((TM, 128), lambda i: (i, 0)),
                  pl.BlockSpec((TM, 128), lambda i: (i, 0)),
                  pl.BlockSpec((32, TM), lambda i: (0, i)),
                  pl.BlockSpec((32, TM), lambda i: (0, i))],
        out_specs=[pl.BlockSpec((1024, TM), lambda i: (0, i)),
                   pl.BlockSpec((TM, 1024), lambda i: (i, 0)),
                   pl.BlockSpec((TM // KC, nv, KC), lambda i: (i, 0, 0))],
        out_shape=[jax.ShapeDtypeStruct((1024, r), BF16),
                   jax.ShapeDtypeStruct((r, 1024), BF16),
                   jax.ShapeDtypeStruct((r // KC, nv, KC), BF16)],
        compiler_params=_cparams(("parallel",)),
        name="mla_prep",
    )(proj, proj, proj, gq, gkv, wqt, wkn, wvt, ones_col, cos_t, sin_t, cos_tt, sin_tt)


def _alias(o_init):
    return ([], []) if o_init is None else ([pl.BlockSpec(memory_space=pl.ANY)], [o_init])


def _pick_heads(o0, o1):
    lane = lax.broadcasted_iota(jnp.int32, o0.shape, 1)
    return jnp.where(lane < 64, o0, o1)


def _mla_update(s, vt, m_old, acc_ref):
    m_new = jnp.maximum(m_old, jnp.max(s, axis=0, keepdims=True))
    alpha = jnp.exp2(m_old - m_new)
    p = jnp.exp2(s - m_new).astype(BF16)
    acc_ref[...] = alpha * acc_ref[...] + _dot(vt, p)
    return m_new


def _mla_finish(acc0, acc1):
    def rows_to_lanes(acc):
        return jnp.concatenate([acc, jnp.zeros((128 - VROWS, acc.shape[1]), F32)], axis=0).T
    a0 = rows_to_lanes(acc0)
    a1 = pltpu.roll(rows_to_lanes(acc1), 64, 1)
    return _pick_heads(a0 / a0[:, 64:65], a1 / a1[:, 0:1]).astype(BF16)


def _mla_attn_kernel(qt_ref, k_ref, vt_ref, kc_ref, vtc_ref, *rest, cpi, aliased):
    o_ref, *acc_refs = rest[1:] if aliased else rest
    nt = qt_ref.shape[1] // 256
    units = [(t, h) for h in range(2) for t in range(nt)]
    qts = [qt_ref[128 * h:128 * (h + 1), 256 * t:256 * (t + 1)] for t, h in units]
    n_chunks = k_ref.shape[0] // KC
    for acc_ref in acc_refs:
        acc_ref[...] = jnp.zeros_like(acc_ref)

    def scores(kget):
        return tuple(_dot(kget(h), qts[u]) for u, (t, h) in enumerate(units))

    def latent_keys(j):
        off = pl.multiple_of(j * KC, KC)
        return lambda h: k_ref[pl.ds(off, KC), 128 * h:128 * (h + 1)]

    def step(s_cur, vget, ms, kget_next):
        s_next, ms_new = [], []
        for u, (t, h) in enumerate(units):
            if kget_next is not None:
                s_next.append(_dot(kget_next(h), qts[u]))
            ms_new.append(_mla_update(s_cur[u], vget(h), ms[u], acc_refs[u]))
        return tuple(s_next), tuple(ms_new)

    ms = tuple(jnp.full((1, 256), -jnp.inf, F32) for _ in units)
    s_ctx = scores(lambda h: kc_ref[:, 128 * h:128 * (h + 1)])
    s_cur, ms = step(s_ctx, lambda h: vtc_ref[0, VROWS * h:VROWS * (h + 1), :], ms, latent_keys(0))

    def values(j):
        return lambda h: vt_ref[j, VROWS * h:VROWS * (h + 1), :]

    def body(i, carry):
        ms, s_cur = carry
        for c in range(cpi):
            j = i * cpi + c
            s_cur, ms = step(s_cur, values(j), ms, latent_keys(j + 1))
        return ms, s_cur

    n_iter = n_chunks // cpi - 1
    ms, s_cur = lax.fori_loop(0, n_iter, body, (ms, s_cur))
    for j in range(n_iter * cpi, n_chunks):
        s_cur, ms = step(s_cur, values(j), ms, latent_keys(j + 1) if j + 1 < n_chunks else None)
    for t in range(nt):
        o_ref[256 * t:256 * (t + 1), :] = _mla_finish(acc_refs[units.index((t, 0))][...],
                                                      acc_refs[units.index((t, 1))][...])


def _mla_attn(qt, k, vt, nb, s_len, c_len, o_init):
    tq = min(1024, s_len)
    nq = s_len // tq
    t_rows = nb * s_len
    cpi = s_len // KC
    alias_spec, alias_arg = _alias(o_init)
    return pl.pallas_call(
        functools.partial(_mla_attn_kernel, cpi=cpi, aliased=o_init is not None),
        grid=(nb, 4, nq),
        in_specs=[pl.BlockSpec((256, tq), lambda b, hp, i: (hp, b * nq + i)),
                  pl.BlockSpec((s_len, 256), lambda b, hp, i: (b, hp)),
                  pl.BlockSpec((s_len // KC, 2 * VROWS, KC), lambda b, hp, i: (b, hp, 0)),
                  pl.BlockSpec((c_len, 256), lambda b, hp, i: (t_rows // c_len + b, hp)),
                  pl.BlockSpec((1, 2 * VROWS, KC), lambda b, hp, i: (t_rows // KC + b, hp, 0))] + alias_spec,
        out_specs=pl.BlockSpec((tq, 128), lambda b, hp, i: (b * nq + i, hp)),
        out_shape=jax.ShapeDtypeStruct((t_rows if o_init is None else o_init.shape[0], 512), BF16),
        input_output_aliases={} if o_init is None else {5: 0},
        scratch_shapes=[pltpu.VMEM((VROWS, 256), F32)] * (2 * (tq // 256)),
        compiler_params=_cparams(("parallel", "parallel", "arbitrary")),
        name="mla_attn",
    )(qt, k, vt, k, vt, *alias_arg)


def _mla_ctx_attn_kernel(qt_ref, kc_ref, vtc_ref, o_full_ref, o_ref):
    del o_full_ref
    outs = []
    for h in range(2):
        s = _dot(kc_ref[:, 128 * h:128 * (h + 1)], qt_ref[128 * h:128 * (h + 1), :])
        p = jnp.exp2(s - jnp.max(s, axis=0, keepdims=True)).astype(BF16)
        outs.append(_dot(vtc_ref[0, VROWS * h:VROWS * (h + 1), :], p))
    o_ref[...] = _mla_finish(outs[0], outs[1])


def _mla_ctx_attn(qt, k, vt, o_full, nb, t_rows, c_len):
    blk0 = t_rows // c_len
    return pl.pallas_call(
        _mla_ctx_attn_kernel,
        grid=(nb, 4),
        in_specs=[pl.BlockSpec((256, c_len), lambda b, hp: (hp, blk0 + b)),
                  pl.BlockSpec((c_len, 256), lambda b, hp: (blk0 + b, hp)),
                  pl.BlockSpec((1, 2 * VROWS, KC), lambda b, hp: (blk0 + b, hp, 0)),
                  pl.BlockSpec(memory_space=pl.ANY)],
        out_specs=pl.BlockSpec((c_len, 128), lambda b, hp: (blk0 + b, hp)),
        out_shape=jax.ShapeDtypeStruct(o_full.shape, BF16),
        input_output_aliases={3: 0},
        compiler_params=_cparams(("parallel", "parallel")),
        name="mla_ctx_attn",
    )(qt, k, vt, o_full)


def _ctx_attn_kernel(q_ref, k_ref, v_ref, o_full_ref, o_ref, *, dqk):
    del o_full_ref
    v = v_ref[...]
    outs = []
    for h in range(2):
        sl = slice(dqk * h, dqk * (h + 1))
        s = _dot_nt(q_ref[:, sl], k_ref[:, sl])
        m = jnp.max(s, axis=-1, keepdims=True)
        p = jnp.exp(s - m)
        l = jnp.sum(p, axis=-1, keepdims=True)
        outs.append(_dot(p.astype(BF16), v) / l)
    o_ref[...] = _pick_heads(outs[0], outs[1]).astype(BF16)


def _ctx_attn(q, k, v, o_full, nb, c_len, row_blk0, dqk, qcol, kcol, vcol, name):
    w = 2 * dqk
    return pl.pallas_call(
        functools.partial(_ctx_attn_kernel, dqk=dqk),
        grid=(nb, 4),
        in_specs=[pl.BlockSpec((c_len, w), lambda b, hp: (row_blk0 + b, qcol // w + hp)),
                  pl.BlockSpec((c_len, w), lambda b, hp: (row_blk0 + b, kcol // w + hp)),
                  pl.BlockSpec((c_len, 128), lambda b, hp: (row_blk0 + b, vcol // 128 + hp)),
                  pl.BlockSpec(memory_space=pl.ANY)],
        out_specs=pl.BlockSpec((c_len, 128), lambda b, hp: (row_blk0 + b, hp)),
        out_shape=jax.ShapeDtypeStruct(o_full.shape, BF16),
        input_output_aliases={3: 0},
        compiler_params=_cparams(("parallel", "parallel")),
        name=name,
    )(q, k, v, o_full)


def _na_fill_bias(tab_ref, bias_ref):
    lane = lax.broadcasted_iota(jnp.int32, (GRID_W, 2 * GRID_W), 1)
    neg = jnp.full((GRID_W, 2 * GRID_W), NEG, F32)
    for pat in range(3):
        for j in range(NA_G):
            kr_lo = (0, j, NA_G)[pat]
            for h in range(2):
                for kp in range(NA_WIN // 2):
                    halves = []
                    for kr in (2 * kp, 2 * kp + 1):
                        d_row = kr - NA_G * pat - j + NA_ROWS - 1
                        halves.append(tab_ref[h, d_row] if kr_lo <= kr < kr_lo + NA_ROWS else neg)
                    bias_ref[pat, h, GRID_W * j:GRID_W * (j + 1), 2 * GRID_W * kp:2 * GRID_W * (kp + 1)] = (
                        jnp.where(lane < GRID_W, halves[0], halves[1]))


def _na_attn_kernel(q_ref, k_ref, v_ref, kc_ref, vc_ref, tab_ref, *rest, rows, aliased):
    o_ref, bias_ref = rest[1:] if aliased else rest
    nq = NA_G * GRID_W
    nk = NA_WIN * GRID_W
    vc = vc_ref[...]
    _na_fill_bias(tab_ref, bias_ref)

    def group_scores(g):
        r0 = g * NA_G
        ws = jnp.clip(r0 - NA_ROWS // 2, 0, rows - NA_WIN)
        pat = lax.shift_right_logical(r0 - ws, 2)
        qoff = pl.multiple_of(r0 * GRID_W, nq)
        koff = pl.multiple_of(ws * GRID_W, GRID_W)
        q = q_ref[pl.ds(qoff, nq), :]
        kw = k_ref[pl.ds(koff, nk), :]
        scores = []
        for h in range(2):
            sl = slice(64 * h, 64 * (h + 1))
            scores.append((_dot_nt(q[:, sl], kw[:, sl]) + bias_ref[pat, h], _dot_nt(q[:, sl], kc_ref[:, sl])))
        return qoff, koff, scores

    def group_output(qoff, koff, scores):
        vw = v_ref[pl.ds(koff, nk), :]
        outs = []
        for s, sc in scores:
            m = jnp.maximum(jnp.max(s, axis=-1, keepdims=True), jnp.max(sc, axis=-1, keepdims=True))
            p = jnp.exp(s - m)
            pc = jnp.exp(sc - m)
            l = jnp.sum(p, axis=-1, keepdims=True) + jnp.sum(pc, axis=-1, keepdims=True)
            outs.append((_dot(p.astype(BF16), vw) + _dot(pc.astype(BF16), vc)) / l)
        o_ref[pl.ds(qoff, nq), :] = _pick_heads(outs[0], outs[1]).astype(BF16)

    def body(i, carry):
        pending = [group_scores(i * NA_UNROLL + u) for u in range(NA_UNROLL)]
        for qoff, koff, scores in pending:
            group_output(qoff, koff, scores)
        return carry

    lax.fori_loop(0, rows // (NA_G * NA_UNROLL), body, 0)


def _na_attn(proj, tab, nb, s_len, c_len, o_init):
    rows = s_len // GRID_W
    assert NA_G == 4 and rows % (NA_G * NA_UNROLL) == 0 and rows >= NA_WIN and NA_WIN % 2 == 0
    ctx_blk = nb * s_len // c_len
    alias_spec, alias_arg = _alias(o_init)
    return pl.pallas_call(
        functools.partial(_na_attn_kernel, rows=rows, aliased=o_init is not None),
        grid=(nb, 4),
        in_specs=[pl.BlockSpec((s_len, 128), lambda b, hp: (b, OFF_QA // 128 + hp)),
                  pl.BlockSpec((s_len, 128), lambda b, hp: (b, OFF_KA // 128 + hp)),
                  pl.BlockSpec((s_len, 128), lambda b, hp: (b, OFF_VA // 128 + hp)),
                  pl.BlockSpec((c_len, 128), lambda b, hp: (ctx_blk + b, OFF_KA // 128 + hp)),
                  pl.BlockSpec((c_len, 128), lambda b, hp: (ctx_blk + b, OFF_VA // 128 + hp)),
                  pl.BlockSpec((2, 2 * NA_ROWS - 1, GRID_W, 2 * GRID_W), lambda b, hp: (hp, 0, 0, 0))] + alias_spec,
        out_specs=pl.BlockSpec((s_len, 128), lambda b, hp: (b, hp)),
        out_shape=jax.ShapeDtypeStruct((nb * s_len if o_init is None else o_init.shape[0], 512), BF16),
        input_output_aliases={} if o_init is None else {6: 0},
        scratch_shapes=[pltpu.VMEM((3, 2, NA_G * GRID_W, NA_WIN * GRID_W), F32)],
        compiler_params=_cparams(("parallel", "parallel")),
        name="na_attn",
    )(proj, proj, proj, proj, proj, tab, *alias_arg)


def _na_bias_table(rpb):
    col = jnp.arange(GRID_W)
    cstart = jnp.clip(col - NA_COLS // 2, 0, GRID_W - NA_COLS)
    kc = jnp.arange(GRID_W)
    valid_c = (kc[None, :] >= cstart[:, None]) & (kc[None, :] < cstart[:, None] + NA_COLS)
    dcol = jnp.clip(kc[None, :] - col[:, None] + (NA_COLS - 1), 0, 2 * NA_COLS - 2)
    tab = jnp.where(valid_c[None, None], rpb[:, :, dcol], NEG).astype(F32)
    return jnp.concatenate([tab, tab], axis=-1)


def _local_kernel(up_p, up_c, up_n, cg_p, cg_c, cg_n, xc_p, xc_c, xc_n, bg_ref,
                  pw_ref, ps_ref, cw_ref, ob_ref, od_ref, *, lat_tiles, tiles_lat_seq, s_len, c_len):
    i = pl.program_id(0)
    is_lat = i < lat_tiles
    j = jnp.where(is_lat, i % tiles_lat_seq, 0)
    n_seq = jnp.where(is_lat, s_len, c_len)
    first = j == 0
    last = (j + 1) * TL == n_seq

    def ext(p_ref, c_ref, n_ref):
        p = jnp.where(first, 0.0, p_ref[...].astype(F32))
        n = jnp.where(last, 0.0, n_ref[...].astype(F32))
        return jnp.concatenate([p, c_ref[...].astype(F32), n], axis=0)

    z = ext(cg_p, cg_c, cg_n) * ext(xc_p, xc_c, xc_n)
    cw = cw_ref[...]
    y = (cw[0:1] * z[HALO - 1:HALO - 1 + TL] + cw[1:2] * z[HALO:HALO + TL]
         + cw[2:3] * z[HALO + 1:HALO + 1 + TL])
    od_ref[...] = (bg_ref[...].astype(F32) * y).astype(BF16)

    u = ext(up_p, up_c, up_n)
    t = j * TL + lax.broadcasted_iota(jnp.int32, (TL, 1), 0)
    ps = ps_ref[...]
    for g, w in enumerate(POOL_WINDOWS):
        sl = slice(POOL_GROUP * g, POOL_GROUP * (g + 1))
        ug = u[:, sl]
        acc = ug[HALO - w // 2:HALO - w // 2 + TL]
        for d in range(-w // 2 + 1, w // 2):
            acc = acc + ug[HALO + d:HALO + d + TL]
        cnt = (jnp.minimum(t - w // 2 + w, n_seq) - jnp.maximum(t - w // 2, 0)).astype(F32)
        pooled = acc / cnt - ug[HALO:HALO + TL]
        mixed = _dot(pooled.astype(BF16), pw_ref[g])
        ob_ref[:, sl] = (mixed * ps[:, sl]).astype(BF16)


def _local(proj, pool_w, pool_scale, conv_w, nb, s_len, c_len):
    r = proj.shape[0]
    lat_tiles = nb * s_len // TL
    hpt = TL // HALO
    nhalo = r // HALO

    def cur(off):
        return pl.BlockSpec((TL, 512), lambda i: (i, off // 512))

    def prev(off):
        return pl.BlockSpec((HALO, 512), lambda i: (jnp.maximum(i * hpt - 1, 0), off // 512))

    def nxt(off):
        return pl.BlockSpec((HALO, 512), lambda i: (jnp.minimum((i + 1) * hpt, nhalo - 1), off // 512))

    specs = []
    for off in (OFF_UP, OFF_CG, OFF_XC):
        specs += [prev(off), cur(off), nxt(off)]
    specs += [cur(OFF_BG),
              pl.BlockSpec((4, POOL_GROUP, POOL_GROUP), lambda i: (0, 0, 0)),
              pl.BlockSpec((1, 512), lambda i: (0, 0)),
              pl.BlockSpec((CONV_K, 512), lambda i: (0, 0))]
    return pl.pallas_call(
        functools.partial(_local_kernel, lat_tiles=lat_tiles, tiles_lat_seq=s_len // TL,
                          s_len=s_len, c_len=c_len),
        grid=(r // TL,),
        in_specs=specs,
        out_specs=[pl.BlockSpec((TL, 512), lambda i: (i, 0)), pl.BlockSpec((TL, 512), lambda i: (i, 0))],
        out_shape=[jax.ShapeDtypeStruct((r, 512), BF16), jax.ShapeDtypeStruct((r, 512), BF16)],
        compiler_params=_cparams(("parallel",)),
        name="local_mix",
    )(*([proj] * 10), pool_w, pool_scale, conv_w)


def _merge_kernel(oa_ref, ob_ref, oc_ref, od_ref, gate_ref, wb_ref, wo_ref, x_ref, gm_ref, g_ref, o_ref):
    merged = None
    for k, o in enumerate((oa_ref, ob_ref, oc_ref, od_ref)):
        proj = _dot(o[...], wb_ref[k])
        gk = jax.nn.sigmoid(gate_ref[:, 1024 * k:1024 * (k + 1)].astype(F32))
        merged = gk * proj if merged is None else merged + gk * proj
    y = _dot(merged.astype(BF16), wo_ref[...])
    o_ref[...] = x_ref[...] + gm_ref[...] * _rms(y, g_ref[...])


def _merge(oa, ob, oc, od, proj, wb, wo, xall, gate_mod, g, n_rows, tiles_per_batch):
    d = xall.shape[1]
    tm = TM
    tpb = tiles_per_batch * (TM // tm)
    row = lambda w: pl.BlockSpec((tm, w), lambda i: (i, 0))
    return pl.pallas_call(
        _merge_kernel,
        grid=(n_rows // tm,),
        in_specs=[row(512), row(512), row(512), row(512), row(4096),
                  pl.BlockSpec((4, 512, d), lambda i: (0, 0, 0)),
                  pl.BlockSpec((d, d), lambda i: (0, 0)),
                  row(d),
                  pl.BlockSpec((None, 1, d), lambda i: (i // tpb, 0, 0)),
                  pl.BlockSpec((1, d), lambda i: (0, 0))],
        out_specs=row(d),
        out_shape=jax.ShapeDtypeStruct((n_rows, d), F32),
        compiler_params=_cparams(("parallel",)),
        name="merge",
    )(oa, ob, oc, od, proj, wb, wo, xall, gate_mod, g.reshape(1, d))


def _ffn_kernel(x_ref, g2_ref, sh_ref, sc_ref, gm_ref, g3_ref, w1_ref, w3_ref, w2_ref, o_ref, *, n_split):
    x = x_ref[...]
    h = (_rms(x, g2_ref[...]) * (1.0 + sc_ref[...]) + sh_ref[...]).astype(BF16)
    tf = w1_ref.shape[1] // n_split
    acc = None
    for j in range(n_split):
        sl = slice(tf * j, tf * (j + 1))
        a = _dot(h, w1_ref[:, sl])
        b = _dot(h, w3_ref[:, sl])
        y = _dot((a * jax.nn.sigmoid(a) * b).astype(BF16), w2_ref[sl, :])
        acc = y if acc is None else acc + y
    o_ref[...] = x + gm_ref[...] * _rms(acc, g3_ref[...])


def _ffn(xall, g2, shift, scale, gate_mod, g3, w1, w3, w2, tiles_per_batch):
    r, d = xall.shape
    dff = w1.shape[1]
    mod = pl.BlockSpec((None, 1, d), lambda i: (i // tiles_per_batch, 0, 0))
    vec = pl.BlockSpec((1, d), lambda i: (0, 0))
    resident = lambda shape: pl.BlockSpec(shape, lambda i: (0, 0), pipeline_mode=pl.Buffered(1))
    return pl.pallas_call(
        functools.partial(_ffn_kernel, n_split=2),
        grid=(r // TM,),
        in_specs=[pl.BlockSpec((TM, d), lambda i: (i, 0)), vec, mod, mod, mod, vec,
                  resident((d, dff)), resident((d, dff)), resident((dff, d))],
        out_specs=pl.BlockSpec((TM, d), lambda i: (i, 0)),
        out_shape=jax.ShapeDtypeStruct((r, d), F32),
        compiler_params=_cparams(("parallel",)),
        name="ffn_dense",
    )(xall, g2.reshape(1, d), shift, scale, gate_mod, g3.reshape(1, d), w1, w3, w2)


def _router_kernel(x_ref, g2_ref, sh_ref, sc_ref, rt_ref, t_ref, idx_ref, w_ref):
    t = _rms(x_ref[...], g2_ref[...]) * (1.0 + sc_ref[...]) + sh_ref[...]
    t_ref[...] = t
    logits = lax.dot_general(rt_ref[...], t, (((1,), (1,)), ((), ())), preferred_element_type=F32,
                             precision=lax.Precision.HIGHEST)
    e = lax.broadcasted_iota(jnp.int32, logits.shape, 0).astype(F32)
    m1 = jnp.max(logits, axis=0, keepdims=True)
    i1 = jnp.min(jnp.where(logits == m1, e, float(N_EXPERTS)), axis=0, keepdims=True)
    rest = jnp.where(e == i1, -jnp.inf, logits)
    m2 = jnp.max(rest, axis=0, keepdims=True)
    i2 = jnp.min(jnp.where(rest == m2, e, float(N_EXPERTS)), axis=0, keepdims=True)
    ex = jnp.exp(m2 - m1)
    w1 = 1.0 / (1.0 + ex)
    idx_ref[0:1, :] = i1.astype(jnp.int32)
    idx_ref[1:2, :] = i2.astype(jnp.int32)
    w_ref[0:1, :] = w1
    w_ref[1:2, :] = ex * w1


def _router(x, g2, shift, scale, router_t, tiles_per_batch):
    t_rows, d = x.shape
    mod = pl.BlockSpec((None, 1, d), lambda i: (i // tiles_per_batch, 0, 0))
    return pl.pallas_call(
        _router_kernel,
        grid=(t_rows // TM,),
        in_specs=[pl.BlockSpec((TM, d), lambda i: (i, 0)),
                  pl.BlockSpec((1, d), lambda i: (0, 0)), mod, mod,
                  pl.BlockSpec((N_EXPERTS, d), lambda i: (0, 0))],
        out_specs=[pl.BlockSpec((TM, d), lambda i: (i, 0)),
                   pl.BlockSpec((TOP_K, TM), lambda i: (0, i)),
                   pl.BlockSpec((TOP_K, TM), lambda i: (0, i))],
        out_shape=[jax.ShapeDtypeStruct((t_rows, d), F32),
                   jax.ShapeDtypeStruct((TOP_K, t_rows), jnp.int32),
                   jax.ShapeDtypeStruct((TOP_K, t_rows), F32)],
        compiler_params=_cparams(("parallel",)),
        name="router",
    )(x, g2.reshape(1, d), shift, scale, router_t)


def _row_gather_start(ids_ref, src_ref, buf_ref, sem, n):
    def issue(i, c):
        for p in range(2):
            r = 2 * i + p
            pltpu.make_async_copy(src_ref.at[pl.ds(ids_ref[0, r], 1)], buf_ref.at[pl.ds(r, 1)],
                                  sem).start(priority=p)
        return c

    lax.fori_loop(0, n // 2, issue, 0, unroll=4)


def _row_gather_wait(src_ref, buf_ref, sem, n):
    pltpu.make_async_copy(src_ref.at[pl.ds(0, n)], buf_ref, sem).wait()


def _moe_ffn_kernel(eid_ref, nused_ref, ids_ref, ids_next_ref, t_ref, w1_ref, w3_ref, w2_ref, o_ref,
                    xbuf_ref, xs_ref, acc_ref, sems):
    i = pl.program_id(0)
    f = pl.program_id(1)
    n_f = pl.num_programs(1)
    slot = lax.rem(i, 2)

    def row_copy(ids, r, s):
        return pltpu.make_async_copy(t_ref.at[pl.ds(ids[0, r], 1)], xbuf_ref.at[s, pl.ds(r, 1)], sems.at[s])

    def all_rows(s):
        return pltpu.make_async_copy(t_ref.at[pl.ds(0, TM)], xbuf_ref.at[s], sems.at[s])

    @pl.when((i == 0) & (f == 0))
    def _():
        def issue(r, c):
            row_copy(ids_ref, r, 0).start()
            return c

        lax.fori_loop(0, TM, issue, 0)

    @pl.when(i < nused_ref[0])
    def _():
        @pl.when(f == 0)
        def _():
            all_rows(slot).wait()
            xs_ref[...] = xbuf_ref[slot].astype(BF16)

        for step in range(MOE_F_STEPS):
            @pl.when(f == step)
            def _():
                for r in range(step * (TM // MOE_F_STEPS), (step + 1) * (TM // MOE_F_STEPS)):
                    row_copy(ids_next_ref, r, 1 - slot).start()

        x = xs_ref[...]
        a = _dot(x, w1_ref[...])
        b = _dot(x, w3_ref[...])
        y = _dot((a * jax.nn.sigmoid(a) * b).astype(BF16), w2_ref[...])

        @pl.when(f == 0)
        def _():
            acc_ref[...] = y

        @pl.when(f == n_f - 1)
        def _():
            o_ref[...] = acc_ref[...] + y

    @pl.when((i + 1 == nused_ref[0]) & (f == n_f - 1))
    def _():
        all_rows(1 - slot).wait()

    @pl.when((i >= nused_ref[0]) & (f == n_f - 1))
    def _():
        o_ref[...] = jnp.zeros_like(o_ref)


def _moe_ffn(t, slot_tok, tile_eid, n_used, w1, w3, w2):
    n_tiles = slot_tok.shape[0]
    d = t.shape[1]
    dff = w1.shape[2]
    tf = dff // MOE_F_STEPS
    assert MOE_F_STEPS == 2, "the kernel stores the first step's partial sum and adds it in the second"
    wf = lambda i, f, nu: jnp.where(i < nu[0], f, MOE_F_STEPS - 1)
    ids = lambda nxt: pl.BlockSpec((None, 1, TM),
                                   lambda i, f, eid, nu: (jnp.minimum(i + nxt, n_tiles - 1), 0, 0),
                                   memory_space=pltpu.SMEM)
    grid_spec = pltpu.PrefetchScalarGridSpec(
        num_scalar_prefetch=2,
        grid=(n_tiles, MOE_F_STEPS),
        in_specs=[ids(0), ids(1),
                  pl.BlockSpec(memory_space=pl.ANY),
                  pl.BlockSpec((None, d, tf), lambda i, f, eid, nu: (eid[i], 0, wf(i, f, nu))),
                  pl.BlockSpec((None, d, tf), lambda i, f, eid, nu: (eid[i], 0, wf(i, f, nu))),
                  pl.BlockSpec((None, tf, d), lambda i, f, eid, nu: (eid[i], wf(i, f, nu), 0))],
        out_specs=pl.BlockSpec((TM, d), lambda i, f, eid, nu: (i, 0)),
        scratch_shapes=[pltpu.VMEM((2, TM, d), F32), pltpu.VMEM((TM, d), BF16), pltpu.VMEM((TM, d), F32),
                        pltpu.SemaphoreType.DMA((2,))])
    return pl.pallas_call(
        _moe_ffn_kernel,
        grid_spec=grid_spec,
        out_shape=jax.ShapeDtypeStruct((n_tiles * TM, d), F32),
        compiler_params=_cparams(("arbitrary", "arbitrary")),
        name="moe_ffn",
    )(tile_eid, n_used, slot_tok, slot_tok, t, w1, w3, w2)


def _combine_kernel(p0_ref, p1_ref, p0n_ref, p1n_ref, ys_ref, w_ref, x_ref, gm_ref, g3_ref, o_ref,
                    b0_ref, b1_ref, sems):
    i = pl.program_id(0)
    slot = lax.rem(i, 2)

    def start(s, ids0, ids1):
        _row_gather_start(ids0, ys_ref, b0_ref.at[s], sems.at[0, s], TM)
        _row_gather_start(ids1, ys_ref, b1_ref.at[s], sems.at[1, s], TM)

    def wait(s):
        _row_gather_wait(ys_ref, b0_ref.at[s], sems.at[0, s], TM)
        _row_gather_wait(ys_ref, b1_ref.at[s], sems.at[1, s], TM)

    @pl.when(i == 0)
    def _():
        start(0, p0_ref, p1_ref)

    start(1 - slot, p0n_ref, p1n_ref)
    wait(slot)
    w = w_ref[...]
    y = w[:, 0:1] * b0_ref[slot] + w[:, 1:2] * b1_ref[slot]
    o_ref[...] = x_ref[...] + gm_ref[...] * _rms(y, g3_ref[...])

    @pl.when(i == pl.num_programs(0) - 1)
    def _():
        wait(1 - slot)


def _combine(pos0, pos1, ys, wcol, x, gate_mod, g3, tiles_per_batch):
    t_rows, d = x.shape
    n_tiles = t_rows // TM
    ids = lambda nxt: pl.BlockSpec((None, 1, TM), lambda i: (jnp.minimum(i + nxt, n_tiles - 1), 0, 0),
                                   memory_space=pltpu.SMEM)
    return pl.pallas_call(
        _combine_kernel,
        grid=(n_tiles,),
        in_specs=[ids(0), ids(0), ids(1), ids(1),
                  pl.BlockSpec(memory_space=pl.ANY),
                  pl.BlockSpec((TM, TOP_K), lambda i: (i, 0)),
                  pl.BlockSpec((TM, d), lambda i: (i, 0)),
                  pl.BlockSpec((None, 1, d), lambda i: (i // tiles_per_batch, 0, 0)),
                  pl.BlockSpec((1, d), lambda i: (0, 0))],
        out_specs=pl.BlockSpec((TM, d), lambda i: (i, 0)),
        out_shape=jax.ShapeDtypeStruct((t_rows, d), F32),
        scratch_shapes=[pltpu.VMEM((2, TM, d), F32), pltpu.VMEM((2, TM, d), F32),
                        pltpu.SemaphoreType.DMA((2, 2))],
        compiler_params=_cparams(("arbitrary",)),
        name="moe_combine",
    )(pos0, pos1, pos0, pos1, ys, wcol, x, gate_mod, g3.reshape(1, d))


def _routing_tables(top_i, t_rows):
    n_assign = TOP_K * t_rows
    n_tiles = n_assign // TM + N_EXPERTS
    e_flat = top_i.reshape(n_assign)
    onehot = (e_flat[:, None] == jnp.arange(N_EXPERTS)[None, :]).astype(jnp.int32)
    csum = jnp.cumsum(onehot, axis=0)
    counts = csum[-1]
    rank = jnp.sum((csum - onehot) * onehot, axis=1)
    tiles_e = (counts + TM - 1) // TM
    tile_end = jnp.cumsum(tiles_e)
    tile_start = tile_end - tiles_e
    slot = (tile_start * TM)[e_flat] + rank
    tok = jnp.tile(jnp.arange(t_rows, dtype=jnp.int32), TOP_K)
    slot_tok = jnp.zeros((n_tiles * TM,), jnp.int32).at[slot].set(tok)
    n_used = tile_end[-1]
    tile_ids = jnp.arange(n_tiles)
    tile_eid = jnp.sum(tile_ids[:, None] >= tile_end[None, :], axis=1)
    last_eid = jnp.sum(n_used - 1 >= tile_end)
    tile_eid = jnp.where(tile_ids < n_used, tile_eid, last_eid).astype(jnp.int32)
    pos = slot.reshape(TOP_K, t_rows).astype(jnp.int32)
    return (slot_tok.reshape(n_tiles, 1, TM), tile_eid, n_used.reshape(1).astype(jnp.int32),
            pos[0].reshape(t_rows // TM, 1, TM), pos[1].reshape(t_rows // TM, 1, TM))


def _prep_w_in(w):
    d = w.shape[0]
    w = w.astype(BF16)
    q_a, k_a, v_a, u_p, c_q, c_kv, k_r, b_g, c_g, x_c, gate = jnp.split(w, IN_SPLITS, axis=1)
    z = lambda n: jnp.zeros((d, n), w.dtype)
    kr_e, kr_o = k_r[:, 0::2], k_r[:, 1::2]
    cols = [gate, q_a * NA_SCALE, k_a, v_a, u_p, b_g, c_g, x_c, c_q, z(128), c_kv,
            z(64), kr_e, kr_o, -kr_o, kr_e, z(128)]
    return jnp.concatenate(cols, axis=1).astype(BF16)


def _prep_w_uq(w_uq):
    wq = w_uq.reshape(MLA_Q_LORA, MLA_HEADS, MLA_NOPE + MLA_ROPE)
    nope, r = wq[..., :MLA_NOPE], wq[..., MLA_NOPE:]
    re, ro = r[..., 0::2], r[..., 1::2]
    ext = jnp.concatenate([nope, re, ro, -ro, re], axis=-1).reshape(MLA_Q_LORA, MLA_HEADS * 128)
    return jnp.pad(ext, ((0, 512 - MLA_Q_LORA), (0, 0))).T.astype(BF16)


def _prep_w_ukv(w_ukv):
    wkv = w_ukv.reshape(MLA_KV_LORA, MLA_HEADS, MLA_NOPE + MLA_V)
    zeros = jnp.zeros((MLA_KV_LORA, MLA_HEADS, 64), w_ukv.dtype)
    kn = jnp.concatenate([wkv[..., :MLA_NOPE], zeros], axis=-1).reshape(MLA_KV_LORA, MLA_HEADS * 128)
    vt = jnp.concatenate([wkv[..., MLA_NOPE:], zeros[..., :VROWS - MLA_V]], axis=-1)
    return kn.astype(BF16), vt.reshape(MLA_KV_LORA, MLA_HEADS * VROWS).T.astype(BF16)


def _rope_tables(nb, s_len, n_ctx_rows):
    pos = jnp.arange(s_len)
    row = (pos // GRID_W).astype(F32)
    col = (pos % GRID_W).astype(F32)
    n_pairs = MLA_ROPE // 4
    inv_freq = ROPE_BASE ** (-jnp.arange(n_pairs, dtype=F32) / n_pairs)
    ang = jnp.concatenate([row[:, None] * inv_freq, col[:, None] * inv_freq], axis=-1)
    cos = jnp.concatenate([jnp.tile(jnp.cos(ang), (nb, 1)), jnp.ones((n_ctx_rows, 16), F32)], axis=0)
    sin = jnp.concatenate([jnp.tile(jnp.sin(ang), (nb, 1)), jnp.zeros((n_ctx_rows, 16), F32)], axis=0)
    r = cos.shape[0]
    cos2 = jnp.concatenate([cos, cos], axis=1)
    sin2 = jnp.concatenate([sin, sin], axis=1)
    pad = lambda t, lead: jnp.concatenate([lead, t, jnp.zeros((r, 32), F32)], axis=1)
    return (pad(cos2, jnp.ones((r, 64), F32)), pad(sin2, jnp.zeros((r, 64), F32)), cos2.T, sin2.T)


def kernel(x, c, ctx, c_ctx, w_ada, b_ada, g_norm, w_in, na_rpb, pool_w, pool_scale, mla_g_q, mla_w_uq,
           mla_g_kv, mla_w_ukv, conv_w, w_branch, w_out, ffn_w1, ffn_w3, ffn_w2, moe_router, moe_w1,
           moe_w3, moe_w2):
    nb, s_len, d = x.shape
    c_len = ctx.shape[1]
    depth = w_in.shape[0]
    t_rows = nb * s_len
    n_ctx = nb * c_len
    assert s_len % TM == 0 and n_ctx == TM and s_len % GRID_W == 0 and t_rows % c_len == 0
    assert c_len == KC and c_len == TL
    tpb = s_len // TM
    rows = s_len // GRID_W

    xall = jnp.concatenate([x.reshape(t_rows, d), ctx.reshape(n_ctx, d)], axis=0)
    c8 = jnp.zeros((8, d), F32).at[:nb].set(c).at[nb].set(c_ctx)
    cos_t, sin_t, cos_tt, sin_tt = _rope_tables(nb, s_len, n_ctx)

    for l in range(depth):
        last = l == depth - 1
        mod = _adaln(c8, w_ada[l], b_ada[l]).reshape(8, 6, 1, d)
        mods = [mod[:, k] for k in range(6)]

        proj = _inproj(xall, g_norm[l, 0], mods[0], mods[1], _prep_w_in(w_in[l]), tpb)

        wkn, wvt = _prep_w_ukv(mla_w_ukv[l])
        gq = jnp.pad(mla_g_q[l], (0, 512 - MLA_Q_LORA)).reshape(1, 512)
        qt_m, k_m, vt_m = _mla_prep(proj, gq, mla_g_kv[l].reshape(1, MLA_KV_LORA), _prep_w_uq(mla_w_uq[l]),
                                    wkn, wvt, cos_t, sin_t, cos_tt, sin_tt)
        n_rows = t_rows if last else t_rows + n_ctx
        o_init = None if last else jnp.zeros((n_rows, 512), BF16)
        o_c = _mla_attn(qt_m, k_m, vt_m, nb, s_len, c_len, o_init)
        o_a = _na_attn(proj, _na_bias_table(na_rpb[l]), nb, s_len, c_len, o_init)
        o_b, o_d = _local(proj, pool_w[l].astype(BF16), pool_scale[l].reshape(1, 512), conv_w[l],
                          nb, s_len, c_len)

        wb = w_branch[l].astype(BF16)
        wo = w_out[l].astype(BF16)
        if not last:
            ctx_blk = t_rows // c_len
            o_c = _mla_ctx_attn(qt_m, k_m, vt_m, o_c, nb, t_rows, c_len)
            o_a = _ctx_attn(proj, proj, proj, o_a, nb, c_len, ctx_blk, 64, OFF_QA, OFF_KA, OFF_VA, "na_ctx_attn")
        xall = _merge(o_a, o_b, o_c, o_d, proj, wb, wo, xall, mods[2], g_norm[l, 1], n_rows, tpb)

        if l % 2 == 0:
            j = l // 2
            xall = _ffn(xall, g_norm[l, 2], mods[3], mods[4], mods[5], g_norm[l, 3],
                        ffn_w1[j].astype(BF16), ffn_w3[j].astype(BF16), ffn_w2[j].astype(BF16), tpb)
        else:
            j = l // 2
            assert last, "context rows are not routed through the experts"
            t_f32, top_i, top_w = _router(xall, g_norm[l, 2], mods[3], mods[4], moe_router[j].T, tpb)
            slot_tok, tile_eid, n_used, pos0, pos1 = _routing_tables(top_i, t_rows)
            ys = _moe_ffn(t_f32, slot_tok, tile_eid, n_used, moe_w1[j].astype(BF16), moe_w3[j].astype(BF16),
                          moe_w2[j].astype(BF16))
            xall = _combine(pos0, pos1, ys, top_w.T, xall, mods[5], g_norm[l, 3], tpb)

    return xall[:t_rows].reshape(nb, s_len, d)
```
